```python
import math
import jax, jax.numpy as jnp
from jax import lax
import numpy as np

D_MODEL = 1024
BATCH = 2
SEQ = 8192
DEPTH = 1

RWKV_HEAD_DIM = 64
RWKV_DIM = D_MODEL // 2
RWKV_HEADS = RWKV_DIM // RWKV_HEAD_DIM
DECAY_LORA = 64
AAA_LORA = 64
GN_EPS = 64e-5

NSA_HEAD_DIM = 64
NSA_DIM = D_MODEL // 2
NSA_Q_HEADS = NSA_DIM // NSA_HEAD_DIM
NSA_KV_GROUPS = 2
NSA_GROUP_SIZE = NSA_Q_HEADS // NSA_KV_GROUPS
NSA_KV_DIM = NSA_KV_GROUPS * NSA_HEAD_DIM
CMP_BLOCK = 32
CMP_STRIDE = 16
CMP_HIDDEN = 256
SEL_BLOCK = 64
N_SELECT = 16
WINDOW = 512
Q_BLOCK = 128

ROPE_THETA = 500000.0
ROPE_DIM = NSA_HEAD_DIM // 4

MEM_LEN = 256
X_HEADS = 4
X_HEAD_DIM = D_MODEL // X_HEADS

D_FF = 2816
CONV_WIDTH = 3

LN_EPS = 1e-5
ALPHA = (2 * DEPTH) ** 0.25
BETA = (8 * DEPTH) ** -0.25
NEG = -1e30
BIG = 1e30

RWKV_COLS = 3 * RWKV_DIM + DECAY_LORA + AAA_LORA
NSA_COLS = NSA_DIM + 6 * NSA_KV_DIM + 3 * NSA_Q_HEADS
GATE_COLS = 2 * D_MODEL
N_IN = RWKV_COLS + NSA_COLS + GATE_COLS

kernel_name = 'hybrid_rwkv7_nsa_deepnorm_layer'


def layer_norm(x, g, b):
    xf = x.astype(jnp.float32)
    mu = jnp.mean(xf, axis=-1, keepdims=True)
    var = jnp.mean(jnp.square(xf - mu), axis=-1, keepdims=True)
    return ((xf - mu) * lax.rsqrt(var + LN_EPS) * g + b).astype(x.dtype)


def partial_rope(x, pos):
    half = ROPE_DIM // 2
    inv = ROPE_THETA ** (-jnp.arange(half, dtype=jnp.float32) * 2.0 / ROPE_DIM)
    ang = pos.astype(jnp.float32)[:, None] * inv[None, :]
    cos = jnp.cos(ang)[None, :, None, :]
    sin = jnp.sin(ang)[None, :, None, :]
    xr = x[..., :ROPE_DIM].astype(jnp.float32)
    x1, x2 = xr[..., :half], xr[..., half:]
    rot = jnp.concatenate([x1 * cos - x2 * sin, x2 * cos + x1 * sin], axis=-1).astype(x.dtype)
    return jnp.concatenate([rot, x[..., ROPE_DIM:]], axis=-1)


def token_shift(p):
    return jnp.pad(p[:, :-1], ((0, 0), (1, 0), (0, 0)))


def rwkv7_time_mix(p, mu, w0, w2, a0, a2, k_k, k_a, r_k, gn_g, gn_b):
    B, S, _ = p.shape
    H, N, C = RWKV_HEADS, RWKV_HEAD_DIM, RWKV_DIM
    p = p + (token_shift(p) - p) * mu
    r, k, v = p[..., :C], p[..., C:2 * C], p[..., 2 * C:3 * C]
    wl = p[..., 3 * C:3 * C + DECAY_LORA]
    al = p[..., 3 * C + DECAY_LORA:]
    w_log = -jax.nn.softplus(-(w0 + jnp.tanh(wl) @ w2)) - 0.5
    decay = jnp.exp(-jnp.exp(w_log.astype(jnp.float32)))
    a = jax.nn.sigmoid(a0 + al @ a2).astype(jnp.float32)
    heads = lambda t: t.astype(jnp.float32).reshape(B, S, H, N)
    kk = heads(k * k_k)
    kk = kk / jnp.maximum(jnp.sqrt(jnp.sum(kk * kk, axis=-1, keepdims=True)), 1e-12)
    k = k.astype(jnp.float32) * (1.0 + (a - 1.0) * k_a)
    r4, k4, v4, a4, w4 = heads(r), heads(k), heads(v), heads(a), heads(decay)
    erase = -kk
    refill = kk * a4

    def step(state, inp):
        r_t, w_t, k_t, v_t, a_t, b_t = inp
        sa = jnp.einsum('bhvk,bhk->bhv', state, a_t)
        state = (state * w_t[:, :, None, :] + sa[..., None] * b_t[:, :, None, :]
                 + v_t[..., None] * k_t[:, :, None, :])
        return state, jnp.einsum('bhvk,bhk->bhv', state, r_t)

    tm = lambda t: jnp.moveaxis(t, 1, 0)
    state0 = jnp.zeros((B, H, N, N), jnp.float32)
    _, y = lax.scan(step, state0, (tm(r4), tm(w4), tm(k4), tm(v4), tm(erase), tm(refill)))
    y = jnp.moveaxis(y, 0, 1)
    ym = jnp.mean(y, axis=-1, keepdims=True)
    yv = jnp.mean(jnp.square(y - ym), axis=-1, keepdims=True)
    y = ((y - ym) * lax.rsqrt(yv + GN_EPS)).reshape(B, S, C) * gn_g + gn_b
    bonus = jnp.sum(r4 * k4 * r_k, axis=-1, keepdims=True) * v4
    return (y + bonus.reshape(B, S, C)).astype(p.dtype)


def nsa_attention(p, pos, pe_k, pe_v, ck_w1, ck_w2, cv_w1, cv_w2):
    B, S, _ = p.shape
    G, R, dh, Hq = NSA_KV_GROUPS, NSA_GROUP_SIZE, NSA_HEAD_DIM, NSA_Q_HEADS
    q = p[..., :NSA_DIM].reshape(B, S, Hq, dh)
    kv = p[..., NSA_DIM:NSA_DIM + 6 * NSA_KV_DIM].reshape(B, S, 6, G, dh)
    k_c, v_c, k_s, v_s, k_w, v_w = [kv[:, :, i] for i in range(6)]
    gates = jax.nn.sigmoid(p[..., NSA_DIM + 6 * NSA_KV_DIM:].reshape(B, S, Hq, 3))
    q = partial_rope(q, pos)
    k_c, k_s, k_w = partial_rope(k_c, pos), partial_rope(k_s, pos), partial_rope(k_w, pos)
    q = (q * dh ** -0.5).reshape(B, S, G, R, dh).transpose(0, 2, 3, 1, 4)

    n_cmp = (S - CMP_BLOCK) // CMP_STRIDE + 1
    cidx = jnp.arange(n_cmp)[:, None] * CMP_STRIDE + jnp.arange(CMP_BLOCK)[None, :]
    cmp_start, cmp_end = cidx[:, 0], cidx[:, -1]

    def compress(t, pe, w1, w2):
        blk = t[:, cidx] + pe[None, None, :, None, :]
        blk = blk.transpose(0, 3, 1, 2, 4).reshape(B, G, n_cmp, CMP_BLOCK * dh)
        return jax.nn.gelu(blk @ w1) @ w2

    kc = compress(k_c, pe_k, ck_w1, ck_w2)
    vc = compress(v_c, pe_v, cv_w1, cv_w2)

    n_sb = S // SEL_BLOCK
    n_sel = min(N_SELECT, n_sb)
    sb_idx = jnp.arange(n_sb)
    sel_start = sb_idx * SEL_BLOCK
    overlap = ((cmp_start[:, None] <= sel_start[None, :] + SEL_BLOCK - 1)
               & (cmp_end[:, None] >= sel_start[None, :])).astype(jnp.float32)
    ks_blocks = k_s.transpose(0, 2, 1, 3).reshape(B, G, n_sb, SEL_BLOCK, dh)
    vs_blocks = v_s.transpose(0, 2, 1, 3).reshape(B, G, n_sb, SEL_BLOCK, dh)
    gather_blocks = jax.vmap(jax.vmap(lambda blocks, ix: blocks[ix]))

    kw_pad = jnp.pad(k_w.transpose(0, 2, 1, 3), ((0, 0), (0, 0), (WINDOW, 0), (0, 0)))
    vw_pad = jnp.pad(v_w.transpose(0, 2, 1, 3), ((0, 0), (0, 0), (WINDOW, 0), (0, 0)))

    def query_block(qb):
        t0 = qb * Q_BLOCK
        qblk = lax.dynamic_slice_in_dim(q, t0, Q_BLOCK, axis=3)
        tpos = t0 + jnp.arange(Q_BLOCK)
        s_c = jnp.einsum('bgrqd,bgcd->bgrqc', qblk, kc).astype(jnp.float32)
        m_c = cmp_end[None, :] <= tpos[:, None]
        p_c = jax.nn.softmax(jnp.where(m_c, s_c, NEG), axis=-1) * jnp.any(m_c, axis=-1)[:, None]
        o_c = jnp.einsum('bgrqc,bgcd->bgrqd', p_c.astype(vc.dtype), vc)
        imp = jnp.einsum('bgrqc,cj->bgqj', p_c, overlap)
        cur = tpos // SEL_BLOCK
        valid = sel_start[None, :] <= tpos[:, None]
        forced = ((sb_idx[None, :] == 0) | (sb_idx[None, :] == cur[:, None])
                  | (sb_idx[None, :] == cur[:, None] - 1))
        imp = jnp.where(valid, jnp.where(forced, BIG, imp), NEG)
        _, idx = lax.top_k(imp, n_sel)
        ks = gather_blocks(ks_blocks, idx)
        vs = gather_blocks(vs_blocks, idx)
        s_s = jnp.einsum('bgrqd,bgqnkd->bgrqnk', qblk, ks).astype(jnp.float32)
        kpos = idx[..., None] * SEL_BLOCK + jnp.arange(SEL_BLOCK)
        m_s = (kpos <= tpos[None, None, :, None, None])[:, :, None]
        s_s = jnp.where(m_s, s_s, NEG).reshape(B, G, R, Q_BLOCK, n_sel * SEL_BLOCK)
        p_s = jax.nn.softmax(s_s, axis=-1).reshape(B, G, R, Q_BLOCK, n_sel, SEL_BLOCK)
        o_s = jnp.einsum('bgrqnk,bgqnkd->bgrqd', p_s.astype(vs.dtype), vs)
        kw = lax.dynamic_slice_in_dim(kw_pad, t0, Q_BLOCK + WINDOW, axis=2)
        vw = lax.dynamic_slice_in_dim(vw_pad, t0, Q_BLOCK + WINDOW, axis=2)
        kwpos = t0 - WINDOW + jnp.arange(Q_BLOCK + WINDOW)
        diff = tpos[:, None] - kwpos[None, :]
        m_w = (kwpos[None, :] >= 0) & (diff >= 0) & (diff < WINDOW)
        s_w = jnp.einsum('bgrqd,bgkd->bgrqk', qblk, kw).astype(jnp.float32)
        p_w = jax.nn.softmax(jnp.where(m_w, s_w, NEG), axis=-1)
        o_w = jnp.einsum('bgrqk,bgkd->bgrqd', p_w.astype(vw.dtype), vw)
        return jnp.stack([o_c, o_s, o_w], axis=-1)

    o = lax.map(query_block, jnp.arange(S // Q_BLOCK))
    o = o.transpose(1, 0, 4, 2, 3, 5, 6).reshape(B, S, Hq, dh, 3)
    return jnp.einsum('bshdc,bshc->bshd', o, gates.astype(o.dtype)).reshape(B, S, NSA_DIM)


def memory_cross_attention(x, mem, wq, wk, wv, wo):
    B, S, _ = x.shape
    M = mem.shape[1]
    q = (x @ wq).reshape(B, S, X_HEADS, X_HEAD_DIM)
    k = (mem @ wk).reshape(B, M, X_HEADS, X_HEAD_DIM)
    v = (mem @ wv).reshape(B, M, X_HEADS, X_HEAD_DIM)
    s = jnp.einsum('bshd,bmhd->bhsm', q, k).astype(jnp.float32) * X_HEAD_DIM ** -0.5
    pr = jax.nn.softmax(s, axis=-1)
    o = jnp.einsum('bhsm,bmhd->bshd', pr.astype(v.dtype), v).reshape(B, S, D_MODEL)
    return o @ wo


def conv_ffn(x, w_up, conv_w, conv_b, w_down):
    h = x @ w_up
    h = lax.conv_general_dilated(h, conv_w[:, None, :], window_strides=(1,),
                                 padding=[(CONV_WIDTH - 1, 0)],
                                 dimension_numbers=('NWC', 'WIO', 'NWC'),
                                 feature_group_count=h.shape[-1]) + conv_b
    gate, val = h[..., :D_FF], h[..., D_FF:]
    return (jax.nn.silu(gate) * val) @ w_down


def setup_inputs(seed: int = 0) -> dict:
    key = jax.random.key(seed)
    ks = jax.random.split(key, 40)
    nrm = lambda k, shape, s: jax.random.normal(k, shape, jnp.float32) * s
    L = DEPTH
    return {
        'x': nrm(ks[0], (BATCH, SEQ, D_MODEL), 1.0),
        'mem': nrm(ks[1], (BATCH, MEM_LEN, D_MODEL), 1.0),
        'w_in': nrm(ks[2], (L, D_MODEL, N_IN), D_MODEL ** -0.5),
        'rwkv_mu': jax.random.uniform(ks[3], (L, RWKV_COLS), jnp.float32),
        'rwkv_w0': jax.random.uniform(ks[4], (L, RWKV_DIM), jnp.float32, minval=-6.0, maxval=0.0),
        'rwkv_w2': nrm(ks[5], (L, DECAY_LORA, RWKV_DIM), DECAY_LORA ** -0.5),
        'rwkv_a0': nrm(ks[6], (L, RWKV_DIM), 0.1),
        'rwkv_a2': nrm(ks[7], (L, AAA_LORA, RWKV_DIM), AAA_LORA ** -0.5),
        'rwkv_k_k': 1.0 + nrm(ks[8], (L, RWKV_DIM), 0.1),
        'rwkv_k_a': 1.0 + nrm(ks[9], (L, RWKV_DIM), 0.1),
        'rwkv_r_k': nrm(ks[10], (L, RWKV_HEADS, RWKV_HEAD_DIM), 0.1),
        'rwkv_gn_g': 1.0 + nrm(ks[11], (L, RWKV_DIM), 0.01),
        'rwkv_gn_b': nrm(ks[12], (L, RWKV_DIM), 0.01),
        'nsa_pe_k': nrm(ks[13], (L, CMP_BLOCK, NSA_HEAD_DIM), 0.1),
        'nsa_pe_v': nrm(ks[14], (L, CMP_BLOCK, NSA_HEAD_DIM), 0.1),
        'nsa_ck_w1': nrm(ks[15], (L, CMP_BLOCK * NSA_HEAD_DIM, CMP_HIDDEN), (CMP_BLOCK * NSA_HEAD_DIM) ** -0.5),
        'nsa_ck_w2': nrm(ks[16], (L, CMP_HIDDEN, NSA_HEAD_DIM), CMP_HIDDEN ** -0.5),
        'nsa_cv_w1': nrm(ks[17], (L, CMP_BLOCK * NSA_HEAD_DIM, CMP_HIDDEN), (CMP_BLOCK * NSA_HEAD_DIM) ** -0.5),
        'nsa_cv_w2': nrm(ks[18], (L, CMP_HIDDEN, NSA_HEAD_DIM), CMP_HIDDEN ** -0.5),
        'merge_p_a': nrm(ks[19], (L, RWKV_DIM, D_MODEL), RWKV_DIM ** -0.5),
        'merge_p_b': nrm(ks[20], (L, NSA_DIM, D_MODEL), NSA_DIM ** -0.5),
        'mix_w_o': nrm(ks[21], (L, D_MODEL, D_MODEL), BETA * D_MODEL ** -0.5),
        'ln1_g': 1.0 + nrm(ks[22], (L, D_MODEL), 0.01),
        'ln1_b': nrm(ks[23], (L, D_MODEL), 0.01),
        'xa_wq': nrm(ks[24], (L, D_MODEL, D_MODEL), D_MODEL ** -0.5),
        'xa_wk': nrm(ks[25], (L, D_MODEL, D_MODEL), D_MODEL ** -0.5),
        'xa_wv': nrm(ks[26], (L, D_MODEL, D_MODEL), D_MODEL ** -0.5),
        'xa_wo': nrm(ks[27], (L, D_MODEL, D_MODEL), BETA * D_MODEL ** -0.5),
        'ln2_g': 1.0 + nrm(ks[28], (L, D_MODEL), 0.01),
        'ln2_b': nrm(ks[29], (L, D_MODEL), 0.01),
        'ffn_w_up': nrm(ks[30], (L, D_MODEL, 2 * D_FF), D_MODEL ** -0.5),
        'ffn_conv_w': nrm(ks[31], (L, CONV_WIDTH, 2 * D_FF), CONV_WIDTH ** -0.5),
        'ffn_conv_b': nrm(ks[32], (L, 2 * D_FF), 0.01),
        'ffn_w_down': nrm(ks[33], (L, D_FF, D_MODEL), BETA * D_FF ** -0.5),
        'ln3_g': 1.0 + nrm(ks[34], (L, D_MODEL), 0.01),
        'ln3_b': nrm(ks[35], (L, D_MODEL), 0.01),
    }


def reference(x, mem, w_in, rwkv_mu, rwkv_w0, rwkv_w2, rwkv_a0, rwkv_a2, rwkv_k_k, rwkv_k_a,
              rwkv_r_k, rwkv_gn_g, rwkv_gn_b, nsa_pe_k, nsa_pe_v, nsa_ck_w1, nsa_ck_w2,
              nsa_cv_w1, nsa_cv_w2, merge_p_a, merge_p_b, mix_w_o, ln1_g, ln1_b,
              xa_wq, xa_wk, xa_wv, xa_wo, ln2_g, ln2_b, ffn_w_up, ffn_conv_w, ffn_conv_b,
              ffn_w_down, ln3_g, ln3_b):
    S = x.shape[1]
    pos = jnp.arange(S)
    for l in range(DEPTH):
        proj = x @ w_in[l]
        p_rwkv = proj[..., :RWKV_COLS]
        p_nsa = proj[..., RWKV_COLS:RWKV_COLS + NSA_COLS]
        gates = jax.nn.sigmoid(proj[..., RWKV_COLS + NSA_COLS:])
        g_a, g_b = gates[..., :D_MODEL], gates[..., D_MODEL:]
        y_a = rwkv7_time_mix(p_rwkv, rwkv_mu[l], rwkv_w0[l], rwkv_w2[l], rwkv_a0[l], rwkv_a2[l],
                             rwkv_k_k[l], rwkv_k_a[l], rwkv_r_k[l], rwkv_gn_g[l], rwkv_gn_b[l])
        y_b = nsa_attention(p_nsa, pos, nsa_pe_k[l], nsa_pe_v[l], nsa_ck_w1[l], nsa_ck_w2[l],
                            nsa_cv_w1[l], nsa_cv_w2[l])
        mixed = (g_a * (y_a @ merge_p_a[l]) + g_b * (y_b @ merge_p_b[l])) @ mix_w_o[l]
        x = layer_norm(ALPHA * x + mixed, ln1_g[l], ln1_b[l])
        xa = memory_cross_attention(x, mem, xa_wq[l], xa_wk[l], xa_wv[l], xa_wo[l])
        x = layer_norm(ALPHA * x + xa, ln2_g[l], ln2_b[l])
        f = conv_ffn(x, ffn_w_up[l], ffn_conv_w[l], ffn_conv_b[l], ffn_w_down[l])
        x = layer_norm(ALPHA * x + f, ln3_g[l], ln3_b[l])
    return x
```

```python
import functools
import math

import jax
import jax.numpy as jnp
from jax import lax
from jax.experimental import pallas as pl
from jax.experimental.pallas import tpu as pltpu

D_MODEL = 1024
HEAD_DIM = 64
RWKV_DIM = 512
RWKV_HEADS = 8
LORA = 64
RWKV_COLS = 3 * RWKV_DIM + 2 * LORA
GN_EPS = 64e-5
NSA_DIM = 512
NSA_Q_HEADS = 8
NSA_GROUPS = 2
NSA_R = NSA_Q_HEADS // NSA_GROUPS
NSA_KV_DIM = NSA_GROUPS * HEAD_DIM
NSA_COLS = NSA_DIM + 6 * NSA_KV_DIM + 3 * NSA_Q_HEADS
NSA_COLS_PAD = 1408
CMP_BLOCK = 32
CMP_STRIDE = 16
CMP_HIDDEN = 256
SEL_BLOCK = 64
N_SELECT = 16
WINDOW = 512
Q_BLOCK = 128
ROPE_THETA = 500000.0
ROPE_DIM = 16
X_HEADS = 4
X_HEAD_DIM = 256
D_FF = 2816
LN_EPS = 1e-5
DEPTH = 1
ALPHA = (2 * DEPTH) ** 0.25
NEG = -1e30
BIG = 1e30

V7X_LANES = 128
V7X_SUBLANES = 8
V7X_VMEM_LIMIT_BYTES = 56 * 1024 * 1024

HI = lax.Precision.HIGHEST
RWKV_CHUNK = 64


def _params(*sem):
    return pltpu.CompilerParams(dimension_semantics=sem, vmem_limit_bytes=V7X_VMEM_LIMIT_BYTES)


def _full(shape):
    n = len(shape)
    return pl.BlockSpec(shape, lambda *_: (0,) * n)


def _head_ones(width):
    r = lax.broadcasted_iota(jnp.int32, (width, width), 0) // HEAD_DIM
    c = lax.broadcasted_iota(jnp.int32, (width, width), 1) // HEAD_DIM
    return (r == c).astype(jnp.float32)


def _layer_norm(y, g, b):
    mu = jnp.mean(y, axis=-1, keepdims=True)
    d = y - mu
    var = jnp.mean(d * d, axis=-1, keepdims=True)
    return d * lax.rsqrt(var + LN_EPS) * g + b


def _rwkv_prep_kernel(seq_tiles, x_ref, xp_ref, w_ref, mu_ref, w0_ref, w2_ref, a0_ref, a2_ref,
                      kk_ref, ka_ref, rk_ref,
                      r_out, k_out, v_out, lw_out, a_out, b_out, bonus_out):
    i = pl.program_id(0)
    tm = x_ref.shape[0]
    C = RWKV_DIM
    w = w_ref[...]
    p = jnp.dot(x_ref[...].astype(jnp.bfloat16), w, preferred_element_type=jnp.float32)
    xprev = xp_ref[...].astype(jnp.bfloat16)
    pprev = jnp.dot(xprev, w, preferred_element_type=jnp.float32)[V7X_SUBLANES - 1:V7X_SUBLANES, :]
    pprev = jnp.where(i % seq_tiles == 0, 0.0, pprev)
    row = lax.broadcasted_iota(jnp.int32, (tm, 1), 0)
    shifted = jnp.where(row == 0, pprev, pltpu.roll(p, 1, 0))
    p = p + (shifted - p) * mu_ref[...]
    r, k, v = p[:, :C], p[:, C:2 * C], p[:, 2 * C:3 * C]
    wl = p[:, 3 * C:3 * C + LORA]
    al = p[:, 3 * C + LORA:]
    z = -(w0_ref[...] + jnp.dot(jnp.tanh(wl).astype(jnp.bfloat16), w2_ref[...],
                                preferred_element_type=jnp.float32))
    softplus = jnp.maximum(z, 0.0) + jnp.log(1.0 + jnp.exp(-jnp.abs(z)))
    w_log = -softplus - 0.5
    lw_out[...] = -jnp.exp(w_log)
    a = jax.nn.sigmoid(a0_ref[...] + jnp.dot(al.astype(jnp.bfloat16), a2_ref[...],
                                             preferred_element_type=jnp.float32))
    ones = _head_ones(C)
    kk = k * kk_ref[...]
    nrm = jnp.sqrt(jnp.dot(kk * kk, ones, precision=HI, preferred_element_type=jnp.float32))
    kk = kk / jnp.maximum(nrm, 1e-12)
    kmod = k * (1.0 + (a - 1.0) * ka_ref[...])
    bonus = jnp.dot(r * kmod * rk_ref[...], ones, precision=HI,
                    preferred_element_type=jnp.float32) * v
    r_out[...] = r
    k_out[...] = kmod
    v_out[...] = v
    a_out[...] = -kk
    b_out[...] = kk * a
    bonus_out[...] = bonus


def _rwkv_prep(x2d, w_rwkv, mu, w0, w2, a0, a2, k_k, k_a, r_k, seq_len, tm=256):
    M = x2d.shape[0]
    C = RWKV_DIM
    seq_tiles = seq_len // tm
    row = lambda a: a.reshape(1, -1)
    out = jax.ShapeDtypeStruct((M, C), jnp.float32)
    tile = pl.BlockSpec((tm, C), lambda i: (i, 0))
    blocks_per_tile = tm // V7X_SUBLANES
    return pl.pallas_call(
        functools.partial(_rwkv_prep_kernel, seq_tiles),
        grid=(M // tm,),
        in_specs=[
            pl.BlockSpec((tm, D_MODEL), lambda i: (i, 0)),
            pl.BlockSpec((V7X_SUBLANES, D_MODEL), lambda i: (jnp.maximum(i * blocks_per_tile - 1, 0), 0)),
            _full((D_MODEL, RWKV_COLS)), _full((1, RWKV_COLS)), _full((1, C)), _full((LORA, C)),
            _full((1, C)), _full((LORA, C)), _full((1, C)), _full((1, C)), _full((1, C)),
        ],
        out_specs=[tile] * 7,
        out_shape=[out] * 7,
        compiler_params=_params("parallel"),
        name="rwkv_prep",
    )(x2d, x2d, w_rwkv, row(mu), row(w0), w2.astype(jnp.bfloat16), row(a0), a2.astype(jnp.bfloat16),
      row(k_k), row(k_a), row(r_k))


def _tri_inverse(L, T):
    ri = lax.broadcasted_iota(jnp.int32, (T, T), 0)
    ci = lax.broadcasted_iota(jnp.int32, (T, T), 1)
    eye = (ri == ci).astype(jnp.float32)
    dot = lambda a, b: jnp.dot(a, b, precision=HI, preferred_element_type=jnp.float32)
    base = V7X_SUBLANES
    D = jnp.where(ri // base == ci // base, L, 0.0)
    D2 = dot(D, D)
    D4 = dot(D2, D2)
    X = dot(dot(eye + D, eye + D2), eye + D4)
    blk = base
    while blk < T:
        pair = (ri // (2 * blk) == ci // (2 * blk)) & (ri // blk != ci // blk)
        X = X + dot(dot(X, jnp.where(pair, L, 0.0)), X)
        blk *= 2
    return X


def _rwkv_chunk_kernel(r_ref, k_ref, v_ref, lw_ref, a_ref, b_ref, g_out, h_out, rp_out, yp_out):
    T = r_ref.shape[0]
    N = HEAD_DIM
    dot = lambda a, b: jnp.dot(a, b, precision=HI, preferred_element_type=jnp.float32)
    ri = lax.broadcasted_iota(jnp.int32, (T, T), 0)
    ci = lax.broadcasted_iota(jnp.int32, (T, T), 1)
    incl = (ri >= ci)
    strict = (ri > ci)
    eye_n = (lax.broadcasted_iota(jnp.int32, (N, N), 0)
             == lax.broadcasted_iota(jnp.int32, (N, N), 1)).astype(jnp.float32)
    c_all = dot(incl.astype(jnp.float32), lw_ref[...])
    for h in range(r_ref.shape[1] // N):
        sl = slice(h * N, (h + 1) * N)
        lw, c = lw_ref[:, sl], c_all[:, sl]
        r, k, v, a, b = r_ref[:, sl], k_ref[:, sl], v_ref[:, sl], a_ref[:, sl], b_ref[:, sl]
        c_last = c[T - 1:T, :]
        e_neg = jnp.exp(-c)
        e_end = jnp.exp(c_last - c)
        a_t = a * jnp.exp(c - lw)
        r_t = r * jnp.exp(c)
        lhs = jnp.concatenate([a_t, r_t], axis=0)
        rhs = jnp.concatenate([b * e_neg, k * e_neg], axis=0)
        P = lax.dot_general(lhs, rhs, (((1,), (1,)), ((), ())), precision=HI,
                            preferred_element_type=jnp.float32)
        L_ab = jnp.where(strict, P[:T, :T], 0.0)
        L_ak = jnp.where(strict, P[:T, T:], 0.0)
        M_rb = jnp.where(incl, P[T:, :T], 0.0)
        M_rk = jnp.where(incl, P[T:, T:], 0.0)
        t_inv = _tri_inverse(L_ab, T)
        X = dot(t_inv, jnp.concatenate([a_t, dot(L_ak, v)], axis=1))
        ry = dot(M_rb, X)
        rp_out[:, sl] = r_t + ry[:, :N]
        yp_out[:, sl] = ry[:, N:] + dot(M_rk, v)
        bh_t = (b * e_end).T
        kh_t = (k * e_end).T
        gh = dot(bh_t, X)
        g_out[0, :, sl] = eye_n * jnp.exp(c_last) + gh[:, :N]
        h_out[0, :, sl] = gh[:, N:] + dot(kh_t, v)


def _rwkv_chunks(r, k, v, lw, a, b):
    M, C = r.shape
    T = RWKV_CHUNK
    nct = M // T
    hp = 2 * HEAD_DIM
    tile = pl.BlockSpec((T, hp), lambda i, j: (i, j))
    st = pl.BlockSpec((1, HEAD_DIM, hp), lambda i, j: (i, 0, j))
    return pl.pallas_call(
        _rwkv_chunk_kernel,
        grid=(nct, C // hp),
        in_specs=[tile] * 6,
        out_specs=[st, st, tile, tile],
        out_shape=[jax.ShapeDtypeStruct((nct, HEAD_DIM, C), jnp.float32)] * 2
        + [jax.ShapeDtypeStruct((M, C), jnp.float32)] * 2,
        compiler_params=_params("parallel", "parallel"),
        name="rwkv_chunks",
    )(r, k, v, lw, a, b)


def _rwkv_scan_kernel(g_ref, h_ref, rp_ref, yp_ref, bonus_ref, gng_ref, gnb_ref, y_out, state):
    N = HEAD_DIM
    dot = lambda a, b: jnp.dot(a, b, precision=HI, preferred_element_type=jnp.float32)

    @pl.when(pl.program_id(1) == 0)
    def _():
        state[...] = jnp.zeros_like(state)

    ys = []
    for h in range(RWKV_HEADS):
        sl = slice(h * N, (h + 1) * N)
        s0 = state[:, sl]
        ys.append(dot(rp_ref[:, sl], s0) + yp_ref[:, sl])
        state[:, sl] = dot(g_ref[0, :, sl], s0) + h_ref[0, :, sl]
    y = jnp.concatenate(ys, axis=1)
    ones = _head_ones(RWKV_DIM) * (1.0 / N)
    ym = dot(y, ones)
    d = y - ym
    yv = dot(d * d, ones)
    y_out[...] = d * lax.rsqrt(yv + GN_EPS) * gng_ref[...] + gnb_ref[...] + bonus_ref[...]


def _rwkv_scan(g, h, rp, yp, bonus, gn_g, gn_b, batch):
    M, C = rp.shape
    T = RWKV_CHUNK
    nc = M // T // batch
    st = pl.BlockSpec((1, HEAD_DIM, C), lambda bi, ci: (bi * nc + ci, 0, 0))
    tile = pl.BlockSpec((T, C), lambda bi, ci: (bi * nc + ci, 0))
    return pl.pallas_call(
        _rwkv_scan_kernel,
        grid=(batch, nc),
        in_specs=[st, st, tile, tile, tile, _full((1, C)), _full((1, C))],
        out_specs=tile,
        out_shape=jax.ShapeDtypeStruct((M, C), jnp.float32),
        scratch_shapes=[pltpu.VMEM((HEAD_DIM, C), jnp.float32)],
        compiler_params=_params("parallel", "arbitrary"),
        name="rwkv_scan",
    )(g, h, rp, yp, bonus, gn_g.reshape(1, C), gn_b.reshape(1, C))


def _rwkv_time_mix(x2d, w_rwkv, mu, w0, w2, a0, a2, k_k, k_a, r_k, gn_g, gn_b, batch, seq_len):
    r, k, v, lw, a, b, bonus = _rwkv_prep(x2d, w_rwkv, mu, w0, w2, a0, a2, k_k, k_a, r_k, seq_len)
    g, h, rp, yp = _rwkv_chunks(r, k, v, lw, a, b)
    return _rwkv_scan(g, h, rp, yp, bonus, gn_g, gn_b, batch)


def _rope_tables(seq_len):
    half = ROPE_DIM // 2
    inv = ROPE_THETA ** (-jnp.arange(half, dtype=jnp.float32) * 2.0 / ROPE_DIM)
    ang = jnp.arange(seq_len).astype(jnp.float32)[:, None] * inv[None, :]
    cos, sin = jnp.cos(ang), jnp.sin(ang)
    pad = jnp.zeros((seq_len, HEAD_DIM - ROPE_DIM), jnp.float32)
    zero = jnp.zeros_like(sin)
    c = jnp.concatenate([cos, cos, pad + 1.0], axis=1)
    s_lo = jnp.concatenate([-sin, zero, pad], axis=1)
    s_hi = jnp.concatenate([zero, sin, pad], axis=1)
    two = lambda t: jnp.concatenate([t, t], axis=1)
    return two(c), two(s_lo), two(s_hi)


def _rope_pair(x, c, s_lo, s_hi):
    return x * c + pltpu.roll(x, V7X_LANES - ROPE_DIM // 2, 1) * s_lo + pltpu.roll(x, ROPE_DIM // 2, 1) * s_hi


def _nsa_prep_kernel(x_ref, w_ref, c_ref, slo_ref, shi_ref,
                     q_out, kc_out, vc_out, ks_out, vs_out, kw_out, vw_out, gate_out):
    p = jnp.dot(x_ref[...].astype(jnp.bfloat16), w_ref[...], preferred_element_type=jnp.float32)
    c, s_lo, s_hi = c_ref[...], slo_ref[...], shi_ref[...]
    L = V7X_LANES
    rope = lambda t: _rope_pair(t, c, s_lo, s_hi)
    for j in range(NSA_DIM // L):
        q_out[:, j * L:(j + 1) * L] = rope(p[:, j * L:(j + 1) * L]) * (HEAD_DIM ** -0.5)
    kv = lambda i: p[:, NSA_DIM + i * L:NSA_DIM + (i + 1) * L]
    kc_out[...] = rope(kv(0))
    vc_out[...] = kv(1)
    ks_out[...] = rope(kv(2)).astype(ks_out.dtype)
    vs_out[...] = kv(3).astype(vs_out.dtype)
    kw_out[...] = rope(kv(4)).astype(kw_out.dtype)
    vw_out[...] = kv(5).astype(vw_out.dtype)
    gate_out[...] = jax.nn.sigmoid(kv(6))


def _nsa_prep(x2d, w_nsa, seq_len, tm=256):
    M = x2d.shape[0]
    L = V7X_LANES
    seq_tiles = seq_len // tm
    tabs = _rope_tables(seq_len)
    tab_spec = pl.BlockSpec((tm, L), lambda i: (i % seq_tiles, 0))
    narrow = pl.BlockSpec((tm, L), lambda i: (i, 0))
    f32, bf16 = jnp.float32, jnp.bfloat16
    sds = lambda w, dt: jax.ShapeDtypeStruct((M, w), dt)
    return pl.pallas_call(
        _nsa_prep_kernel,
        grid=(M // tm,),
        in_specs=[pl.BlockSpec((tm, D_MODEL), lambda i: (i, 0)), _full((D_MODEL, NSA_COLS_PAD)),
                  tab_spec, tab_spec, tab_spec],
        out_specs=[pl.BlockSpec((tm, NSA_DIM), lambda i: (i, 0))] + [narrow] * 7,
        out_shape=[sds(NSA_DIM, f32), sds(L, f32), sds(L, f32), sds(L, bf16), sds(L, bf16),
                   sds(L, bf16), sds(L, bf16), sds(L, f32)],
        compiler_params=_params("parallel"),
        name="nsa_prep",
    )(x2d, w_nsa, *tabs)


def _gelu_tanh(x):
    return 0.5 * x * (1.0 + jnp.tanh(math.sqrt(2.0 / math.pi) * (x + 0.044715 * x * x * x)))


def _nsa_compress_kernel(xk_ref, xv_ref, pek_ref, pev_ref, kw1_ref, kw2_ref, vw1_ref, vw2_ref,
                         kc_out, vc_out):
    half = CMP_STRIDE * HEAD_DIM

    def mlp(x, pe, w1_ref, w2_ref):
        lo = jnp.dot((x + pe[:, :half]).astype(jnp.bfloat16), w1_ref[:half, :],
                     preferred_element_type=jnp.float32)
        hi = jnp.dot((x + pe[:, half:]).astype(jnp.bfloat16), w1_ref[half:, :],
                     preferred_element_type=jnp.float32)
        n = x.shape[0]
        pre = lo + pltpu.roll(hi, n - 1, 0)
        return jnp.dot(_gelu_tanh(pre).astype(jnp.bfloat16), w2_ref[...],
                       preferred_element_type=jnp.float32)

    kc_out[0, 0] = mlp(xk_ref[0, 0], pek_ref[...], kw1_ref, kw2_ref).astype(kc_out.dtype)
    vc_out[0, 0] = mlp(xv_ref[0, 0], pev_ref[...], vw1_ref, vw2_ref).astype(vc_out.dtype)


def _nsa_compress(xk, xv, pe_k, pe_v, ck_w1, ck_w2, cv_w1, cv_w2):
    B, G, NC, W = xk.shape
    bf16 = jnp.bfloat16
    xin = pl.BlockSpec((1, 1, NC, W), lambda b, g: (b, g, 0, 0))
    xout = pl.BlockSpec((1, 1, NC, HEAD_DIM), lambda b, g: (b, g, 0, 0))
    out = jax.ShapeDtypeStruct((B, G, NC, HEAD_DIM), bf16)
    return pl.pallas_call(
        _nsa_compress_kernel,
        grid=(B, G),
        in_specs=[xin, xin, _full((1, 2 * W)), _full((1, 2 * W)), _full((2 * W, CMP_HIDDEN)),
                  _full((CMP_HIDDEN, HEAD_DIM)), _full((2 * W, CMP_HIDDEN)), _full((CMP_HIDDEN, HEAD_DIM))],
        out_specs=[xout, xout],
        out_shape=[out, out],
        compiler_params=_params("parallel", "parallel"),
        name="nsa_compress",
    )(xk, xv, pe_k.reshape(1, -1), pe_v.reshape(1, -1), ck_w1.astype(bf16), ck_w2.astype(bf16),
      cv_w1.astype(bf16), cv_w2.astype(bf16))


SEL_KEYS = 512


def _nsa_attn_kernel(qT_ref, kc_ref, vcT_ref, ovT_ref, ks_ref, vsT_ref, kw_ref, vwT_ref, gT_ref,
                     o_ref, val_ref, sel_ref):
    qb = pl.program_id(2)
    t0 = qb * Q_BLOCK
    R, Q = NSA_R, Q_BLOCK
    RQ = R * Q
    f32, bf16 = jnp.float32, jnp.bfloat16
    qT = qT_ref[0, 0, 0]
    lane = lax.broadcasted_iota(jnp.int32, (1, RQ), 1)
    tpos = t0 + lane % Q
    mm = lambda a, b: jnp.dot(a, b, preferred_element_type=f32)

    NC = kc_ref.shape[2]
    s = mm(kc_ref[0, 0], qT)
    cend = lax.broadcasted_iota(jnp.int32, (NC, 1), 0) * CMP_STRIDE + (CMP_BLOCK - 1)
    mask = cend <= tpos
    sm = jnp.where(mask, s, NEG)
    e = jnp.where(mask, jnp.exp(sm - jnp.max(sm, axis=0, keepdims=True)), 0.0)
    den = jnp.sum(e, axis=0, keepdims=True)
    p = e / jnp.where(den > 0.0, den, 1.0)
    o_c = mm(vcT_ref[0, 0], p.astype(bf16))
    psum = p[:, :Q]
    for r in range(1, R):
        psum = psum + p[:, r * Q:(r + 1) * Q]
    imp = jnp.dot(ovT_ref[...], psum, precision=HI, preferred_element_type=f32)

    NSB = ovT_ref.shape[0]
    tq = t0 + lax.broadcasted_iota(jnp.int32, (1, Q), 1)
    jblk = lax.broadcasted_iota(jnp.int32, (NSB, 1), 0)
    cur = tq // SEL_BLOCK
    forced = (jblk == 0) | (jblk == cur) | (jblk == cur - 1)
    val = jnp.where(jblk * SEL_BLOCK <= tq, jnp.where(forced, BIG, imp), NEG)
    val_ref[...] = val
    n_live = (t0 + Q - 1) // SEL_BLOCK + 1

    def rank_body(i, cnt):
        row = val_ref[pl.ds(i, 1), :]
        ge = jnp.where(row >= val, 1, 0)
        gt = jnp.where(row > val, 1, 0)
        return cnt + jnp.where(jblk > i, ge, gt)

    cnt = lax.fori_loop(0, n_live, rank_body, jnp.zeros((NSB, Q), jnp.int32))
    sel = (cnt < N_SELECT).astype(f32)
    sel_ref[...] = jnp.concatenate([sel] * R, axis=1)

    blocks_per_step = SEL_KEYS // SEL_BLOCK
    krow = lax.broadcasted_iota(jnp.int32, (SEL_KEYS, 1), 0)

    def sel_body(kc, carry):
        m, l, acc = carry
        k0 = pl.multiple_of(kc * SEL_KEYS, SEL_KEYS)
        s = mm(ks_ref[0, 0, pl.ds(k0, SEL_KEYS), :], qT)
        rows = [jnp.broadcast_to(sel_ref[pl.ds(kc * blocks_per_step + jb, 1), :], (SEL_BLOCK, RQ))
                for jb in range(blocks_per_step)]
        chosen = jnp.concatenate(rows, axis=0) > 0.5
        mask = chosen & (k0 + krow <= tpos)
        sm = jnp.where(mask, s, NEG)
        m_new = jnp.maximum(m, jnp.max(sm, axis=0, keepdims=True))
        alpha = jnp.exp(m - m_new)
        p = jnp.exp(sm - m_new)
        l = alpha * l + jnp.sum(p, axis=0, keepdims=True)
        acc = alpha * acc + mm(vsT_ref[0, 0, :, pl.ds(k0, SEL_KEYS)], p.astype(bf16))
        return m_new, l, acc

    n_steps = (t0 + Q - 1) // SEL_KEYS + 1
    init = (jnp.full((1, RQ), NEG, f32), jnp.zeros((1, RQ), f32), jnp.zeros((HEAD_DIM, RQ), f32))
    _, l, acc = lax.fori_loop(0, n_steps, sel_body, init)
    o_s = acc / l

    span = WINDOW + Q
    w0 = pl.multiple_of(jnp.maximum(t0 - WINDOW, 0), Q)
    s = mm(kw_ref[0, 0, pl.ds(w0, span), :], qT)
    diff = tpos - (w0 + lax.broadcasted_iota(jnp.int32, (span, 1), 0))
    sm = jnp.where((diff >= 0) & (diff < WINDOW), s, NEG)
    p = jnp.exp(sm - jnp.max(sm, axis=0, keepdims=True))
    den = jnp.sum(p, axis=0, keepdims=True)
    o_w = mm(vwT_ref[0, 0, :, pl.ds(w0, span)], p.astype(bf16)) / den

    g = gT_ref[0, 0, 0]
    o_ref[0, 0, 0] = g[0:1, :] * o_c + g[1:2, :] * o_s + g[2:3, :] * o_w


def _nsa_attention(qT, kc, vcT, ks, vsT, kw, vwT, gT):
    B, G, NQB, _, RQ = qT.shape
    S = ks.shape[2]
    NC = kc.shape[2]
    NSB = S // SEL_BLOCK
    c = jnp.arange(NC)[None, :] * CMP_STRIDE
    j = jnp.arange(NSB)[:, None] * SEL_BLOCK
    ovT = ((c <= j + SEL_BLOCK - 1) & (c + CMP_BLOCK - 1 >= j)).astype(jnp.float32)
    per_q = lambda rows: pl.BlockSpec((1, 1, 1, rows, RQ), lambda b, g, q: (b, g, q, 0, 0))
    per_g = lambda d0, d1: pl.BlockSpec((1, 1, d0, d1), lambda b, g, q: (b, g, 0, 0))
    return pl.pallas_call(
        _nsa_attn_kernel,
        grid=(B, G, NQB),
        in_specs=[per_q(HEAD_DIM), per_g(NC, HEAD_DIM), per_g(HEAD_DIM, NC), _full((NSB, NC)),
                  per_g(S, HEAD_DIM), per_g(HEAD_DIM, S), per_g(S, HEAD_DIM), per_g(HEAD_DIM, S),
                  per_q(V7X_SUBLANES)],
        out_specs=per_q(HEAD_DIM),
        out_shape=jax.ShapeDtypeStruct((B, G, NQB, HEAD_DIM, RQ), jnp.float32),
        scratch_shapes=[pltpu.VMEM((NSB, Q_BLOCK), jnp.float32), pltpu.VMEM((NSB, RQ), jnp.float32)],
        compiler_params=_params("parallel", "parallel", "arbitrary"),
        name="nsa_attention",
    )(qT, kc, vcT, ovT, ks, vsT, kw, vwT, gT)


def _nsa_branch(x2d, w_nsa, pe_k, pe_v, ck_w1, ck_w2, cv_w1, cv_w2, batch, seq_len):
    B, S, G, R, N, Q = batch, seq_len, NSA_GROUPS, NSA_R, HEAD_DIM, Q_BLOCK
    NQB = S // Q
    q, kc_in, vc_in, ks, vs, kw, vw, gates = _nsa_prep(x2d, w_nsa, S)
    per_group = lambda t: t.reshape(B, S, G, N).transpose(0, 2, 1, 3)
    per_group_t = lambda t: t.reshape(B, S, G, N).transpose(0, 2, 3, 1)
    blocks = lambda t: per_group(t).reshape(B, G, S // CMP_STRIDE, CMP_STRIDE * N)
    kc, vc = _nsa_compress(blocks(kc_in), blocks(vc_in), pe_k, pe_v, ck_w1, ck_w2, cv_w1, cv_w2)
    qT = (q.astype(jnp.bfloat16).reshape(B, NQB, Q, G, R, N).transpose(0, 3, 1, 5, 4, 2)
          .reshape(B, G, NQB, N, R * Q))
    g3 = gates[:, :3 * NSA_Q_HEADS].reshape(B, NQB, Q, G, R, 3).transpose(0, 3, 1, 5, 4, 2)
    gT = jnp.pad(g3.reshape(B, G, NQB, 3, R * Q), ((0, 0),) * 3 + ((0, V7X_SUBLANES - 3), (0, 0)))
    oT = _nsa_attention(qT, kc, vc.transpose(0, 1, 3, 2), per_group(ks), per_group_t(vs),
                        per_group(kw), per_group_t(vw), gT)
    return oT.reshape(B, G, NQB, N, R, Q).transpose(0, 2, 5, 1, 4, 3).reshape(B * S, NSA_DIM)


def _merge_kernel(x_ref, ya_ref, yb_ref, wg_ref, pa_ref, pb_ref, wo_ref, g_ref, b_ref, o_ref):
    bf16 = jnp.bfloat16
    mm = lambda a, w: jnp.dot(a.astype(bf16), w, preferred_element_type=jnp.float32)
    x = x_ref[...]
    gates = jax.nn.sigmoid(mm(x, wg_ref[...]))
    mixed = gates[:, :D_MODEL] * mm(ya_ref[...], pa_ref[...]) + gates[:, D_MODEL:] * mm(yb_ref[...], pb_ref[...])
    o_ref[...] = _layer_norm(ALPHA * x + mm(mixed, wo_ref[...]), g_ref[...], b_ref[...])


def _merge(x2d, y_a, y_b, w_gate, p_a, p_b, w_o, ln_g, ln_b, tm=256):
    M = x2d.shape[0]
    bf16 = jnp.bfloat16
    rows = lambda w: pl.BlockSpec((tm, w), lambda i: (i, 0))
    return pl.pallas_call(
        _merge_kernel,
        grid=(M // tm,),
        in_specs=[rows(D_MODEL), rows(RWKV_DIM), rows(NSA_DIM), _full((D_MODEL, 2 * D_MODEL)),
                  _full((RWKV_DIM, D_MODEL)), _full((NSA_DIM, D_MODEL)), _full((D_MODEL, D_MODEL)),
                  _full((1, D_MODEL)), _full((1, D_MODEL))],
        out_specs=rows(D_MODEL),
        out_shape=jax.ShapeDtypeStruct((M, D_MODEL), jnp.float32),
        compiler_params=_params("parallel"),
        name="merge",
    )(x2d, y_a, y_b, w_gate, p_a.astype(bf16), p_b.astype(bf16), w_o.astype(bf16),
      ln_g.reshape(1, -1), ln_b.reshape(1, -1))


def _mem_kv_kernel(mem_ref, wk_ref, wv_ref, k_out, v_out):
    m = mem_ref[...].astype(jnp.bfloat16)
    k_out[...] = jnp.dot(m, wk_ref[...], preferred_element_type=jnp.float32).astype(k_out.dtype)
    v_out[...] = jnp.dot(m, wv_ref[...], preferred_element_type=jnp.float32).astype(v_out.dtype)


def _mem_kv(mem2d, wk, wv):
    M = mem2d.shape[0]
    bf16 = jnp.bfloat16
    out = jax.ShapeDtypeStruct((M, D_MODEL), bf16)
    return pl.pallas_call(
        _mem_kv_kernel,
        grid=(1,),
        in_specs=[_full((M, D_MODEL)), _full((D_MODEL, D_MODEL)), _full((D_MODEL, D_MODEL))],
        out_specs=[_full((M, D_MODEL))] * 2,
        out_shape=[out, out],
        compiler_params=_params("arbitrary"),
        name="mem_kv",
    )(mem2d, wk.astype(bf16), wv.astype(bf16))


def _xattn_kernel(x_ref, k_ref, v_ref, wq_ref, wo_ref, g_ref, b_ref, o_ref):
    bf16, f32 = jnp.bfloat16, jnp.float32
    x = x_ref[...]
    q = jnp.dot(x.astype(bf16), wq_ref[...], preferred_element_type=f32).astype(bf16)
    heads = []
    for h in range(X_HEADS):
        sl = slice(h * X_HEAD_DIM, (h + 1) * X_HEAD_DIM)
        s = lax.dot_general(q[:, sl], k_ref[:, sl], (((1,), (1,)), ((), ())),
                            preferred_element_type=f32) * (X_HEAD_DIM ** -0.5)
        p = jnp.exp(s - jnp.max(s, axis=-1, keepdims=True))
        p = p / jnp.sum(p, axis=-1, keepdims=True)
        heads.append(jnp.dot(p.astype(bf16), v_ref[:, sl], preferred_element_type=f32))
    o = jnp.concatenate(heads, axis=1).astype(bf16)
    xa = jnp.dot(o, wo_ref[...], preferred_element_type=f32)
    o_ref[...] = _layer_norm(ALPHA * x + xa, g_ref[...], b_ref[...])


def _xattn(x2d, k_mem, v_mem, wq, wo, ln_g, ln_b, seq_len, mem_len, tm=256):
    M = x2d.shape[0]
    bf16 = jnp.bfloat16
    seq_tiles = seq_len // tm
    rows = pl.BlockSpec((tm, D_MODEL), lambda i: (i, 0))
    mem_spec = pl.BlockSpec((mem_len, D_MODEL), lambda i: (i // seq_tiles, 0))
    return pl.pallas_call(
        _xattn_kernel,
        grid=(M // tm,),
        in_specs=[rows, mem_spec, mem_spec, _full((D_MODEL, D_MODEL)), _full((D_MODEL, D_MODEL)),
                  _full((1, D_MODEL)), _full((1, D_MODEL))],
        out_specs=rows,
        out_shape=jax.ShapeDtypeStruct((M, D_MODEL), jnp.float32),
        compiler_params=_params("parallel"),
        name="xattn",
    )(x2d, k_mem, v_mem, wq.astype(bf16), wo.astype(bf16), ln_g.reshape(1, -1), ln_b.reshape(1, -1))


FFN_CHUNK = 256


def _ffn_kernel(seq_tiles, x_ref, xp_ref, wup_ref, cw_ref, cb_ref, wdn_ref, g_ref, b_ref, o_ref):
    bf16, f32 = jnp.bfloat16, jnp.float32
    i = pl.program_id(0)
    tm = x_ref.shape[0]
    H = V7X_SUBLANES
    x = x_ref[...]
    xprev = jnp.where(i % seq_tiles == 0, 0.0, xp_ref[...])
    xe = jnp.concatenate([xprev, x], axis=0).astype(bf16)

    def conv(cols):
        h = jnp.dot(xe, wup_ref[:, cols], preferred_element_type=f32)
        w = cw_ref[:, cols]
        return (h[H - 2:H - 2 + tm] * w[0:1] + h[H - 1:H - 1 + tm] * w[1:2] + h[H:] * w[2:3]
                + cb_ref[:, cols])

    acc = jnp.zeros((tm, D_MODEL), f32)
    for c in range(D_FF // FFN_CHUNK):
        gate = conv(slice(c * FFN_CHUNK, (c + 1) * FFN_CHUNK))
        val = conv(slice(D_FF + c * FFN_CHUNK, D_FF + (c + 1) * FFN_CHUNK))
        act = (gate * jax.nn.sigmoid(gate) * val).astype(bf16)
        acc = acc + jnp.dot(act, wdn_ref[c * FFN_CHUNK:(c + 1) * FFN_CHUNK, :], preferred_element_type=f32)
    o_ref[...] = _layer_norm(ALPHA * x + acc, g_ref[...], b_ref[...])


def _ffn(x2d, w_up, conv_w, conv_b, w_down, ln_g, ln_b, seq_len, tm=256):
    M = x2d.shape[0]
    bf16 = jnp.bfloat16
    seq_tiles = seq_len // tm
    blocks_per_tile = tm // V7X_SUBLANES
    rows = pl.BlockSpec((tm, D_MODEL), lambda i: (i, 0))
    once = lambda shape: pl.BlockSpec(shape, lambda i: (0,) * len(shape), pipeline_mode=pl.Buffered(1))
    return pl.pallas_call(
        functools.partial(_ffn_kernel, seq_tiles),
        grid=(M // tm,),
        in_specs=[rows,
                  pl.BlockSpec((V7X_SUBLANES, D_MODEL), lambda i: (jnp.maximum(i * blocks_per_tile - 1, 0), 0)),
                  once((D_MODEL, 2 * D_FF)), _full((3, 2 * D_FF)), _full((1, 2 * D_FF)),
                  once((D_FF, D_MODEL)), _full((1, D_MODEL)), _full((1, D_MODEL))],
        out_specs=rows,
        out_shape=jax.ShapeDtypeStruct((M, D_MODEL), jnp.float32),
        compiler_params=_params("parallel"),
        name="ffn",
    )(x2d, x2d, w_up.astype(bf16), conv_w, conv_b.reshape(1, -1), w_down.astype(bf16),
      ln_g.reshape(1, -1), ln_b.reshape(1, -1))


def kernel(x, mem, w_in, rwkv_mu, rwkv_w0, rwkv_w2, rwkv_a0, rwkv_a2, rwkv_k_k, rwkv_k_a, rwkv_r_k, rwkv_gn_g, rwkv_gn_b, nsa_pe_k, nsa_pe_v, nsa_ck_w1, nsa_ck_w2, nsa_cv_w1, nsa_cv_w2, merge_p_a, merge_p_b, mix_w_o, ln1_g, ln1_b, xa_wq, xa_wk, xa_wv, xa_wo, ln2_g, ln2_b, ffn_w_up, ffn_conv_w, ffn_conv_b, ffn_w_down, ln3_g, ln3_b):
    B, S, _ = x.shape
    mem_len = mem.shape[1]
    bf16 = jnp.bfloat16
    x2d = x.reshape(B * S, D_MODEL)
    for l in range(DEPTH):
        w = w_in[l]
        w_rwkv = w[:, :RWKV_COLS].astype(bf16)
        w_nsa = jnp.pad(w[:, RWKV_COLS:RWKV_COLS + NSA_COLS], ((0, 0), (0, NSA_COLS_PAD - NSA_COLS))).astype(bf16)
        w_gate = w[:, RWKV_COLS + NSA_COLS:].astype(bf16)
        y_a = _rwkv_time_mix(x2d, w_rwkv, rwkv_mu[l], rwkv_w0[l], rwkv_w2[l], rwkv_a0[l], rwkv_a2[l],
                             rwkv_k_k[l], rwkv_k_a[l], rwkv_r_k[l], rwkv_gn_g[l], rwkv_gn_b[l], B, S)
        y_b = _nsa_branch(x2d, w_nsa, nsa_pe_k[l], nsa_pe_v[l], nsa_ck_w1[l], nsa_ck_w2[l],
                          nsa_cv_w1[l], nsa_cv_w2[l], B, S)
        x2d = _merge(x2d, y_a, y_b, w_gate, merge_p_a[l], merge_p_b[l], mix_w_o[l], ln1_g[l], ln1_b[l])
        k_mem, v_mem = _mem_kv(mem.reshape(B * mem_len, D_MODEL), xa_wk[l], xa_wv[l])
        x2d = _xattn(x2d, k_mem, v_mem, xa_wq[l], xa_wo[l], ln2_g[l], ln2_b[l], S, mem_len)
        x2d = _ffn(x2d, ffn_w_up[l], ffn_conv_w[l], ffn_conv_b[l], ffn_w_down[l], ln3_g[l], ln3_b[l], S)
    return x2d.reshape(B, S, D_MODEL)
```

```python
import functools
import math

import jax
import jax.numpy as jnp
from jax import lax
from jax.experimental import pallas as pl
from jax.experimental.pallas import tpu as pltpu

D_MODEL = 1024
HEAD_DIM = 64
RWKV_DIM = 512
RWKV_HEADS = 8
LORA = 64
RWKV_COLS = 3 * RWKV_DIM + 2 * LORA
GN_EPS = 64e-5
NSA_DIM = 512
NSA_Q_HEADS = 8
NSA_GROUPS = 2
NSA_R = NSA_Q_HEADS // NSA_GROUPS
NSA_KV_DIM = NSA_GROUPS * HEAD_DIM
NSA_COLS = NSA_DIM + 6 * NSA_KV_DIM + 3 * NSA_Q_HEADS
NSA_COLS_PAD = 1408
CMP_BLOCK = 32
CMP_STRIDE = 16
CMP_HIDDEN = 256
SEL_BLOCK = 64
N_SELECT = 16
WINDOW = 512
Q_BLOCK = 128
ROPE_THETA = 500000.0
ROPE_DIM = 16
X_HEADS = 4
X_HEAD_DIM = 256
D_FF = 2816
LN_EPS = 1e-5
DEPTH = 1
ALPHA = (2 * DEPTH) ** 0.25
NEG = -1e30
BIG = 1e30

V7X_LANES = 128
V7X_SUBLANES = 8
V7X_VMEM_LIMIT_BYTES = 56 * 1024 * 1024

HI = lax.Precision.HIGHEST
RWKV_CHUNK = 64


def _params(*sem):
    return pltpu.CompilerParams(dimension_semantics=sem, vmem_limit_bytes=V7X_VMEM_LIMIT_BYTES)


def _full(shape):
    n = len(shape)
    return pl.BlockSpec(shape, lambda *_: (0,) * n)


def _head_ones(width):
    r = lax.broadcasted_iota(jnp.int32, (width, width), 0) // HEAD_DIM
    c = lax.broadcasted_iota(jnp.int32, (width, width), 1) // HEAD_DIM
    return (r == c).astype(jnp.float32)


def _layer_norm(y, g, b):
    mu = jnp.mean(y, axis=-1, keepdims=True)
    d = y - mu
    var = jnp.mean(d * d, axis=-1, keepdims=True)
    return d * lax.rsqrt(var + LN_EPS) * g + b


def _rwkv_prep_kernel(seq_tiles, x_ref, xp_ref, w_ref, mu_ref, w0_ref, w2_ref, a0_ref, a2_ref,
                      kk_ref, ka_ref, rk_ref,
                      r_out, k_out, v_out, lw_out, a_out, b_out, bonus_out):
    i = pl.program_id(0)
    tm = x_ref.shape[0]
    C = RWKV_DIM
    w = w_ref[...]
    p = jnp.dot(x_ref[...].astype(jnp.bfloat16), w, preferred_element_type=jnp.float32)
    xprev = xp_ref[...].astype(jnp.bfloat16)
    pprev = jnp.dot(xprev, w, preferred_element_type=jnp.float32)[V7X_SUBLANES - 1:V7X_SUBLANES, :]
    pprev = jnp.where(i % seq_tiles == 0, 0.0, pprev)
    row = lax.broadcasted_iota(jnp.int32, (tm, 1), 0)
    shifted = jnp.where(row == 0, pprev, pltpu.roll(p, 1, 0))
    p = p + (shifted - p) * mu_ref[...]
    r, k, v = p[:, :C], p[:, C:2 * C], p[:, 2 * C:3 * C]
    wl = p[:, 3 * C:3 * C + LORA]
    al = p[:, 3 * C + LORA:]
    z = -(w0_ref[...] + jnp.dot(jnp.tanh(wl).astype(jnp.bfloat16), w2_ref[...],
                                preferred_element_type=jnp.float32))
    softplus = jnp.maximum(z, 0.0) + jnp.log(1.0 + jnp.exp(-jnp.abs(z)))
    w_log = -softplus - 0.5
    lw_out[...] = -jnp.exp(w_log)
    a = jax.nn.sigmoid(a0_ref[...] + jnp.dot(al.astype(jnp.bfloat16), a2_ref[...],
                                             preferred_element_type=jnp.float32))
    ones = _head_ones(C)
    kk = k * kk_ref[...]
    nrm = jnp.sqrt(jnp.dot(kk * kk, ones, precision=HI, preferred_element_type=jnp.float32))
    kk = kk / jnp.maximum(nrm, 1e-12)
    kmod = k * (1.0 + (a - 1.0) * ka_ref[...])
    bonus = jnp.dot(r * kmod * rk_ref[...], ones, precision=HI,
                    preferred_element_type=jnp.float32) * v
    r_out[...] = r
    k_out[...] = kmod
    v_out[...] = v
    a_out[...] = -kk
    b_out[...] = kk * a
    bonus_out[...] = bonus


def _rwkv_prep(x2d, w_rwkv, mu, w0, w2, a0, a2, k_k, k_a, r_k, seq_len, tm=256):
    M = x2d.shape[0]
    C = RWKV_DIM
    seq_tiles = seq_len // tm
    row = lambda a: a.reshape(1, -1)
    out = jax.ShapeDtypeStruct((M, C), jnp.float32)
    tile = pl.BlockSpec((tm, C), lambda i: (i, 0))
    blocks_per_tile = tm // V7X_SUBLANES
    return pl.pallas_call(
        functools.partial(_rwkv_prep_kernel, seq_tiles),
        grid=(M // tm,),
        in_specs=[
            pl.BlockSpec((tm, D_MODEL), lambda i: (i, 0)),
            pl.BlockSpec((V7X_SUBLANES, D_MODEL), lambda i: (jnp.maximum(i * blocks_per_tile - 1, 0), 0)),
            _full((D_MODEL, RWKV_COLS)), _full((1, RWKV_COLS)), _full((1, C)), _full((LORA, C)),
            _full((1, C)), _full((LORA, C)), _full((1, C)), _full((1, C)), _full((1, C)),
        ],
        out_specs=[tile] * 7,
        out_shape=[out] * 7,
        compiler_params=_params("parallel"),
        name="rwkv_prep",
    )(x2d, x2d, w_rwkv, row(mu), row(w0), w2.astype(jnp.bfloat16), row(a0), a2.astype(jnp.bfloat16),
      row(k_k), row(k_a), row(r_k))


RWKV_GROUP = 4


def _split(x):
    hi = x.astype(jnp.bfloat16)
    return hi, (x - hi.astype(jnp.float32)).astype(jnp.bfloat16)


def _mm(a, b):
    return jnp.dot(a.astype(jnp.bfloat16), b.astype(jnp.bfloat16), preferred_element_type=jnp.float32)


def _tri_inverse(L, ri, ci, blocks_to):
    eye = (ri == ci).astype(jnp.float32)
    base = V7X_SUBLANES
    D = jnp.where(ri // base == ci // base, L, 0.0)
    D2 = _mm(D, D)
    D4 = _mm(D2, D2)
    X = _mm(_mm(eye + D, eye + D2), eye + D4)
    blk = base
    while blk < blocks_to:
        pair = (ri // (2 * blk) == ci // (2 * blk)) & (ri // blk != ci // blk)
        Xb = X.astype(jnp.bfloat16)
        X = X + _mm(_mm(Xb, jnp.where(pair, L, 0.0)), Xb)
        blk *= 2
    return X


def _rwkv_mix_kernel(r_ref, k_ref, v_ref, lw_ref, a_ref, b_ref, bonus_ref, gng_ref, gnb_ref,
                     y_out, state):
    T, N, GH = RWKV_CHUNK, HEAD_DIM, RWKV_GROUP
    W = GH * N
    f32 = jnp.float32

    @pl.when(pl.program_id(1) == 0)
    def _():
        state[...] = jnp.zeros_like(state)

    ri = lax.broadcasted_iota(jnp.int32, (W, W), 0)
    ci = lax.broadcasted_iota(jnp.int32, (W, W), 1)
    same_head = ri // N == ci // N
    strict, incl = ri > ci, ri >= ci
    eye = (ri == ci).astype(f32)
    tri = (lax.broadcasted_iota(jnp.int32, (T, T), 0)
           >= lax.broadcasted_iota(jnp.int32, (T, T), 1)).astype(f32)
    mean_w = (_head_ones(RWKV_DIM) * (1.0 / N)).astype(jnp.bfloat16)

    def expand(x):
        return jnp.where(same_head, jnp.concatenate([x] * GH, axis=0), jnp.zeros((), x.dtype))

    def collapse(x):
        out = x[:T]
        for h in range(1, GH):
            out = out + x[h * T:(h + 1) * T]
        return out

    bf = lambda t: t.astype(jnp.bfloat16)

    for ch in range(r_ref.shape[0] // T):
        rows = slice(ch * T, (ch + 1) * T)
        c_all = jnp.dot(tri, lw_ref[rows, :], precision=HI, preferred_element_type=f32)
        ys = []
        for g in range(RWKV_HEADS // GH):
            cols = slice(g * W, (g + 1) * W)
            lw, c = lw_ref[rows, cols], c_all[:, cols]
            r, k, v = r_ref[rows, cols], k_ref[rows, cols], v_ref[rows, cols]
            a, b = a_ref[rows, cols], b_ref[rows, cols]
            c_last = c[T - 1:T, :]
            e_neg = jnp.exp(-c)
            e_end = jnp.exp(c_last - c)
            r_t = r * jnp.exp(c)
            a_x, r_x = expand(bf(a * jnp.exp(c - lw))), expand(bf(r_t))
            b_x, k_x = expand(bf(b * e_neg)), expand(bf(k * e_neg))
            v_x = expand(bf(v))
            P = lax.dot_general(jnp.concatenate([a_x, r_x], axis=0), jnp.concatenate([b_x, k_x], axis=0),
                                (((1,), (1,)), ((), ())), preferred_element_type=f32)
            L_ab = jnp.where(strict, P[:W, :W], 0.0)
            L_ak = jnp.where(strict, P[:W, W:], 0.0)
            M_rb = jnp.where(incl, P[W:, :W], 0.0)
            M_rk = jnp.where(incl, P[W:, W:], 0.0)
            t_inv = _tri_inverse(L_ab, ri, ci, T)
            X = bf(_mm(t_inv, jnp.concatenate([a_x, bf(_mm(L_ak, v_x))], axis=1)))
            ry = _mm(M_rb, X)
            rp = expand(r_t) + ry[:, :W]
            yp = ry[:, W:] + _mm(M_rk, v_x)
            gh = _mm(expand(b * e_end).T, X)
            G = eye * jnp.exp(c_last) + gh[:, :W]
            H = gh[:, W:] + _mm(expand(k * e_end).T, v_x)
            s0 = bf(state[g])
            ys.append(collapse(_mm(rp, s0) + yp))
            state[g] = _mm(G, s0) + H
        y = jnp.concatenate(ys, axis=1)
        mean = lambda t: sum(jnp.dot(p, mean_w, preferred_element_type=f32) for p in _split(t))
        d = y - mean(y)
        yv = mean(d * d)
        y_out[rows, :] = d * lax.rsqrt(yv + GN_EPS) * gng_ref[...] + gnb_ref[...] + bonus_ref[rows, :]


def _rwkv_mix(r, k, v, lw, a, b, bonus, gn_g, gn_b, batch, chunks_per_step=2):
    M, C = r.shape
    rows = RWKV_CHUNK * chunks_per_step
    steps = M // rows // batch
    W = RWKV_GROUP * HEAD_DIM
    tile = pl.BlockSpec((rows, C), lambda bi, ci: (bi * steps + ci, 0))
    return pl.pallas_call(
        _rwkv_mix_kernel,
        grid=(batch, steps),
        in_specs=[tile] * 7 + [_full((1, C)), _full((1, C))],
        out_specs=tile,
        out_shape=jax.ShapeDtypeStruct((M, C), jnp.float32),
        scratch_shapes=[pltpu.VMEM((RWKV_HEADS // RWKV_GROUP, W, W), jnp.float32)],
        compiler_params=_params("parallel", "arbitrary"),
        name="rwkv_mix",
    )(r, k, v, lw, a, b, bonus, gn_g.reshape(1, C), gn_b.reshape(1, C))


def _rwkv_time_mix(x2d, w_rwkv, mu, w0, w2, a0, a2, k_k, k_a, r_k, gn_g, gn_b, batch, seq_len):
    r, k, v, lw, a, b, bonus = _rwkv_prep(x2d, w_rwkv, mu, w0, w2, a0, a2, k_k, k_a, r_k, seq_len)
    return _rwkv_mix(r, k, v, lw, a, b, bonus, gn_g, gn_b, batch)


def _rope_tables(seq_len):
    half = ROPE_DIM // 2
    inv = ROPE_THETA ** (-jnp.arange(half, dtype=jnp.float32) * 2.0 / ROPE_DIM)
    ang = jnp.arange(seq_len).astype(jnp.float32)[:, None] * inv[None, :]
    cos, sin = jnp.cos(ang), jnp.sin(ang)
    pad = jnp.zeros((seq_len, HEAD_DIM - ROPE_DIM), jnp.float32)
    zero = jnp.zeros_like(sin)
    c = jnp.concatenate([cos, cos, pad + 1.0], axis=1)
    s_lo = jnp.concatenate([-sin, zero, pad], axis=1)
    s_hi = jnp.concatenate([zero, sin, pad], axis=1)
    two = lambda t: jnp.concatenate([t, t], axis=1)
    return two(c), two(s_lo), two(s_hi)


def _rope_pair(x, c, s_lo, s_hi):
    return x * c + pltpu.roll(x, V7X_LANES - ROPE_DIM // 2, 1) * s_lo + pltpu.roll(x, ROPE_DIM // 2, 1) * s_hi


def _nsa_prep_kernel(x_ref, w_ref, c_ref, slo_ref, shi_ref,
                     q_out, kc_out, vc_out, ks_out, vs_out, kw_out, vw_out, gate_out):
    p = jnp.dot(x_ref[...].astype(jnp.bfloat16), w_ref[...], preferred_element_type=jnp.float32)
    c, s_lo, s_hi = c_ref[...], slo_ref[...], shi_ref[...]
    L = V7X_LANES
    rope = lambda t: _rope_pair(t, c, s_lo, s_hi)
    for j in range(NSA_DIM // L):
        q_out[:, j * L:(j + 1) * L] = rope(p[:, j * L:(j + 1) * L]) * (HEAD_DIM ** -0.5)
    kv = lambda i: p[:, NSA_DIM + i * L:NSA_DIM + (i + 1) * L]
    kc_out[...] = rope(kv(0))
    vc_out[...] = kv(1)
    ks_out[...] = rope(kv(2)).astype(ks_out.dtype)
    vs_out[...] = kv(3).astype(vs_out.dtype)
    kw_out[...] = rope(kv(4)).astype(kw_out.dtype)
    vw_out[...] = kv(5).astype(vw_out.dtype)
    gate_out[...] = jax.nn.sigmoid(kv(6))


def _nsa_prep(x2d, w_nsa, seq_len, tm=256):
    M = x2d.shape[0]
    L = V7X_LANES
    seq_tiles = seq_len // tm
    tabs = _rope_tables(seq_len)
    tab_spec = pl.BlockSpec((tm, L), lambda i: (i % seq_tiles, 0))
    narrow = pl.BlockSpec((tm, L), lambda i: (i, 0))
    f32, bf16 = jnp.float32, jnp.bfloat16
    sds = lambda w, dt: jax.ShapeDtypeStruct((M, w), dt)
    return pl.pallas_call(
        _nsa_prep_kernel,
        grid=(M // tm,),
        in_specs=[pl.BlockSpec((tm, D_MODEL), lambda i: (i, 0)), _full((D_MODEL, NSA_COLS_PAD)),
                  tab_spec, tab_spec, tab_spec],
        out_specs=[pl.BlockSpec((tm, NSA_DIM), lambda i: (i, 0))] + [narrow] * 7,
        out_shape=[sds(NSA_DIM, f32), sds(L, f32), sds(L, f32), sds(L, bf16), sds(L, bf16),
                   sds(L, bf16), sds(L, bf16), sds(L, f32)],
        compiler_params=_params("parallel"),
        name="nsa_prep",
    )(x2d, w_nsa, *tabs)


def _gelu_tanh(x):
    return 0.5 * x * (1.0 + jnp.tanh(math.sqrt(2.0 / math.pi) * (x + 0.044715 * x * x * x)))


def _nsa_compress_kernel(xk_ref, xv_ref, pek_ref, pev_ref, kw1_ref, kw2_ref, vw1_ref, vw2_ref,
                         kc_out, vc_out):
    half = CMP_STRIDE * HEAD_DIM

    def mlp(x, pe, w1_ref, w2_ref):
        lo = jnp.dot((x + pe[:, :half]).astype(jnp.bfloat16), w1_ref[:half, :],
                     preferred_element_type=jnp.float32)
        hi = jnp.dot((x + pe[:, half:]).astype(jnp.bfloat16), w1_ref[half:, :],
                     preferred_element_type=jnp.float32)
        n = x.shape[0]
        pre = lo + pltpu.roll(hi, n - 1, 0)
        return jnp.dot(_gelu_tanh(pre).astype(jnp.bfloat16), w2_ref[...],
                       preferred_element_type=jnp.float32)

    kc_out[0, 0] = mlp(xk_ref[0, 0], pek_ref[...], kw1_ref, kw2_ref).astype(kc_out.dtype)
    vc_out[0, 0] = mlp(xv_ref[0, 0], pev_ref[...], vw1_ref, vw2_ref).astype(vc_out.dtype)


def _nsa_compress(xk, xv, pe_k, pe_v, ck_w1, ck_w2, cv_w1, cv_w2):
    B, G, NC, W = xk.shape
    bf16 = jnp.bfloat16
    xin = pl.BlockSpec((1, 1, NC, W), lambda b, g: (b, g, 0, 0))
    xout = pl.BlockSpec((1, 1, NC, HEAD_DIM), lambda b, g: (b, g, 0, 0))
    out = jax.ShapeDtypeStruct((B, G, NC, HEAD_DIM), bf16)
    return pl.pallas_call(
        _nsa_compress_kernel,
        grid=(B, G),
        in_specs=[xin, xin, _full((1, 2 * W)), _full((1, 2 * W)), _full((2 * W, CMP_HIDDEN)),
                  _full((CMP_HIDDEN, HEAD_DIM)), _full((2 * W, CMP_HIDDEN)), _full((CMP_HIDDEN, HEAD_DIM))],
        out_specs=[xout, xout],
        out_shape=[out, out],
        compiler_params=_params("parallel", "parallel"),
        name="nsa_compress",
    )(xk, xv, pe_k.reshape(1, -1), pe_v.reshape(1, -1), ck_w1.astype(bf16), ck_w2.astype(bf16),
      cv_w1.astype(bf16), cv_w2.astype(bf16))


SEL_KEYS = 512


def _nsa_attn_kernel(qT_ref, kc_ref, vcT_ref, ovT_ref, ks_ref, vsT_ref, kw_ref, vwT_ref, gT_ref,
                     o_ref, val_ref, sel_ref):
    qb = pl.program_id(2)
    t0 = qb * Q_BLOCK
    R, Q = NSA_R, Q_BLOCK
    RQ = R * Q
    f32, bf16 = jnp.float32, jnp.bfloat16
    qT = qT_ref[0, 0, 0]
    lane = lax.broadcasted_iota(jnp.int32, (1, RQ), 1)
    tpos = t0 + lane % Q
    mm = lambda a, b: jnp.dot(a, b, preferred_element_type=f32)

    NC = kc_ref.shape[2]
    s = mm(kc_ref[0, 0], qT)
    cend = lax.broadcasted_iota(jnp.int32, (NC, 1), 0) * CMP_STRIDE + (CMP_BLOCK - 1)
    mask = cend <= tpos
    sm = jnp.where(mask, s, NEG)
    e = jnp.where(mask, jnp.exp(sm - jnp.max(sm, axis=0, keepdims=True)), 0.0)
    den = jnp.sum(e, axis=0, keepdims=True)
    p = e / jnp.where(den > 0.0, den, 1.0)
    o_c = mm(vcT_ref[0, 0], p.astype(bf16))
    psum = p[:, :Q]
    for r in range(1, R):
        psum = psum + p[:, r * Q:(r + 1) * Q]
    imp = jnp.dot(ovT_ref[...], psum, precision=HI, preferred_element_type=f32)

    NSB = ovT_ref.shape[0]
    tq = t0 + lax.broadcasted_iota(jnp.int32, (1, Q), 1)
    jblk = lax.broadcasted_iota(jnp.int32, (NSB, 1), 0)
    cur = tq // SEL_BLOCK
    forced = (jblk == 0) | (jblk == cur) | (jblk == cur - 1)
    val = jnp.where(jblk * SEL_BLOCK <= tq, jnp.where(forced, BIG, imp), NEG)
    val_ref[...] = val
    n_live = (t0 + Q - 1) // SEL_BLOCK + 1

    def rank_body(i, cnt):
        row = val_ref[pl.ds(i, 1), :]
        ge = jnp.where(row >= val, 1, 0)
        gt = jnp.where(row > val, 1, 0)
        return cnt + jnp.where(jblk > i, ge, gt)

    cnt = lax.fori_loop(0, n_live, rank_body, jnp.zeros((NSB, Q), jnp.int32))
    sel = (cnt < N_SELECT).astype(f32)
    sel_ref[...] = jnp.concatenate([sel] * R, axis=1)

    blocks_per_step = SEL_KEYS // SEL_BLOCK
    krow = lax.broadcasted_iota(jnp.int32, (SEL_KEYS, 1), 0)

    def sel_body(kc, carry):
        m, l, acc = carry
        k0 = pl.multiple_of(kc * SEL_KEYS, SEL_KEYS)
        s = mm(ks_ref[0, 0, pl.ds(k0, SEL_KEYS), :], qT)
        rows = [jnp.broadcast_to(sel_ref[pl.ds(kc * blocks_per_step + jb, 1), :], (SEL_BLOCK, RQ))
                for jb in range(blocks_per_step)]
        chosen = jnp.concatenate(rows, axis=0) > 0.5
        mask = chosen & (k0 + krow <= tpos)
        sm = jnp.where(mask, s, NEG)
        m_new = jnp.maximum(m, jnp.max(sm, axis=0, keepdims=True))
        alpha = jnp.exp(m - m_new)
        p = jnp.exp(sm - m_new)
        l = alpha * l + jnp.sum(p, axis=0, keepdims=True)
        acc = alpha * acc + mm(vsT_ref[0, 0, :, pl.ds(k0, SEL_KEYS)], p.astype(bf16))
        return m_new, l, acc

    n_steps = (t0 + Q - 1) // SEL_KEYS + 1
    init = (jnp.full((1, RQ), NEG, f32), jnp.zeros((1, RQ), f32), jnp.zeros((HEAD_DIM, RQ), f32))
    _, l, acc = lax.fori_loop(0, n_steps, sel_body, init)
    o_s = acc / l

    span = WINDOW + Q
    w0 = pl.multiple_of(jnp.maximum(t0 - WINDOW, 0), Q)
    s = mm(kw_ref[0, 0, pl.ds(w0, span), :], qT)
    diff = tpos - (w0 + lax.broadcasted_iota(jnp.int32, (span, 1), 0))
    sm = jnp.where((diff >= 0) & (diff < WINDOW), s, NEG)
    p = jnp.exp(sm - jnp.max(sm, axis=0, keepdims=True))
    den = jnp.sum(p, axis=0, keepdims=True)
    o_w = mm(vwT_ref[0, 0, :, pl.ds(w0, span)], p.astype(bf16)) / den

    g = gT_ref[0, 0, 0]
    o_ref[0, 0, 0] = g[0:1, :] * o_c + g[1:2, :] * o_s + g[2:3, :] * o_w


def _nsa_attention(qT, kc, vcT, ks, vsT, kw, vwT, gT):
    B, G, NQB, _, RQ = qT.shape
    S = ks.shape[2]
    NC = kc.shape[2]
    NSB = S // SEL_BLOCK
    c = jnp.arange(NC)[None, :] * CMP_STRIDE
    j = jnp.arange(NSB)[:, None] * SEL_BLOCK
    ovT = ((c <= j + SEL_BLOCK - 1) & (c + CMP_BLOCK - 1 >= j)).astype(jnp.float32)
    per_q = lambda rows: pl.BlockSpec((1, 1, 1, rows, RQ), lambda b, g, q: (b, g, q, 0, 0))
    per_g = lambda d0, d1: pl.BlockSpec((1, 1, d0, d1), lambda b, g, q: (b, g, 0, 0))
    return pl.pallas_call(
        _nsa_attn_kernel,
        grid=(B, G, NQB),
        in_specs=[per_q(HEAD_DIM), per_g(NC, HEAD_DIM), per_g(HEAD_DIM, NC), _full((NSB, NC)),
                  per_g(S, HEAD_DIM), per_g(HEAD_DIM, S), per_g(S, HEAD_DIM), per_g(HEAD_DIM, S),
                  per_q(V7X_SUBLANES)],
        out_specs=per_q(HEAD_DIM),
        out_shape=jax.ShapeDtypeStruct((B, G, NQB, HEAD_DIM, RQ), jnp.float32),
        scratch_shapes=[pltpu.VMEM((NSB, Q_BLOCK), jnp.float32), pltpu.VMEM((NSB, RQ), jnp.float32)],
        compiler_params=_params("parallel", "parallel", "arbitrary"),
        name="nsa_attention",
    )(qT, kc, vcT, ovT, ks, vsT, kw, vwT, gT)


def _nsa_branch(x2d, w_nsa, pe_k, pe_v, ck_w1, ck_w2, cv_w1, cv_w2, batch, seq_len):
    B, S, G, R, N, Q = batch, seq_len, NSA_GROUPS, NSA_R, HEAD_DIM, Q_BLOCK
    NQB = S // Q
    q, kc_in, vc_in, ks, vs, kw, vw, gates = _nsa_prep(x2d, w_nsa, S)
    per_group = lambda t: t.reshape(B, S, G, N).transpose(0, 2, 1, 3)
    per_group_t = lambda t: t.reshape(B, S, G, N).transpose(0, 2, 3, 1)
    blocks = lambda t: per_group(t).reshape(B, G, S // CMP_STRIDE, CMP_STRIDE * N)
    kc, vc = _nsa_compress(blocks(kc_in), blocks(vc_in), pe_k, pe_v, ck_w1, ck_w2, cv_w1, cv_w2)
    qT = (q.astype(jnp.bfloat16).reshape(B, NQB, Q, G, R, N).transpose(0, 3, 1, 5, 4, 2)
          .reshape(B, G, NQB, N, R * Q))
    g3 = gates[:, :3 * NSA_Q_HEADS].reshape(B, NQB, Q, G, R, 3).transpose(0, 3, 1, 5, 4, 2)
    gT = jnp.pad(g3.reshape(B, G, NQB, 3, R * Q), ((0, 0),) * 3 + ((0, V7X_SUBLANES - 3), (0, 0)))
    oT = _nsa_attention(qT, kc, vc.transpose(0, 1, 3, 2), per_group(ks), per_group_t(vs),
                        per_group(kw), per_group_t(vw), gT)
    return oT.reshape(B, G, NQB, N, R, Q).transpose(0, 2, 5, 1, 4, 3).reshape(B * S, NSA_DIM)


def _merge_kernel(x_ref, ya_ref, yb_ref, wg_ref, pa_ref, pb_ref, wo_ref, g_ref, b_ref, o_ref):
    bf16 = jnp.bfloat16
    mm = lambda a, w: jnp.dot(a.astype(bf16), w, preferred_element_type=jnp.float32)
    x = x_ref[...]
    gates = jax.nn.sigmoid(mm(x, wg_ref[...]))
    mixed = gates[:, :D_MODEL] * mm(ya_ref[...], pa_ref[...]) + gates[:, D_MODEL:] * mm(yb_ref[...], pb_ref[...])
    o_ref[...] = _layer_norm(ALPHA * x + mm(mixed, wo_ref[...]), g_ref[...], b_ref[...])


def _merge(x2d, y_a, y_b, w_gate, p_a, p_b, w_o, ln_g, ln_b, tm=256):
    M = x2d.shape[0]
    bf16 = jnp.bfloat16
    rows = lambda w: pl.BlockSpec((tm, w), lambda i: (i, 0))
    return pl.pallas_call(
        _merge_kernel,
        grid=(M // tm,),
        in_specs=[rows(D_MODEL), rows(RWKV_DIM), rows(NSA_DIM), _full((D_MODEL, 2 * D_MODEL)),
                  _full((RWKV_DIM, D_MODEL)), _full((NSA_DIM, D_MODEL)), _full((D_MODEL, D_MODEL)),
                  _full((1, D_MODEL)), _full((1, D_MODEL))],
        out_specs=rows(D_MODEL),
        out_shape=jax.ShapeDtypeStruct((M, D_MODEL), jnp.float32),
        compiler_params=_params("parallel"),
        name="merge",
    )(x2d, y_a, y_b, w_gate, p_a.astype(bf16), p_b.astype(bf16), w_o.astype(bf16),
      ln_g.reshape(1, -1), ln_b.reshape(1, -1))


def _mem_kv_kernel(mem_ref, wk_ref, wv_ref, k_out, v_out):
    m = mem_ref[...].astype(jnp.bfloat16)
    k_out[...] = jnp.dot(m, wk_ref[...], preferred_element_type=jnp.float32).astype(k_out.dtype)
    v_out[...] = jnp.dot(m, wv_ref[...], preferred_element_type=jnp.float32).astype(v_out.dtype)


def _mem_kv(mem2d, wk, wv):
    M = mem2d.shape[0]
    bf16 = jnp.bfloat16
    out = jax.ShapeDtypeStruct((M, D_MODEL), bf16)
    return pl.pallas_call(
        _mem_kv_kernel,
        grid=(1,),
        in_specs=[_full((M, D_MODEL)), _full((D_MODEL, D_MODEL)), _full((D_MODEL, D_MODEL))],
        out_specs=[_full((M, D_MODEL))] * 2,
        out_shape=[out, out],
        compiler_params=_params("arbitrary"),
        name="mem_kv",
    )(mem2d, wk.astype(bf16), wv.astype(bf16))


def _xattn_kernel(x_ref, k_ref, v_ref, wq_ref, wo_ref, g_ref, b_ref, o_ref):
    bf16, f32 = jnp.bfloat16, jnp.float32
    x = x_ref[...]
    q = jnp.dot(x.astype(bf16), wq_ref[...], preferred_element_type=f32).astype(bf16)
    heads = []
    for h in range(X_HEADS):
        sl = slice(h * X_HEAD_DIM, (h + 1) * X_HEAD_DIM)
        s = lax.dot_general(q[:, sl], k_ref[:, sl], (((1,), (1,)), ((), ())),
                            preferred_element_type=f32) * (X_HEAD_DIM ** -0.5)
        p = jnp.exp(s - jnp.max(s, axis=-1, keepdims=True))
        p = p / jnp.sum(p, axis=-1, keepdims=True)
        heads.append(jnp.dot(p.astype(bf16), v_ref[:, sl], preferred_element_type=f32))
    o = jnp.concatenate(heads, axis=1).astype(bf16)
    xa = jnp.dot(o, wo_ref[...], preferred_element_type=f32)
    o_ref[...] = _layer_norm(ALPHA * x + xa, g_ref[...], b_ref[...])


def _xattn(x2d, k_mem, v_mem, wq, wo, ln_g, ln_b, seq_len, mem_len, tm=256):
    M = x2d.shape[0]
    bf16 = jnp.bfloat16
    seq_tiles = seq_len // tm
    rows = pl.BlockSpec((tm, D_MODEL), lambda i: (i, 0))
    mem_spec = pl.BlockSpec((mem_len, D_MODEL), lambda i: (i // seq_tiles, 0))
    return pl.pallas_call(
        _xattn_kernel,
        grid=(M // tm,),
        in_specs=[rows, mem_spec, mem_spec, _full((D_MODEL, D_MODEL)), _full((D_MODEL, D_MODEL)),
                  _full((1, D_MODEL)), _full((1, D_MODEL))],
        out_specs=rows,
        out_shape=jax.ShapeDtypeStruct((M, D_MODEL), jnp.float32),
        compiler_params=_params("parallel"),
        name="xattn",
    )(x2d, k_mem, v_mem, wq.astype(bf16), wo.astype(bf16), ln_g.reshape(1, -1), ln_b.reshape(1, -1))


FFN_CHUNK = 256


def _ffn_kernel(seq_tiles, x_ref, xp_ref, wup_ref, cw_ref, cb_ref, wdn_ref, g_ref, b_ref, o_ref):
    bf16, f32 = jnp.bfloat16, jnp.float32
    i = pl.program_id(0)
    tm = x_ref.shape[0]
    H = V7X_SUBLANES
    x = x_ref[...]
    xprev = jnp.where(i % seq_tiles == 0, 0.0, xp_ref[...])
    xe = jnp.concatenate([xprev, x], axis=0).astype(bf16)

    def conv(cols):
        h = jnp.dot(xe, wup_ref[:, cols], preferred_element_type=f32)
        w = cw_ref[:, cols]
        return (h[H - 2:H - 2 + tm] * w[0:1] + h[H - 1:H - 1 + tm] * w[1:2] + h[H:] * w[2:3]
                + cb_ref[:, cols])

    acc = jnp.zeros((tm, D_MODEL), f32)
    for c in range(D_FF // FFN_CHUNK):
        gate = conv(slice(c * FFN_CHUNK, (c + 1) * FFN_CHUNK))
        val = conv(slice(D_FF + c * FFN_CHUNK, D_FF + (c + 1) * FFN_CHUNK))
        act = (gate * jax.nn.sigmoid(gate) * val).astype(bf16)
        acc = acc + jnp.dot(act, wdn_ref[c * FFN_CHUNK:(c + 1) * FFN_CHUNK, :], preferred_element_type=f32)
    o_ref[...] = _layer_norm(ALPHA * x + acc, g_ref[...], b_ref[...])


def _ffn(x2d, w_up, conv_w, conv_b, w_down, ln_g, ln_b, seq_len, tm=256):
    M = x2d.shape[0]
    bf16 = jnp.bfloat16
    seq_tiles = seq_len // tm
    blocks_per_tile = tm // V7X_SUBLANES
    rows = pl.BlockSpec((tm, D_MODEL), lambda i: (i, 0))
    once = lambda shape: pl.BlockSpec(shape, lambda i: (0,) * len(shape), pipeline_mode=pl.Buffered(1))
    return pl.pallas_call(
        functools.partial(_ffn_kernel, seq_tiles),
        grid=(M // tm,),
        in_specs=[rows,
                  pl.BlockSpec((V7X_SUBLANES, D_MODEL), lambda i: (jnp.maximum(i * blocks_per_tile - 1, 0), 0)),
                  once((D_MODEL, 2 * D_FF)), _full((3, 2 * D_FF)), _full((1, 2 * D_FF)),
                  once((D_FF, D_MODEL)), _full((1, D_MODEL)), _full((1, D_MODEL))],
        out_specs=rows,
        out_shape=jax.ShapeDtypeStruct((M, D_MODEL), jnp.float32),
        compiler_params=_params("parallel"),
        name="ffn",
    )(x2d, x2d, w_up.astype(bf16), conv_w, conv_b.reshape(1, -1), w_down.astype(bf16),
      ln_g.reshape(1, -1), ln_b.reshape(1, -1))


def kernel(x, mem, w_in, rwkv_mu, rwkv_w0, rwkv_w2, rwkv_a0, rwkv_a2, rwkv_k_k, rwkv_k_a, rwkv_r_k, rwkv_gn_g, rwkv_gn_b, nsa_pe_k, nsa_pe_v, nsa_ck_w1, nsa_ck_w2, nsa_cv_w1, nsa_cv_w2, merge_p_a, merge_p_b, mix_w_o, ln1_g, ln1_b, xa_wq, xa_wk, xa_wv, xa_wo, ln2_g, ln2_b, ffn_w_up, ffn_conv_w, ffn_conv_b, ffn_w_down, ln3_g, ln3_b):
    B, S, _ = x.shape
    mem_len = mem.shape[1]
    bf16 = jnp.bfloat16
    x2d = x.reshape(B * S, D_MODEL)
    for l in range(DEPTH):
        w = w_in[l]
        w_rwkv = w[:, :RWKV_COLS].astype(bf16)
        w_nsa = jnp.pad(w[:, RWKV_COLS:RWKV_COLS + NSA_COLS], ((0, 0), (0, NSA_COLS_PAD - NSA_COLS))).astype(bf16)
        w_gate = w[:, RWKV_COLS + NSA_COLS:].astype(bf16)
        y_a = _rwkv_time_mix(x2d, w_rwkv, rwkv_mu[l], rwkv_w0[l], rwkv_w2[l], rwkv_a0[l], rwkv_a2[l],
                             rwkv_k_k[l], rwkv_k_a[l], rwkv_r_k[l], rwkv_gn_g[l], rwkv_gn_b[l], B, S)
        y_b = _nsa_branch(x2d, w_nsa, nsa_pe_k[l], nsa_pe_v[l], nsa_ck_w1[l], nsa_ck_w2[l],
                          nsa_cv_w1[l], nsa_cv_w2[l], B, S)
        x2d = _merge(x2d, y_a, y_b, w_gate, merge_p_a[l], merge_p_b[l], mix_w_o[l], ln1_g[l], ln1_b[l])
        k_mem, v_mem = _mem_kv(mem.reshape(B * mem_len, D_MODEL), xa_wk[l], xa_wv[l])
        x2d = _xattn(x2d, k_mem, v_mem, xa_wq[l], xa_wo[l], ln2_g[l], ln2_b[l], S, mem_len)
        x2d = _ffn(x2d, ffn_w_up[l], ffn_conv_w[l], ffn_conv_b[l], ffn_w_down[l], ln3_g[l], ln3_b[l], S)
    return x2d.reshape(B, S, D_MODEL)
```

```python
import functools
import math

import jax
import jax.numpy as jnp
from jax import lax
from jax.experimental import pallas as pl
from jax.experimental.pallas import tpu as pltpu

D_MODEL = 1024
HEAD_DIM = 64
RWKV_DIM = 512
RWKV_HEADS = 8
LORA = 64
RWKV_COLS = 3 * RWKV_DIM + 2 * LORA
GN_EPS = 64e-5
NSA_DIM = 512
NSA_Q_HEADS = 8
NSA_GROUPS = 2
NSA_R = NSA_Q_HEADS // NSA_GROUPS
NSA_KV_DIM = NSA_GROUPS * HEAD_DIM
NSA_COLS = NSA_DIM + 6 * NSA_KV_DIM + 3 * NSA_Q_HEADS
NSA_COLS_PAD = 1408
CMP_BLOCK = 32
CMP_STRIDE = 16
CMP_HIDDEN = 256
SEL_BLOCK = 64
N_SELECT = 16
WINDOW = 512
Q_BLOCK = 128
ROPE_THETA = 500000.0
ROPE_DIM = 16
X_HEADS = 4
X_HEAD_DIM = 256
D_FF = 2816
LN_EPS = 1e-5
DEPTH = 1
ALPHA = (2 * DEPTH) ** 0.25
NEG = -1e30
BIG = 1e30

V7X_LANES = 128
V7X_SUBLANES = 8
V7X_VMEM_LIMIT_BYTES = 56 * 1024 * 1024

HI = lax.Precision.HIGHEST
RWKV_CHUNK = 64


def _params(*sem):
    return pltpu.CompilerParams(dimension_semantics=sem, vmem_limit_bytes=V7X_VMEM_LIMIT_BYTES)


def _full(shape):
    n = len(shape)
    return pl.BlockSpec(shape, lambda *_: (0,) * n)


def _head_ones(width):
    r = lax.broadcasted_iota(jnp.int32, (width, width), 0) // HEAD_DIM
    c = lax.broadcasted_iota(jnp.int32, (width, width), 1) // HEAD_DIM
    return (r == c).astype(jnp.float32)


def _layer_norm(y, g, b):
    mu = jnp.mean(y, axis=-1, keepdims=True)
    d = y - mu
    var = jnp.mean(d * d, axis=-1, keepdims=True)
    return d * lax.rsqrt(var + LN_EPS) * g + b


def _rwkv_prep_kernel(seq_tiles, x_ref, xp_ref, w_ref, mu_ref, w0_ref, w2_ref, a0_ref, a2_ref,
                      kk_ref, ka_ref, rk_ref,
                      r_out, k_out, v_out, lw_out, a_out, b_out, bonus_out):
    i = pl.program_id(0)
    tm = x_ref.shape[0]
    C = RWKV_DIM
    w = w_ref[...]
    p = jnp.dot(x_ref[...].astype(jnp.bfloat16), w, preferred_element_type=jnp.float32)
    xprev = xp_ref[...].astype(jnp.bfloat16)
    pprev = jnp.dot(xprev, w, preferred_element_type=jnp.float32)[V7X_SUBLANES - 1:V7X_SUBLANES, :]
    pprev = jnp.where(i % seq_tiles == 0, 0.0, pprev)
    row = lax.broadcasted_iota(jnp.int32, (tm, 1), 0)
    shifted = jnp.where(row == 0, pprev, pltpu.roll(p, 1, 0))
    p = p + (shifted - p) * mu_ref[...]
    r, k, v = p[:, :C], p[:, C:2 * C], p[:, 2 * C:3 * C]
    wl = p[:, 3 * C:3 * C + LORA]
    al = p[:, 3 * C + LORA:]
    z = -(w0_ref[...] + jnp.dot(jnp.tanh(wl).astype(jnp.bfloat16), w2_ref[...],
                                preferred_element_type=jnp.float32))
    softplus = jnp.maximum(z, 0.0) + jnp.log(1.0 + jnp.exp(-jnp.abs(z)))
    w_log = -softplus - 0.5
    lw_out[...] = -jnp.exp(w_log)
    a = jax.nn.sigmoid(a0_ref[...] + jnp.dot(al.astype(jnp.bfloat16), a2_ref[...],
                                             preferred_element_type=jnp.float32))
    ones = _head_ones(C)
    kk = k * kk_ref[...]
    nrm = jnp.sqrt(jnp.dot(kk * kk, ones, precision=HI, preferred_element_type=jnp.float32))
    kk = kk / jnp.maximum(nrm, 1e-12)
    kmod = k * (1.0 + (a - 1.0) * ka_ref[...])
    bonus = jnp.dot(r * kmod * rk_ref[...], ones, precision=HI,
                    preferred_element_type=jnp.float32) * v
    r_out[...] = r
    k_out[...] = kmod
    v_out[...] = v
    a_out[...] = -kk
    b_out[...] = kk * a
    bonus_out[...] = bonus


def _rwkv_prep(x2d, w_rwkv, mu, w0, w2, a0, a2, k_k, k_a, r_k, seq_len, tm=256):
    M = x2d.shape[0]
    C = RWKV_DIM
    seq_tiles = seq_len // tm
    row = lambda a: a.reshape(1, -1)
    out = jax.ShapeDtypeStruct((M, C), jnp.float32)
    tile = pl.BlockSpec((tm, C), lambda i: (i, 0))
    blocks_per_tile = tm // V7X_SUBLANES
    return pl.pallas_call(
        functools.partial(_rwkv_prep_kernel, seq_tiles),
        grid=(M // tm,),
        in_specs=[
            pl.BlockSpec((tm, D_MODEL), lambda i: (i, 0)),
            pl.BlockSpec((V7X_SUBLANES, D_MODEL), lambda i: (jnp.maximum(i * blocks_per_tile - 1, 0), 0)),
            _full((D_MODEL, RWKV_COLS)), _full((1, RWKV_COLS)), _full((1, C)), _full((LORA, C)),
            _full((1, C)), _full((LORA, C)), _full((1, C)), _full((1, C)), _full((1, C)),
        ],
        out_specs=[tile] * 7,
        out_shape=[out] * 7,
        compiler_params=_params("parallel"),
        name="rwkv_prep",
    )(x2d, x2d, w_rwkv, row(mu), row(w0), w2.astype(jnp.bfloat16), row(a0), a2.astype(jnp.bfloat16),
      row(k_k), row(k_a), row(r_k))


RWKV_GROUP = 4


def _split(x):
    hi = x.astype(jnp.bfloat16)
    return hi, (x - hi.astype(jnp.float32)).astype(jnp.bfloat16)


def _mm(a, b):
    return jnp.dot(a.astype(jnp.bfloat16), b.astype(jnp.bfloat16), preferred_element_type=jnp.float32)


def _tri_inverse(L, ri, ci, blocks_to):
    eye = (ri == ci).astype(jnp.float32)
    base = V7X_SUBLANES
    D = jnp.where(ri // base == ci // base, L, 0.0)
    D2 = _mm(D, D)
    D4 = _mm(D2, D2)
    X = _mm(_mm(eye + D, eye + D2), eye + D4)
    blk = base
    while blk < blocks_to:
        pair = (ri // (2 * blk) == ci // (2 * blk)) & (ri // blk != ci // blk)
        Xb = X.astype(jnp.bfloat16)
        X = X + _mm(_mm(Xb, jnp.where(pair, L, 0.0)), Xb)
        blk *= 2
    return X


def _rwkv_mix_kernel(r_ref, k_ref, v_ref, lw_ref, a_ref, b_ref, bonus_ref, gng_ref, gnb_ref,
                     y_out, state):
    T, N, GH = RWKV_CHUNK, HEAD_DIM, RWKV_GROUP
    W = GH * N
    f32 = jnp.float32

    @pl.when(pl.program_id(1) == 0)
    def _():
        state[...] = jnp.zeros_like(state)

    ri = lax.broadcasted_iota(jnp.int32, (W, W), 0)
    ci = lax.broadcasted_iota(jnp.int32, (W, W), 1)
    same_head = ri // N == ci // N
    strict, incl = ri > ci, ri >= ci
    eye = (ri == ci).astype(f32)
    tri = (lax.broadcasted_iota(jnp.int32, (T, T), 0)
           >= lax.broadcasted_iota(jnp.int32, (T, T), 1)).astype(f32)
    mean_w = (_head_ones(RWKV_DIM) * (1.0 / N)).astype(jnp.bfloat16)

    def expand(x):
        return jnp.where(same_head, jnp.concatenate([x] * GH, axis=0), jnp.zeros((), x.dtype))

    def collapse(x):
        out = x[:T]
        for h in range(1, GH):
            out = out + x[h * T:(h + 1) * T]
        return out

    bf = lambda t: t.astype(jnp.bfloat16)

    for ch in range(r_ref.shape[0] // T):
        rows = slice(ch * T, (ch + 1) * T)
        c_all = jnp.dot(tri, lw_ref[rows, :], precision=HI, preferred_element_type=f32)
        ys = []
        for g in range(RWKV_HEADS // GH):
            cols = slice(g * W, (g + 1) * W)
            lw, c = lw_ref[rows, cols], c_all[:, cols]
            r, k, v = r_ref[rows, cols], k_ref[rows, cols], v_ref[rows, cols]
            a, b = a_ref[rows, cols], b_ref[rows, cols]
            c_last = c[T - 1:T, :]
            e_neg = jnp.exp(-c)
            e_end = jnp.exp(c_last - c)
            r_t = r * jnp.exp(c)
            a_x, r_x = expand(bf(a * jnp.exp(c - lw))), expand(bf(r_t))
            b_x, k_x = expand(bf(b * e_neg)), expand(bf(k * e_neg))
            v_x = expand(bf(v))
            P = lax.dot_general(jnp.concatenate([a_x, r_x], axis=0), jnp.concatenate([b_x, k_x], axis=0),
                                (((1,), (1,)), ((), ())), preferred_element_type=f32)
            L_ab = jnp.where(strict, P[:W, :W], 0.0)
            L_ak = jnp.where(strict, P[:W, W:], 0.0)
            M_rb = jnp.where(incl, P[W:, :W], 0.0)
            M_rk = jnp.where(incl, P[W:, W:], 0.0)
            t_inv = _tri_inverse(L_ab, ri, ci, T)
            X = bf(_mm(t_inv, jnp.concatenate([a_x, bf(_mm(L_ak, v_x))], axis=1)))
            ry = _mm(M_rb, X)
            rp = expand(r_t) + ry[:, :W]
            yp = ry[:, W:] + _mm(M_rk, v_x)
            gh = _mm(expand(b * e_end).T, X)
            G = eye * jnp.exp(c_last) + gh[:, :W]
            H = gh[:, W:] + _mm(expand(k * e_end).T, v_x)
            s0 = bf(state[g])
            ys.append(collapse(_mm(rp, s0) + yp))
            state[g] = _mm(G, s0) + H
        y = jnp.concatenate(ys, axis=1)
        mean = lambda t: sum(jnp.dot(p, mean_w, preferred_element_type=f32) for p in _split(t))
        d = y - mean(y)
        yv = mean(d * d)
        y_out[rows, :] = d * lax.rsqrt(yv + GN_EPS) * gng_ref[...] + gnb_ref[...] + bonus_ref[rows, :]


def _rwkv_mix(r, k, v, lw, a, b, bonus, gn_g, gn_b, batch, chunks_per_step=2):
    M, C = r.shape
    rows = RWKV_CHUNK * chunks_per_step
    steps = M // rows // batch
    W = RWKV_GROUP * HEAD_DIM
    tile = pl.BlockSpec((rows, C), lambda bi, ci: (bi * steps + ci, 0))
    return pl.pallas_call(
        _rwkv_mix_kernel,
        grid=(batch, steps),
        in_specs=[tile] * 7 + [_full((1, C)), _full((1, C))],
        out_specs=tile,
        out_shape=jax.ShapeDtypeStruct((M, C), jnp.float32),
        scratch_shapes=[pltpu.VMEM((RWKV_HEADS // RWKV_GROUP, W, W), jnp.float32)],
        compiler_params=_params("parallel", "arbitrary"),
        name="rwkv_mix",
    )(r, k, v, lw, a, b, bonus, gn_g.reshape(1, C), gn_b.reshape(1, C))


def _rwkv_time_mix(x2d, w_rwkv, mu, w0, w2, a0, a2, k_k, k_a, r_k, gn_g, gn_b, batch, seq_len):
    r, k, v, lw, a, b, bonus = _rwkv_prep(x2d, w_rwkv, mu, w0, w2, a0, a2, k_k, k_a, r_k, seq_len)
    return _rwkv_mix(r, k, v, lw, a, b, bonus, gn_g, gn_b, batch)


def _rope_tables(seq_len):
    half = ROPE_DIM // 2
    inv = ROPE_THETA ** (-jnp.arange(half, dtype=jnp.float32) * 2.0 / ROPE_DIM)
    ang = jnp.arange(seq_len).astype(jnp.float32)[:, None] * inv[None, :]
    cos, sin = jnp.cos(ang), jnp.sin(ang)
    pad = jnp.zeros((seq_len, HEAD_DIM - ROPE_DIM), jnp.float32)
    zero = jnp.zeros_like(sin)
    c = jnp.concatenate([cos, cos, pad + 1.0], axis=1)
    s_lo = jnp.concatenate([-sin, zero, pad], axis=1)
    s_hi = jnp.concatenate([zero, sin, pad], axis=1)
    two = lambda t: jnp.concatenate([t, t], axis=1)
    return two(c), two(s_lo), two(s_hi)


def _rope_pair(x, c, s_lo, s_hi):
    return x * c + pltpu.roll(x, V7X_LANES - ROPE_DIM // 2, 1) * s_lo + pltpu.roll(x, ROPE_DIM // 2, 1) * s_hi


def _nsa_prep_kernel(x_ref, w_ref, c_ref, slo_ref, shi_ref,
                     q_out, kc_out, vc_out, ks_out, vs_out, kw_out, vw_out, gate_out):
    p = jnp.dot(x_ref[...].astype(jnp.bfloat16), w_ref[...], preferred_element_type=jnp.float32)
    c, s_lo, s_hi = c_ref[...], slo_ref[...], shi_ref[...]
    L = V7X_LANES
    rope = lambda t: _rope_pair(t, c, s_lo, s_hi)
    for j in range(NSA_DIM // L):
        q_out[:, j * L:(j + 1) * L] = rope(p[:, j * L:(j + 1) * L]) * (HEAD_DIM ** -0.5)
    kv = lambda i: p[:, NSA_DIM + i * L:NSA_DIM + (i + 1) * L]
    kc_out[...] = rope(kv(0))
    vc_out[...] = kv(1)
    ks_out[...] = rope(kv(2)).astype(ks_out.dtype)
    vs_out[...] = kv(3).astype(vs_out.dtype)
    kw_out[...] = rope(kv(4)).astype(kw_out.dtype)
    vw_out[...] = kv(5).astype(vw_out.dtype)
    gate_out[...] = jax.nn.sigmoid(kv(6))


def _nsa_prep(x2d, w_nsa, seq_len, tm=256):
    M = x2d.shape[0]
    L = V7X_LANES
    seq_tiles = seq_len // tm
    tabs = _rope_tables(seq_len)
    tab_spec = pl.BlockSpec((tm, L), lambda i: (i % seq_tiles, 0))
    narrow = pl.BlockSpec((tm, L), lambda i: (i, 0))
    f32, bf16 = jnp.float32, jnp.bfloat16
    sds = lambda w, dt: jax.ShapeDtypeStruct((M, w), dt)
    return pl.pallas_call(
        _nsa_prep_kernel,
        grid=(M // tm,),
        in_specs=[pl.BlockSpec((tm, D_MODEL), lambda i: (i, 0)), _full((D_MODEL, NSA_COLS_PAD)),
                  tab_spec, tab_spec, tab_spec],
        out_specs=[pl.BlockSpec((tm, NSA_DIM), lambda i: (i, 0))] + [narrow] * 7,
        out_shape=[sds(NSA_DIM, f32), sds(L, f32), sds(L, f32), sds(L, bf16), sds(L, bf16),
                   sds(L, bf16), sds(L, bf16), sds(L, f32)],
        compiler_params=_params("parallel"),
        name="nsa_prep",
    )(x2d, w_nsa, *tabs)


def _gelu_tanh(x):
    return 0.5 * x * (1.0 + jnp.tanh(math.sqrt(2.0 / math.pi) * (x + 0.044715 * x * x * x)))


def _nsa_compress_kernel(xk_ref, xv_ref, pek_ref, pev_ref, kw1_ref, kw2_ref, vw1_ref, vw2_ref,
                         kc_out, vc_out):
    half = CMP_STRIDE * HEAD_DIM

    def mlp(x, pe, w1_ref, w2_ref):
        lo = jnp.dot((x + pe[:, :half]).astype(jnp.bfloat16), w1_ref[:half, :],
                     preferred_element_type=jnp.float32)
        hi = jnp.dot((x + pe[:, half:]).astype(jnp.bfloat16), w1_ref[half:, :],
                     preferred_element_type=jnp.float32)
        n = x.shape[0]
        pre = lo + pltpu.roll(hi, n - 1, 0)
        return jnp.dot(_gelu_tanh(pre).astype(jnp.bfloat16), w2_ref[...],
                       preferred_element_type=jnp.float32)

    kc_out[0, 0] = mlp(xk_ref[0, 0], pek_ref[...], kw1_ref, kw2_ref).astype(kc_out.dtype)
    vc_out[0, 0] = mlp(xv_ref[0, 0], pev_ref[...], vw1_ref, vw2_ref).astype(vc_out.dtype)


def _nsa_compress(xk, xv, pe_k, pe_v, ck_w1, ck_w2, cv_w1, cv_w2):
    B, G, NC, W = xk.shape
    bf16 = jnp.bfloat16
    xin = pl.BlockSpec((1, 1, NC, W), lambda b, g: (b, g, 0, 0))
    xout = pl.BlockSpec((1, 1, NC, HEAD_DIM), lambda b, g: (b, g, 0, 0))
    out = jax.ShapeDtypeStruct((B, G, NC, HEAD_DIM), bf16)
    return pl.pallas_call(
        _nsa_compress_kernel,
        grid=(B, G),
        in_specs=[xin, xin, _full((1, 2 * W)), _full((1, 2 * W)), _full((2 * W, CMP_HIDDEN)),
                  _full((CMP_HIDDEN, HEAD_DIM)), _full((2 * W, CMP_HIDDEN)), _full((CMP_HIDDEN, HEAD_DIM))],
        out_specs=[xout, xout],
        out_shape=[out, out],
        compiler_params=_params("parallel", "parallel"),
        name="nsa_compress",
    )(xk, xv, pe_k.reshape(1, -1), pe_v.reshape(1, -1), ck_w1.astype(bf16), ck_w2.astype(bf16),
      cv_w1.astype(bf16), cv_w2.astype(bf16))


SEL_KEYS = 512
SEL_SLOTS = V7X_SUBLANES


def _nsa_attn_kernel(qT_ref, kc_ref, vcT_ref, ovT_ref, ks_ref, vsT_ref, kw_ref, vwT_ref, gT_ref,
                     o_ref, val_ref, cnt_ref, sel_ref, qa_ref, s_ref, cm_ref, m_ref, l_ref, acc_ref):
    qb = pl.program_id(2)
    t0 = qb * Q_BLOCK
    R, Q = NSA_R, Q_BLOCK
    RQ = R * Q
    f32, bf16 = jnp.float32, jnp.bfloat16
    qT = qT_ref[0, 0, 0]
    lane = lax.broadcasted_iota(jnp.int32, (1, RQ), 1)
    tpos = t0 + lane % Q
    mm = lambda a, b: jnp.dot(a, b, preferred_element_type=f32)

    NC = kc_ref.shape[2]
    s = mm(kc_ref[0, 0], qT)
    cend = lax.broadcasted_iota(jnp.int32, (NC, 1), 0) * CMP_STRIDE + (CMP_BLOCK - 1)
    mask = cend <= tpos
    sm = jnp.where(mask, s, NEG)
    e = jnp.where(mask, jnp.exp(sm - jnp.max(sm, axis=0, keepdims=True)), 0.0)
    den = jnp.sum(e, axis=0, keepdims=True)
    p = e / jnp.where(den > 0.0, den, 1.0)
    o_c = mm(vcT_ref[0, 0], p.astype(bf16))
    psum = p[:, :Q]
    for r in range(1, R):
        psum = psum + p[:, r * Q:(r + 1) * Q]
    imp = jnp.dot(ovT_ref[...], psum, precision=HI, preferred_element_type=f32)

    span = WINDOW + Q
    w0 = pl.multiple_of(jnp.maximum(t0 - WINDOW, 0), Q)
    s = mm(kw_ref[0, 0, pl.ds(w0, span), :], qT)
    diff = tpos - (w0 + lax.broadcasted_iota(jnp.int32, (span, 1), 0))
    sm = jnp.where((diff >= 0) & (diff < WINDOW), s, NEG)
    p = jnp.exp(sm - jnp.max(sm, axis=0, keepdims=True))
    den = jnp.sum(p, axis=0, keepdims=True)
    o_w = mm(vwT_ref[0, 0, :, pl.ds(w0, span)], p.astype(bf16)) / den

    g = gT_ref[0, 0, 0]
    o_ref[0, 0, 0] = g[0:1, :] * o_c + g[2:3, :] * o_w

    NSB = ovT_ref.shape[0]
    tq = t0 + lax.broadcasted_iota(jnp.int32, (1, Q), 1)
    jblk = lax.broadcasted_iota(jnp.int32, (NSB, 1), 0)
    cur = tq // SEL_BLOCK
    forced = (jblk == 0) | (jblk == cur) | (jblk == cur - 1)
    valid = jblk * SEL_BLOCK <= tq
    val = jnp.where(valid, jnp.where(forced, BIG, imp), NEG)
    val_ref[...] = val
    n_live = (t0 + Q - 1) // SEL_BLOCK + 1

    def strict_body(u, cnt):
        for i in (2 * u, 2 * u + 1):
            cnt = cnt + jnp.where(val_ref[pl.ds(i, 1), :] > val, 1, 0)
        return cnt

    cnt_ref[...] = lax.fori_loop(0, n_live // 2, strict_body, jnp.zeros((NSB, Q), jnp.int32))
    taken = jnp.sum(jnp.where(valid & (cnt_ref[...] < N_SELECT), 1, 0), axis=0, keepdims=True)

    @pl.when(jnp.max(taken) > N_SELECT)
    def _():
        def tie_body(i, cnt):
            row = val_ref[pl.ds(i, 1), :]
            ge = jnp.where(row >= val, 1, 0)
            gt = jnp.where(row > val, 1, 0)
            return cnt + jnp.where(jblk > i, ge, gt)

        cnt_ref[...] = lax.fori_loop(0, n_live, tie_body, jnp.zeros((NSB, Q), jnp.int32))

    bias = jnp.where(cnt_ref[...] < N_SELECT, 0.0, NEG)
    sel_ref[...] = jnp.concatenate([bias] * R, axis=1)

    bps = SEL_SLOTS
    for buf in range(qa_ref.shape[0]):
        qa_ref[buf, :HEAD_DIM, :] = qT
        qa_ref[buf, HEAD_DIM:, :] = jnp.zeros((qa_ref.shape[1] - HEAD_DIM, RQ), bf16)
    m_ref[...] = jnp.full(m_ref.shape, NEG, f32)
    l_ref[...] = jnp.zeros(l_ref.shape, f32)
    acc_ref[...] = jnp.zeros(acc_ref.shape, f32)
    krow = lax.broadcasted_iota(jnp.int32, (SEL_KEYS, 1), 0)

    def scores(kc, slot):
        k0 = pl.multiple_of(kc * SEL_KEYS, SEL_KEYS)
        grp = pl.multiple_of((kc * (SEL_KEYS // SEL_BLOCK)) // bps * bps, bps)
        rows = sel_ref[pl.ds(grp, bps), :]
        qa_ref[slot, HEAD_DIM:HEAD_DIM + 2 * bps, :] = (
            jnp.concatenate([rows, jnp.zeros_like(rows)], axis=0).astype(bf16))
        s = mm(ks_ref[0, 0, pl.ds(k0, SEL_KEYS), :], qa_ref[slot])
        s_ref[slot] = s
        cm_ref[slot] = jnp.max(s, axis=0, keepdims=True)

    def accumulate(kc, slot, diagonal):
        k0 = pl.multiple_of(kc * SEL_KEYS, SEL_KEYS)
        s, cm = s_ref[slot], cm_ref[slot]
        if diagonal:
            s = jnp.where(k0 + krow <= tpos, s, NEG)
            cm = jnp.max(s, axis=0, keepdims=True)
        m = m_ref[...]
        m_new = jnp.maximum(m, cm)
        alpha = jnp.exp(m - m_new)
        p = jnp.exp(s - m_new)
        l_ref[...] = alpha * l_ref[...] + jnp.sum(p, axis=0, keepdims=True)
        m_ref[...] = m_new
        acc_ref[...] = alpha * acc_ref[...] + mm(vsT_ref[0, 0, :, pl.ds(k0, SEL_KEYS)], p.astype(bf16))

    last = (t0 + Q - 1) // SEL_KEYS
    scores(0, 0)

    def sel_body(j, carry):
        scores(2 * j + 1, 1)
        accumulate(2 * j, 0, False)
        scores(2 * j + 2, 0)
        accumulate(2 * j + 1, 1, False)
        return carry

    lax.fori_loop(0, last // 2, sel_body, 0)

    @pl.when(last % 2 == 1)
    def _():
        scores(last, 1)
        accumulate(last - 1, 0, False)
        accumulate(last, 1, True)

    @pl.when(last % 2 == 0)
    def _():
        accumulate(last, 0, True)

    o_ref[0, 0, 0] = o_ref[0, 0, 0] + g[1:2, :] * (acc_ref[...] / l_ref[...])


def _nsa_attention(qT, kc, vcT, ks, vsT, kw, vwT, gT):
    B, G, NQB, _, RQ = qT.shape
    S = ks.shape[2]
    NC = kc.shape[2]
    NSB = S // SEL_BLOCK
    c = jnp.arange(NC)[None, :] * CMP_STRIDE
    j = jnp.arange(NSB)[:, None] * SEL_BLOCK
    ovT = ((c <= j + SEL_BLOCK - 1) & (c + CMP_BLOCK - 1 >= j)).astype(jnp.float32)
    per_q = lambda rows: pl.BlockSpec((1, 1, 1, rows, RQ), lambda b, g, q: (b, g, q, 0, 0))
    per_g = lambda d0, d1: pl.BlockSpec((1, 1, d0, d1), lambda b, g, q: (b, g, 0, 0))
    return pl.pallas_call(
        _nsa_attn_kernel,
        grid=(B, G, NQB),
        in_specs=[per_q(HEAD_DIM), per_g(NC, HEAD_DIM), per_g(HEAD_DIM, NC), _full((NSB, NC)),
                  per_g(S, V7X_LANES), per_g(HEAD_DIM, S), per_g(S, HEAD_DIM), per_g(HEAD_DIM, S),
                  per_q(V7X_SUBLANES)],
        out_specs=per_q(HEAD_DIM),
        out_shape=jax.ShapeDtypeStruct((B, G, NQB, HEAD_DIM, RQ), jnp.float32),
        scratch_shapes=[pltpu.VMEM((NSB, Q_BLOCK), jnp.float32), pltpu.VMEM((NSB, Q_BLOCK), jnp.int32),
                        pltpu.VMEM((NSB, RQ), jnp.float32),
                        pltpu.VMEM((2, V7X_LANES, RQ), jnp.bfloat16),
                        pltpu.VMEM((2, SEL_KEYS, RQ), jnp.float32), pltpu.VMEM((2, 1, RQ), jnp.float32),
                        pltpu.VMEM((1, RQ), jnp.float32), pltpu.VMEM((1, RQ), jnp.float32),
                        pltpu.VMEM((HEAD_DIM, RQ), jnp.float32)],
        compiler_params=_params("parallel", "parallel", "arbitrary"),
        name="nsa_attention",
    )(qT, kc, vcT, ovT, ks, vsT, kw, vwT, gT)


def _nsa_branch(x2d, w_nsa, pe_k, pe_v, ck_w1, ck_w2, cv_w1, cv_w2, batch, seq_len):
    B, S, G, R, N, Q = batch, seq_len, NSA_GROUPS, NSA_R, HEAD_DIM, Q_BLOCK
    NQB = S // Q
    q, kc_in, vc_in, ks, vs, kw, vw, gates = _nsa_prep(x2d, w_nsa, S)
    per_group = lambda t: t.reshape(B, S, G, N).transpose(0, 2, 1, 3)
    per_group_t = lambda t: t.reshape(B, S, G, N).transpose(0, 2, 3, 1)
    blocks = lambda t: per_group(t).reshape(B, G, S // CMP_STRIDE, CMP_STRIDE * N)
    kc, vc = _nsa_compress(blocks(kc_in), blocks(vc_in), pe_k, pe_v, ck_w1, ck_w2, cv_w1, cv_w2)
    qT = (q.astype(jnp.bfloat16).reshape(B, NQB, Q, G, R, N).transpose(0, 3, 1, 5, 4, 2)
          .reshape(B, G, NQB, N, R * Q))
    g3 = gates[:, :3 * NSA_Q_HEADS].reshape(B, NQB, Q, G, R, 3).transpose(0, 3, 1, 5, 4, 2)
    gT = jnp.pad(g3.reshape(B, G, NQB, 3, R * Q), ((0, 0),) * 3 + ((0, V7X_SUBLANES - 3), (0, 0)))
    slot = (jnp.arange(S) // SEL_BLOCK) % SEL_SLOTS
    onehot = (slot[:, None] == jnp.arange(V7X_LANES - N)[None, :]).astype(ks.dtype)
    ks_aug = jnp.concatenate([per_group(ks), jnp.broadcast_to(onehot, (B, G, S, V7X_LANES - N))], axis=-1)
    oT = _nsa_attention(qT, kc, vc.transpose(0, 1, 3, 2), ks_aug, per_group_t(vs),
                        per_group(kw), per_group_t(vw), gT)
    return oT.reshape(B, G, NQB, N, R, Q).transpose(0, 2, 5, 1, 4, 3).reshape(B * S, NSA_DIM)


def _merge_kernel(x_ref, ya_ref, yb_ref, wg_ref, pa_ref, pb_ref, wo_ref, g_ref, b_ref, o_ref):
    bf16 = jnp.bfloat16
    mm = lambda a, w: jnp.dot(a.astype(bf16), w, preferred_element_type=jnp.float32)
    x = x_ref[...]
    gates = jax.nn.sigmoid(mm(x, wg_ref[...]))
    mixed = gates[:, :D_MODEL] * mm(ya_ref[...], pa_ref[...]) + gates[:, D_MODEL:] * mm(yb_ref[...], pb_ref[...])
    o_ref[...] = _layer_norm(ALPHA * x + mm(mixed, wo_ref[...]), g_ref[...], b_ref[...])


def _merge(x2d, y_a, y_b, w_gate, p_a, p_b, w_o, ln_g, ln_b, tm=256):
    M = x2d.shape[0]
    bf16 = jnp.bfloat16
    rows = lambda w: pl.BlockSpec((tm, w), lambda i: (i, 0))
    return pl.pallas_call(
        _merge_kernel,
        grid=(M // tm,),
        in_specs=[rows(D_MODEL), rows(RWKV_DIM), rows(NSA_DIM), _full((D_MODEL, 2 * D_MODEL)),
                  _full((RWKV_DIM, D_MODEL)), _full((NSA_DIM, D_MODEL)), _full((D_MODEL, D_MODEL)),
                  _full((1, D_MODEL)), _full((1, D_MODEL))],
        out_specs=rows(D_MODEL),
        out_shape=jax.ShapeDtypeStruct((M, D_MODEL), jnp.float32),
        compiler_params=_params("parallel"),
        name="merge",
    )(x2d, y_a, y_b, w_gate, p_a.astype(bf16), p_b.astype(bf16), w_o.astype(bf16),
      ln_g.reshape(1, -1), ln_b.reshape(1, -1))


def _mem_kv_kernel(mem_ref, wk_ref, wv_ref, k_out, v_out):
    m = mem_ref[...].astype(jnp.bfloat16)
    k_out[...] = jnp.dot(m, wk_ref[...], preferred_element_type=jnp.float32).astype(k_out.dtype)
    v_out[...] = jnp.dot(m, wv_ref[...], preferred_element_type=jnp.float32).astype(v_out.dtype)


def _mem_kv(mem2d, wk, wv):
    M = mem2d.shape[0]
    bf16 = jnp.bfloat16
    out = jax.ShapeDtypeStruct((M, D_MODEL), bf16)
    return pl.pallas_call(
        _mem_kv_kernel,
        grid=(1,),
        in_specs=[_full((M, D_MODEL)), _full((D_MODEL, D_MODEL)), _full((D_MODEL, D_MODEL))],
        out_specs=[_full((M, D_MODEL))] * 2,
        out_shape=[out, out],
        compiler_params=_params("arbitrary"),
        name="mem_kv",
    )(mem2d, wk.astype(bf16), wv.astype(bf16))


def _xattn_kernel(x_ref, k_ref, v_ref, wq_ref, wo_ref, g_ref, b_ref, o_ref):
    bf16, f32 = jnp.bfloat16, jnp.float32
    x = x_ref[...]
    q = jnp.dot(x.astype(bf16), wq_ref[...], preferred_element_type=f32).astype(bf16)
    heads = []
    for h in range(X_HEADS):
        sl = slice(h * X_HEAD_DIM, (h + 1) * X_HEAD_DIM)
        s = lax.dot_general(q[:, sl], k_ref[:, sl], (((1,), (1,)), ((), ())),
                            preferred_element_type=f32) * (X_HEAD_DIM ** -0.5)
        p = jnp.exp(s - jnp.max(s, axis=-1, keepdims=True))
        p = p / jnp.sum(p, axis=-1, keepdims=True)
        heads.append(jnp.dot(p.astype(bf16), v_ref[:, sl], preferred_element_type=f32))
    o = jnp.concatenate(heads, axis=1).astype(bf16)
    xa = jnp.dot(o, wo_ref[...], preferred_element_type=f32)
    o_ref[...] = _layer_norm(ALPHA * x + xa, g_ref[...], b_ref[...])


def _xattn(x2d, k_mem, v_mem, wq, wo, ln_g, ln_b, seq_len, mem_len, tm=256):
    M = x2d.shape[0]
    bf16 = jnp.bfloat16
    seq_tiles = seq_len // tm
    rows = pl.BlockSpec((tm, D_MODEL), lambda i: (i, 0))
    mem_spec = pl.BlockSpec((mem_len, D_MODEL), lambda i: (i // seq_tiles, 0))
    return pl.pallas_call(
        _xattn_kernel,
        grid=(M // tm,),
        in_specs=[rows, mem_spec, mem_spec, _full((D_MODEL, D_MODEL)), _full((D_MODEL, D_MODEL)),
                  _full((1, D_MODEL)), _full((1, D_MODEL))],
        out_specs=rows,
        out_shape=jax.ShapeDtypeStruct((M, D_MODEL), jnp.float32),
        compiler_params=_params("parallel"),
        name="xattn",
    )(x2d, k_mem, v_mem, wq.astype(bf16), wo.astype(bf16), ln_g.reshape(1, -1), ln_b.reshape(1, -1))


FFN_CHUNK = 256


def _ffn_kernel(seq_tiles, x_ref, xp_ref, wup_ref, cw_ref, cb_ref, wdn_ref, g_ref, b_ref, o_ref):
    bf16, f32 = jnp.bfloat16, jnp.float32
    i = pl.program_id(0)
    tm = x_ref.shape[0]
    H = V7X_SUBLANES
    x = x_ref[...]
    xprev = jnp.where(i % seq_tiles == 0, 0.0, xp_ref[...])
    xe = jnp.concatenate([xprev, x], axis=0).astype(bf16)

    def conv(cols):
        h = jnp.dot(xe, wup_ref[:, cols], preferred_element_type=f32)
        w = cw_ref[:, cols]
        return (h[H - 2:H - 2 + tm] * w[0:1] + h[H - 1:H - 1 + tm] * w[1:2] + h[H:] * w[2:3]
                + cb_ref[:, cols])

    acc = jnp.zeros((tm, D_MODEL), f32)
    for c in range(D_FF // FFN_CHUNK):
        gate = conv(slice(c * FFN_CHUNK, (c + 1) * FFN_CHUNK))
        val = conv(slice(D_FF + c * FFN_CHUNK, D_FF + (c + 1) * FFN_CHUNK))
        act = (gate * jax.nn.sigmoid(gate) * val).astype(bf16)
        acc = acc + jnp.dot(act, wdn_ref[c * FFN_CHUNK:(c + 1) * FFN_CHUNK, :], preferred_element_type=f32)
    o_ref[...] = _layer_norm(ALPHA * x + acc, g_ref[...], b_ref[...])


def _ffn(x2d, w_up, conv_w, conv_b, w_down, ln_g, ln_b, seq_len, tm=256):
    M = x2d.shape[0]
    bf16 = jnp.bfloat16
    seq_tiles = seq_len // tm
    blocks_per_tile = tm // V7X_SUBLANES
    rows = pl.BlockSpec((tm, D_MODEL), lambda i: (i, 0))
    once = lambda shape: pl.BlockSpec(shape, lambda i: (0,) * len(shape), pipeline_mode=pl.Buffered(1))
    return pl.pallas_call(
        functools.partial(_ffn_kernel, seq_tiles),
        grid=(M // tm,),
        in_specs=[rows,
                  pl.BlockSpec((V7X_SUBLANES, D_MODEL), lambda i: (jnp.maximum(i * blocks_per_tile - 1, 0), 0)),
                  once((D_MODEL, 2 * D_FF)), _full((3, 2 * D_FF)), _full((1, 2 * D_FF)),
                  once((D_FF, D_MODEL)), _full((1, D_MODEL)), _full((1, D_MODEL))],
        out_specs=rows,
        out_shape=jax.ShapeDtypeStruct((M, D_MODEL), jnp.float32),
        compiler_params=_params("parallel"),
        name="ffn",
    )(x2d, x2d, w_up.astype(bf16), conv_w, conv_b.reshape(1, -1), w_down.astype(bf16),
      ln_g.reshape(1, -1), ln_b.reshape(1, -1))


def kernel(x, mem, w_in, rwkv_mu, rwkv_w0, rwkv_w2, rwkv_a0, rwkv_a2, rwkv_k_k, rwkv_k_a, rwkv_r_k, rwkv_gn_g, rwkv_gn_b, nsa_pe_k, nsa_pe_v, nsa_ck_w1, nsa_ck_w2, nsa_cv_w1, nsa_cv_w2, merge_p_a, merge_p_b, mix_w_o, ln1_g, ln1_b, xa_wq, xa_wk, xa_wv, xa_wo, ln2_g, ln2_b, ffn_w_up, ffn_conv_w, ffn_conv_b, ffn_w_down, ln3_g, ln3_b):
    B, S, _ = x.shape
    mem_len = mem.shape[1]
    bf16 = jnp.bfloat16
    x2d = x.reshape(B * S, D_MODEL)
    for l in range(DEPTH):
        w = w_in[l]
        w_rwkv = w[:, :RWKV_COLS].astype(bf16)
        w_nsa = jnp.pad(w[:, RWKV_COLS:RWKV_COLS + NSA_COLS], ((0, 0), (0, NSA_COLS_PAD - NSA_COLS))).astype(bf16)
        w_gate = w[:, RWKV_COLS + NSA_COLS:].astype(bf16)
        y_a = _rwkv_time_mix(x2d, w_rwkv, rwkv_mu[l], rwkv_w0[l], rwkv_w2[l], rwkv_a0[l], rwkv_a2[l],
                             rwkv_k_k[l], rwkv_k_a[l], rwkv_r_k[l], rwkv_gn_g[l], rwkv_gn_b[l], B, S)
        y_b = _nsa_branch(x2d, w_nsa, nsa_pe_k[l], nsa_pe_v[l], nsa_ck_w1[l], nsa_ck_w2[l],
                          nsa_cv_w1[l], nsa_cv_w2[l], B, S)
        x2d = _merge(x2d, y_a, y_b, w_gate, merge_p_a[l], merge_p_b[l], mix_w_o[l], ln1_g[l], ln1_b[l])
        k_mem, v_mem = _mem_kv(mem.reshape(B * mem_len, D_MODEL), xa_wk[l], xa_wv[l])
        x2d = _xattn(x2d, k_mem, v_mem, xa_wq[l], xa_wo[l], ln2_g[l], ln2_b[l], S, mem_len)
        x2d = _ffn(x2d, ffn_w_up[l], ffn_conv_w[l], ffn_conv_b[l], ffn_w_down[l], ln3_g[l], ln3_b[l], S)
    return x2d.reshape(B, S, D_MODEL)
```

```python
import functools
import math

import jax
import jax.numpy as jnp
import numpy as np
from jax import lax
from jax.experimental import pallas as pl
from jax.experimental.pallas import tpu as pltpu

D_MODEL = 1024
HEAD_DIM = 64
RWKV_DIM = 512
RWKV_HEADS = 8
LORA = 64
RWKV_COLS = 3 * RWKV_DIM + 2 * LORA
GN_EPS = 64e-5
NSA_DIM = 512
NSA_Q_HEADS = 8
NSA_GROUPS = 2
NSA_R = NSA_Q_HEADS // NSA_GROUPS
NSA_KV_DIM = NSA_GROUPS * HEAD_DIM
NSA_COLS = NSA_DIM + 6 * NSA_KV_DIM + 3 * NSA_Q_HEADS
NSA_COLS_PAD = 1408
CMP_BLOCK = 32
CMP_STRIDE = 16
CMP_HIDDEN = 256
SEL_BLOCK = 64
N_SELECT = 16
WINDOW = 512
Q_BLOCK = 128
ROPE_THETA = 500000.0
ROPE_DIM = 16
X_HEADS = 4
X_HEAD_DIM = 256
D_FF = 2816
LN_EPS = 1e-5
DEPTH = 1
ALPHA = (2 * DEPTH) ** 0.25
NEG = -1e30
BIG = 1e30

V7X_LANES = 128
V7X_SUBLANES = 8
V7X_VMEM_LIMIT_BYTES = 56 * 1024 * 1024

HI = lax.Precision.HIGHEST
RWKV_CHUNK = 64


def _params(*sem):
    return pltpu.CompilerParams(dimension_semantics=sem, vmem_limit_bytes=V7X_VMEM_LIMIT_BYTES)


def _full(shape):
    n = len(shape)
    return pl.BlockSpec(shape, lambda *_: (0,) * n)


def _head_ones(width):
    r = lax.broadcasted_iota(jnp.int32, (width, width), 0) // HEAD_DIM
    c = lax.broadcasted_iota(jnp.int32, (width, width), 1) // HEAD_DIM
    return (r == c).astype(jnp.float32)


def _layer_norm(y, g, b):
    mu = jnp.mean(y, axis=-1, keepdims=True)
    d = y - mu
    var = jnp.mean(d * d, axis=-1, keepdims=True)
    return d * lax.rsqrt(var + LN_EPS) * g + b


def _rwkv_prep_kernel(seq_tiles, x_ref, xp_ref, w_ref, mu_ref, w0_ref, w2_ref, a0_ref, a2_ref,
                      kk_ref, ka_ref, rk_ref,
                      r_out, k_out, v_out, lw_out, a_out, b_out, bonus_out):
    i = pl.program_id(0)
    tm = x_ref.shape[0]
    C = RWKV_DIM
    w = w_ref[...]
    p = jnp.dot(x_ref[...].astype(jnp.bfloat16), w, preferred_element_type=jnp.float32)
    xprev = xp_ref[...].astype(jnp.bfloat16)
    pprev = jnp.dot(xprev, w, preferred_element_type=jnp.float32)[V7X_SUBLANES - 1:V7X_SUBLANES, :]
    pprev = jnp.where(i % seq_tiles == 0, 0.0, pprev)
    row = lax.broadcasted_iota(jnp.int32, (tm, 1), 0)
    shifted = jnp.where(row == 0, pprev, pltpu.roll(p, 1, 0))
    p = p + (shifted - p) * mu_ref[...]
    r, k, v = p[:, :C], p[:, C:2 * C], p[:, 2 * C:3 * C]
    wl = p[:, 3 * C:3 * C + LORA]
    al = p[:, 3 * C + LORA:]
    z = -(w0_ref[...] + jnp.dot(jnp.tanh(wl).astype(jnp.bfloat16), w2_ref[...],
                                preferred_element_type=jnp.float32))
    softplus = jnp.maximum(z, 0.0) + jnp.log(1.0 + jnp.exp(-jnp.abs(z)))
    w_log = -softplus - 0.5
    lw_out[...] = -jnp.exp(w_log)
    a = jax.nn.sigmoid(a0_ref[...] + jnp.dot(al.astype(jnp.bfloat16), a2_ref[...],
                                             preferred_element_type=jnp.float32))
    ones = _head_ones(C).astype(jnp.bfloat16)
    ones2 = jnp.concatenate([ones, ones], axis=0)

    def head_sum(t):
        return jnp.dot(jnp.concatenate(_split(t), axis=1), ones2, preferred_element_type=jnp.float32)

    kk = k * kk_ref[...]
    kk = kk / jnp.maximum(jnp.sqrt(head_sum(kk * kk)), 1e-12)
    kmod = k * (1.0 + (a - 1.0) * ka_ref[...])
    bonus = head_sum(r * kmod * rk_ref[...]) * v
    r_out[...] = r
    k_out[...] = kmod
    v_out[...] = v
    a_out[...] = -kk
    b_out[...] = kk * a
    bonus_out[...] = bonus


def _rwkv_prep(x2d, w_rwkv, mu, w0, w2, a0, a2, k_k, k_a, r_k, seq_len, tm=512):
    M = x2d.shape[0]
    C = RWKV_DIM
    seq_tiles = seq_len // tm
    row = lambda a: a.reshape(1, -1)
    out = jax.ShapeDtypeStruct((M, C), jnp.float32)
    tile = pl.BlockSpec((tm, C), lambda i: (i, 0))
    blocks_per_tile = tm // V7X_SUBLANES
    return pl.pallas_call(
        functools.partial(_rwkv_prep_kernel, seq_tiles),
        grid=(M // tm,),
        in_specs=[
            pl.BlockSpec((tm, D_MODEL), lambda i: (i, 0)),
            pl.BlockSpec((V7X_SUBLANES, D_MODEL), lambda i: (jnp.maximum(i * blocks_per_tile - 1, 0), 0)),
            _full((D_MODEL, RWKV_COLS)), _full((1, RWKV_COLS)), _full((1, C)), _full((LORA, C)),
            _full((1, C)), _full((LORA, C)), _full((1, C)), _full((1, C)), _full((1, C)),
        ],
        out_specs=[tile] * 7,
        out_shape=[out] * 7,
        compiler_params=_params("parallel"),
        name="rwkv_prep",
    )(x2d, x2d, w_rwkv, row(mu), row(w0), w2.astype(jnp.bfloat16), row(a0), a2.astype(jnp.bfloat16),
      row(k_k), row(k_a), row(r_k))


RWKV_GROUP = 4


def _split(x):
    hi = x.astype(jnp.bfloat16)
    return hi, (x - hi.astype(jnp.float32)).astype(jnp.bfloat16)


def _mm(a, b):
    return jnp.dot(a.astype(jnp.bfloat16), b.astype(jnp.bfloat16), preferred_element_type=jnp.float32)


def _rwkv_mix_kernel(r_ref, k_ref, v_ref, lw_ref, a_ref, b_ref, bonus_ref, gng_ref, gnb_ref,
                     y_out, state):
    T, N, GH = RWKV_CHUNK, HEAD_DIM, RWKV_GROUP
    W = GH * N
    f32 = jnp.float32

    @pl.when(pl.program_id(1) == 0)
    def _():
        state[...] = jnp.zeros_like(state)

    ri = lax.broadcasted_iota(jnp.int32, (W, W), 0)
    ci = lax.broadcasted_iota(jnp.int32, (W, W), 1)
    same_head = ri // N == ci // N
    strict, incl = ri > ci, ri >= ci
    eye = (ri == ci).astype(f32)
    tri = (lax.broadcasted_iota(jnp.int32, (T, T), 0)
           >= lax.broadcasted_iota(jnp.int32, (T, T), 1)).astype(jnp.bfloat16)
    mean_w = (_head_ones(RWKV_DIM) * (1.0 / N)).astype(jnp.bfloat16)

    def expand(x):
        return jnp.where(same_head, jnp.concatenate([x] * GH, axis=0), jnp.zeros((), x.dtype))

    def collapse(x):
        out = x[:T]
        for h in range(1, GH):
            out = out + x[h * T:(h + 1) * T]
        return out

    bf = lambda t: t.astype(jnp.bfloat16)
    n_chunks = r_ref.shape[0] // T
    n_groups = RWKV_HEADS // GH
    chains = [(ch, g) for ch in range(n_chunks) for g in range(n_groups)]
    each = lambda fn, *lists: [fn(*args) for args in zip(*lists)]

    def cumsum(x):
        hi, lo = _split(x)
        lo2 = (x - hi.astype(f32) - lo.astype(f32)).astype(jnp.bfloat16)
        return jnp.dot(jnp.concatenate([tri] * 3, axis=1), jnp.concatenate([hi, lo, lo2], axis=0),
                       preferred_element_type=f32)

    c_all = [cumsum(lw_ref[ch * T:(ch + 1) * T, :]) for ch in range(n_chunks)]

    def operands(ch, g):
        rows, cols = slice(ch * T, (ch + 1) * T), slice(g * W, (g + 1) * W)
        lw, c = lw_ref[rows, cols], c_all[ch][:, cols]
        r, k, v = r_ref[rows, cols], k_ref[rows, cols], v_ref[rows, cols]
        a, b = a_ref[rows, cols], b_ref[rows, cols]
        c_last = c[T - 1:T, :]
        e_neg, e_end = jnp.exp(-c), jnp.exp(c_last - c)
        r_t = r * jnp.exp(c)
        return dict(a_x=expand(bf(a * jnp.exp(c - lw))), r_x=expand(bf(r_t)), b_x=expand(bf(b * e_neg)),
                    k_x=expand(bf(k * e_neg)), v_x=expand(bf(v)), r_t=r_t, b_end=b * e_end, k_end=k * e_end,
                    w_end=jnp.exp(c_last))

    ops = [operands(ch, g) for ch, g in chains]
    P = [lax.dot_general(jnp.concatenate([o["a_x"], o["r_x"]], axis=0),
                         jnp.concatenate([o["b_x"], o["k_x"]], axis=0),
                         (((1,), (1,)), ((), ())), preferred_element_type=f32) for o in ops]
    L_ab = [jnp.where(strict, p[:W, :W], 0.0) for p in P]
    L_ak = [bf(jnp.where(strict, p[:W, W:], 0.0)) for p in P]
    M_rb = [bf(jnp.where(incl, p[W:, :W], 0.0)) for p in P]
    M_rk = [bf(jnp.where(incl, p[W:, W:], 0.0)) for p in P]
    v_x = [o["v_x"] for o in ops]

    base = V7X_SUBLANES
    D = [bf(jnp.where(ri // base == ci // base, l, 0.0)) for l in L_ab]
    D2 = each(_mm, D, D)
    D4 = each(_mm, D2, D2)
    X = each(_mm, each(lambda d, d2: _mm(eye + d.astype(f32), eye + d2), D, D2), [eye + d4 for d4 in D4])
    blk = base
    while blk < T:
        pair = (ri // (2 * blk) == ci // (2 * blk)) & (ri // blk != ci // blk)
        Xb = [bf(x) for x in X]
        step = each(_mm, each(_mm, Xb, [jnp.where(pair, l, 0.0) for l in L_ab]), Xb)
        X = [x + s for x, s in zip(X, step)]
        blk *= 2

    lakv = each(_mm, L_ak, v_x)
    AV = [bf(_mm(x, jnp.concatenate([o["a_x"], bf(t)], axis=1))) for x, o, t in zip(X, ops, lakv)]
    ry = each(_mm, M_rb, AV)
    mrkv = each(_mm, M_rk, v_x)
    gh = [_mm(expand(o["b_end"]).T, av) for o, av in zip(ops, AV)]
    kv = [_mm(expand(o["k_end"]).T, vx) for o, vx in zip(ops, v_x)]
    rp = [bf(expand(o["r_t"]) + t[:, :W]) for o, t in zip(ops, ry)]
    yp = [t[:, W:] + u for t, u in zip(ry, mrkv)]
    G = [bf(eye * o["w_end"] + t[:, :W]) for o, t in zip(ops, gh)]
    H = [t[:, W:] + u for t, u in zip(gh, kv)]

    y_rows = []
    for ch in range(n_chunks):
        ys = []
        for g in range(n_groups):
            i = chains.index((ch, g))
            s0 = bf(state[g])
            ys.append(collapse(_mm(rp[i], s0) + yp[i]))
            state[g] = _mm(G[i], s0) + H[i]
        y_rows.append(jnp.concatenate(ys, axis=1))
    y = jnp.concatenate(y_rows, axis=0)
    y_hi, y_lo = _split(y)
    ym = jnp.dot(jnp.concatenate([y_hi, y_lo], axis=1), jnp.concatenate([mean_w, mean_w], axis=0),
                 preferred_element_type=f32)
    d = y - ym
    yv = _mm(d * d, mean_w)
    y_out[...] = d * lax.rsqrt(yv + GN_EPS) * gng_ref[...] + gnb_ref[...] + bonus_ref[...]


def _rwkv_mix(r, k, v, lw, a, b, bonus, gn_g, gn_b, batch, chunks_per_step=4):
    M, C = r.shape
    rows = RWKV_CHUNK * chunks_per_step
    steps = M // rows // batch
    W = RWKV_GROUP * HEAD_DIM
    tile = pl.BlockSpec((rows, C), lambda bi, ci: (bi * steps + ci, 0))
    return pl.pallas_call(
        _rwkv_mix_kernel,
        grid=(batch, steps),
        in_specs=[tile] * 7 + [_full((1, C)), _full((1, C))],
        out_specs=tile,
        out_shape=jax.ShapeDtypeStruct((M, C), jnp.float32),
        scratch_shapes=[pltpu.VMEM((RWKV_HEADS // RWKV_GROUP, W, W), jnp.float32)],
        compiler_params=_params("parallel", "arbitrary"),
        name="rwkv_mix",
    )(r, k, v, lw, a, b, bonus, gn_g.reshape(1, C), gn_b.reshape(1, C))


def _rwkv_time_mix(x2d, w_rwkv, mu, w0, w2, a0, a2, k_k, k_a, r_k, gn_g, gn_b, batch, seq_len):
    r, k, v, lw, a, b, bonus = _rwkv_prep(x2d, w_rwkv, mu, w0, w2, a0, a2, k_k, k_a, r_k, seq_len)
    return _rwkv_mix(r, k, v, lw, a, b, bonus, gn_g, gn_b, batch)


def _rope_tables(seq_len):
    half = ROPE_DIM // 2
    inv = ROPE_THETA ** (-np.arange(half, dtype=np.float64) * 2.0 / ROPE_DIM)
    ang = np.arange(seq_len, dtype=np.float64)[:, None] * inv[None, :]
    cos, sin = np.cos(ang).astype(np.float32), np.sin(ang).astype(np.float32)
    pad = np.zeros((seq_len, HEAD_DIM - ROPE_DIM), np.float32)
    zero = np.zeros_like(sin)
    c = np.concatenate([cos, cos, pad + 1.0], axis=1)
    s_lo = np.concatenate([-sin, zero, pad], axis=1)
    s_hi = np.concatenate([zero, sin, pad], axis=1)
    two = lambda t: jnp.asarray(np.concatenate([t, t], axis=1))
    return two(c), two(s_lo), two(s_hi)


def _rope_pair(x, c, s_lo, s_hi):
    return x * c + pltpu.roll(x, V7X_LANES - ROPE_DIM // 2, 1) * s_lo + pltpu.roll(x, ROPE_DIM // 2, 1) * s_hi


def _nsa_prep_kernel(x_ref, w_ref, c_ref, slo_ref, shi_ref,
                     q_out, kc_out, vc_out, ks_out, vs_out, kw_out, vw_out, gate_out):
    p = jnp.dot(x_ref[...].astype(jnp.bfloat16), w_ref[...], preferred_element_type=jnp.float32)
    c, s_lo, s_hi = c_ref[...], slo_ref[...], shi_ref[...]
    L = V7X_LANES
    rope = lambda t: _rope_pair(t, c, s_lo, s_hi)
    for j in range(NSA_DIM // L):
        q_out[:, j * L:(j + 1) * L] = rope(p[:, j * L:(j + 1) * L]) * (HEAD_DIM ** -0.5)
    kv = lambda i: p[:, NSA_DIM + i * L:NSA_DIM + (i + 1) * L]
    kc_out[...] = rope(kv(0))
    vc_out[...] = kv(1)
    ks_out[...] = rope(kv(2)).astype(ks_out.dtype)
    vs_out[...] = kv(3).astype(vs_out.dtype)
    kw_out[...] = rope(kv(4)).astype(kw_out.dtype)
    vw_out[...] = kv(5).astype(vw_out.dtype)
    gate_out[...] = jax.nn.sigmoid(kv(6))


def _nsa_prep(x2d, w_nsa, seq_len, tm=512):
    M = x2d.shape[0]
    L = V7X_LANES
    seq_tiles = seq_len // tm
    tabs = _rope_tables(seq_len)
    tab_spec = pl.BlockSpec((tm, L), lambda i: (i % seq_tiles, 0))
    narrow = pl.BlockSpec((tm, L), lambda i: (i, 0))
    f32, bf16 = jnp.float32, jnp.bfloat16
    sds = lambda w, dt: jax.ShapeDtypeStruct((M, w), dt)
    return pl.pallas_call(
        _nsa_prep_kernel,
        grid=(M // tm,),
        in_specs=[pl.BlockSpec((tm, D_MODEL), lambda i: (i, 0)), _full((D_MODEL, NSA_COLS_PAD)),
                  tab_spec, tab_spec, tab_spec],
        out_specs=[pl.BlockSpec((tm, NSA_DIM), lambda i: (i, 0))] + [narrow] * 7,
        out_shape=[sds(NSA_DIM, f32), sds(L, f32), sds(L, f32), sds(L, bf16), sds(L, bf16),
                   sds(L, bf16), sds(L, bf16), sds(L, f32)],
        compiler_params=_params("parallel"),
        name="nsa_prep",
    )(x2d, w_nsa, *tabs)


def _gelu_tanh(x):
    return 0.5 * x * (1.0 + jnp.tanh(math.sqrt(2.0 / math.pi) * (x + 0.044715 * x * x * x)))


def _nsa_compress_kernel(xk_ref, xv_ref, pek_ref, pev_ref, kw1_ref, kw2_ref, vw1_ref, vw2_ref,
                         kc_out, vc_out):
    half = CMP_STRIDE * HEAD_DIM

    def mlp(x, pe, w1_ref, w2_ref):
        lo = jnp.dot((x + pe[:, :half]).astype(jnp.bfloat16), w1_ref[:half, :],
                     preferred_element_type=jnp.float32)
        hi = jnp.dot((x + pe[:, half:]).astype(jnp.bfloat16), w1_ref[half:, :],
                     preferred_element_type=jnp.float32)
        n = x.shape[0]
        pre = lo + pltpu.roll(hi, n - 1, 0)
        return jnp.dot(_gelu_tanh(pre).astype(jnp.bfloat16), w2_ref[...],
                       preferred_element_type=jnp.float32)

    kc_out[0, 0] = mlp(xk_ref[0, 0], pek_ref[...], kw1_ref, kw2_ref).astype(kc_out.dtype)
    vc_out[0, 0] = mlp(xv_ref[0, 0], pev_ref[...], vw1_ref, vw2_ref).astype(vc_out.dtype)


def _nsa_compress(xk, xv, pe_k, pe_v, ck_w1, ck_w2, cv_w1, cv_w2):
    B, G, NC, W = xk.shape
    bf16 = jnp.bfloat16
    xin = pl.BlockSpec((1, 1, NC, W), lambda b, g: (b, g, 0, 0))
    xout = pl.BlockSpec((1, 1, NC, HEAD_DIM), lambda b, g: (b, g, 0, 0))
    out = jax.ShapeDtypeStruct((B, G, NC, HEAD_DIM), bf16)
    return pl.pallas_call(
        _nsa_compress_kernel,
        grid=(B, G),
        in_specs=[xin, xin, _full((1, 2 * W)), _full((1, 2 * W)), _full((2 * W, CMP_HIDDEN)),
                  _full((CMP_HIDDEN, HEAD_DIM)), _full((2 * W, CMP_HIDDEN)), _full((CMP_HIDDEN, HEAD_DIM))],
        out_specs=[xout, xout],
        out_shape=[out, out],
        compiler_params=_params("parallel", "parallel"),
        name="nsa_compress",
    )(xk, xv, pe_k.reshape(1, -1), pe_v.reshape(1, -1), ck_w1.astype(bf16), ck_w2.astype(bf16),
      cv_w1.astype(bf16), cv_w2.astype(bf16))


SEL_KEYS = 512
SEL_SLOTS = V7X_SUBLANES


def _nsa_attn_kernel(qT_ref, kc_ref, vcT_ref, ovT_ref, ks_ref, vsT_ref, kw_ref, vwT_ref, gT_ref,
                     o_ref, val_ref, cnt_ref, sel_ref, qa_ref, s_ref, cm_ref, m_ref, l_ref, acc_ref):
    qb = pl.program_id(2)
    t0 = qb * Q_BLOCK
    R, Q = NSA_R, Q_BLOCK
    RQ = R * Q
    f32, bf16 = jnp.float32, jnp.bfloat16
    qT = qT_ref[0, 0, 0]
    lane = lax.broadcasted_iota(jnp.int32, (1, RQ), 1)
    tpos = t0 + lane % Q
    mm = lambda a, b: jnp.dot(a, b, preferred_element_type=f32)

    NC = kc_ref.shape[2]
    span = WINDOW + Q
    w0 = pl.multiple_of(jnp.maximum(t0 - WINDOW, 0), Q)
    s_cmp = mm(kc_ref[0, 0], qT)
    s_win = mm(kw_ref[0, 0, pl.ds(w0, span), :], qT)

    cend = lax.broadcasted_iota(jnp.int32, (NC, 1), 0) * CMP_STRIDE + (CMP_BLOCK - 1)
    mask = cend <= tpos
    sm = jnp.where(mask, s_cmp, NEG)
    e = jnp.where(mask, jnp.exp(sm - jnp.max(sm, axis=0, keepdims=True)), 0.0)
    den = jnp.sum(e, axis=0, keepdims=True)
    p = e / jnp.where(den > 0.0, den, 1.0)
    o_c = mm(vcT_ref[0, 0], p.astype(bf16))
    psum = p[:, :Q]
    for r in range(1, R):
        psum = psum + p[:, r * Q:(r + 1) * Q]
    ps_hi, ps_lo = _split(psum)
    ov = ovT_ref[...]
    imp = mm(jnp.concatenate([ov, ov], axis=1), jnp.concatenate([ps_hi, ps_lo], axis=0))

    diff = tpos - (w0 + lax.broadcasted_iota(jnp.int32, (span, 1), 0))
    sm = jnp.where((diff >= 0) & (diff < WINDOW), s_win, NEG)
    p = jnp.exp(sm - jnp.max(sm, axis=0, keepdims=True))
    den = jnp.sum(p, axis=0, keepdims=True)
    o_w = mm(vwT_ref[0, 0, :, pl.ds(w0, span)], p.astype(bf16)) / den

    g = gT_ref[0, 0, 0]
    o_ref[0, 0, 0] = g[0:1, :] * o_c + g[2:3, :] * o_w

    NSB = ovT_ref.shape[0]
    tq = t0 + lax.broadcasted_iota(jnp.int32, (1, Q), 1)
    jblk = lax.broadcasted_iota(jnp.int32, (NSB, 1), 0)
    cur = tq // SEL_BLOCK
    forced = (jblk == 0) | (jblk == cur) | (jblk == cur - 1)
    valid = jblk * SEL_BLOCK <= tq
    val = jnp.where(valid, jnp.where(forced, BIG, imp), NEG)
    val_ref[...] = val
    n_live = (t0 + Q - 1) // SEL_BLOCK + 1

    def strict_body(u, cnt):
        for i in (2 * u, 2 * u + 1):
            cnt = cnt + jnp.where(val_ref[pl.ds(i, 1), :] > val, 1, 0)
        return cnt

    cnt_ref[...] = lax.fori_loop(0, n_live // 2, strict_body, jnp.zeros((NSB, Q), jnp.int32))
    taken = jnp.sum(jnp.where(valid & (cnt_ref[...] < N_SELECT), 1, 0), axis=0, keepdims=True)

    @pl.when(jnp.max(taken) > N_SELECT)
    def _():
        def tie_body(i, cnt):
            row = val_ref[pl.ds(i, 1), :]
            ge = jnp.where(row >= val, 1, 0)
            gt = jnp.where(row > val, 1, 0)
            return cnt + jnp.where(jblk > i, ge, gt)

        cnt_ref[...] = lax.fori_loop(0, n_live, tie_body, jnp.zeros((NSB, Q), jnp.int32))

    bias = jnp.where(cnt_ref[...] < N_SELECT, 0.0, NEG)
    sel_ref[...] = jnp.concatenate([bias] * R, axis=1)

    bps = SEL_SLOTS
    for buf in range(qa_ref.shape[0]):
        qa_ref[buf, :HEAD_DIM, :] = qT
        qa_ref[buf, HEAD_DIM:, :] = jnp.zeros((qa_ref.shape[1] - HEAD_DIM, RQ), bf16)
    m_ref[...] = jnp.full(m_ref.shape, NEG, f32)
    l_ref[...] = jnp.zeros(l_ref.shape, f32)
    acc_ref[...] = jnp.zeros(acc_ref.shape, f32)
    krow = lax.broadcasted_iota(jnp.int32, (SEL_KEYS, 1), 0)

    def scores(kc, slot):
        k0 = pl.multiple_of(kc * SEL_KEYS, SEL_KEYS)
        grp = pl.multiple_of((kc * (SEL_KEYS // SEL_BLOCK)) // bps * bps, bps)
        rows = sel_ref[pl.ds(grp, bps), :]
        qa_ref[slot, HEAD_DIM:HEAD_DIM + 2 * bps, :] = (
            jnp.concatenate([rows, jnp.zeros_like(rows)], axis=0).astype(bf16))
        s = mm(ks_ref[0, 0, pl.ds(k0, SEL_KEYS), :], qa_ref[slot])
        s_ref[slot] = s
        cm_ref[slot] = jnp.max(s, axis=0, keepdims=True)

    def accumulate(kc, slot, diagonal):
        k0 = pl.multiple_of(kc * SEL_KEYS, SEL_KEYS)
        s, cm = s_ref[slot], cm_ref[slot]
        if diagonal:
            s = jnp.where(k0 + krow <= tpos, s, NEG)
            cm = jnp.max(s, axis=0, keepdims=True)
        m = m_ref[...]
        m_new = jnp.maximum(m, cm)
        alpha = jnp.exp(m - m_new)
        p = jnp.exp(s - m_new)
        l_ref[...] = alpha * l_ref[...] + jnp.sum(p, axis=0, keepdims=True)
        m_ref[...] = m_new
        acc_ref[...] = alpha * acc_ref[...] + mm(vsT_ref[0, 0, :, pl.ds(k0, SEL_KEYS)], p.astype(bf16))

    last = (t0 + Q - 1) // SEL_KEYS
    scores(0, 0)

    def sel_body(j, carry):
        scores(2 * j + 1, 1)
        accumulate(2 * j, 0, False)
        scores(2 * j + 2, 0)
        accumulate(2 * j + 1, 1, False)
        return carry

    lax.fori_loop(0, last // 2, sel_body, 0)

    @pl.when(last % 2 == 1)
    def _():
        scores(last, 1)
        accumulate(last - 1, 0, False)
        accumulate(last, 1, True)

    @pl.when(last % 2 == 0)
    def _():
        accumulate(last, 0, True)

    o_ref[0, 0, 0] = o_ref[0, 0, 0] + g[1:2, :] * (acc_ref[...] / l_ref[...])


def _nsa_attention(qT, kc, vcT, ks, vsT, kw, vwT, gT):
    B, G, NQB, _, RQ = qT.shape
    S = ks.shape[2]
    NC = kc.shape[2]
    NSB = S // SEL_BLOCK
    c = np.arange(NC)[None, :] * CMP_STRIDE
    j = np.arange(NSB)[:, None] * SEL_BLOCK
    ovT = jnp.asarray((c <= j + SEL_BLOCK - 1) & (c + CMP_BLOCK - 1 >= j), jnp.bfloat16)
    per_q = lambda rows: pl.BlockSpec((1, 1, 1, rows, RQ), lambda b, g, q: (b, g, q, 0, 0))
    per_g = lambda d0, d1: pl.BlockSpec((1, 1, d0, d1), lambda b, g, q: (b, g, 0, 0))
    return pl.pallas_call(
        _nsa_attn_kernel,
        grid=(B, G, NQB),
        in_specs=[per_q(HEAD_DIM), per_g(NC, HEAD_DIM), per_g(HEAD_DIM, NC), _full((NSB, NC)),
                  per_g(S, V7X_LANES), per_g(HEAD_DIM, S), per_g(S, HEAD_DIM), per_g(HEAD_DIM, S),
                  per_q(V7X_SUBLANES)],
        out_specs=per_q(HEAD_DIM),
        out_shape=jax.ShapeDtypeStruct((B, G, NQB, HEAD_DIM, RQ), jnp.float32),
        scratch_shapes=[pltpu.VMEM((NSB, Q_BLOCK), jnp.float32), pltpu.VMEM((NSB, Q_BLOCK), jnp.int32),
                        pltpu.VMEM((NSB, RQ), jnp.float32),
                        pltpu.VMEM((2, V7X_LANES, RQ), jnp.bfloat16),
                        pltpu.VMEM((2, SEL_KEYS, RQ), jnp.float32), pltpu.VMEM((2, 1, RQ), jnp.float32),
                        pltpu.VMEM((1, RQ), jnp.float32), pltpu.VMEM((1, RQ), jnp.float32),
                        pltpu.VMEM((HEAD_DIM, RQ), jnp.float32)],
        compiler_params=_params("parallel", "parallel", "arbitrary"),
        name="nsa_attention",
    )(qT, kc, vcT, ovT, ks, vsT, kw, vwT, gT)


def _nsa_branch(x2d, w_nsa, pe_k, pe_v, ck_w1, ck_w2, cv_w1, cv_w2, batch, seq_len):
    B, S, G, R, N, Q = batch, seq_len, NSA_GROUPS, NSA_R, HEAD_DIM, Q_BLOCK
    NQB = S // Q
    q, kc_in, vc_in, ks, vs, kw, vw, gates = _nsa_prep(x2d, w_nsa, S)
    per_group = lambda t: t.reshape(B, S, G, N).transpose(0, 2, 1, 3)
    per_group_t = lambda t: t.reshape(B, S, G, N).transpose(0, 2, 3, 1)
    blocks = lambda t: per_group(t).reshape(B, G, S // CMP_STRIDE, CMP_STRIDE * N)
    kc, vc = _nsa_compress(blocks(kc_in), blocks(vc_in), pe_k, pe_v, ck_w1, ck_w2, cv_w1, cv_w2)
    qT = (q.astype(jnp.bfloat16).reshape(B, NQB, Q, G, R, N).transpose(0, 3, 1, 5, 4, 2)
          .reshape(B, G, NQB, N, R * Q))
    g3 = gates[:, :3 * NSA_Q_HEADS].reshape(B, NQB, Q, G, R, 3).transpose(0, 3, 1, 5, 4, 2)
    gT = jnp.pad(g3.reshape(B, G, NQB, 3, R * Q), ((0, 0),) * 3 + ((0, V7X_SUBLANES - 3), (0, 0)))
    slot = (np.arange(S) // SEL_BLOCK) % SEL_SLOTS
    onehot = jnp.asarray(slot[:, None] == np.arange(V7X_LANES - N)[None, :], ks.dtype)
    ks_aug = jnp.concatenate([per_group(ks), jnp.broadcast_to(onehot, (B, G, S, V7X_LANES - N))], axis=-1)
    oT = _nsa_attention(qT, kc, vc.transpose(0, 1, 3, 2), ks_aug, per_group_t(vs),
                        per_group(kw), per_group_t(vw), gT)
    return oT.reshape(B, G, NQB, N, R, Q).transpose(0, 2, 5, 1, 4, 3).reshape(B * S, NSA_DIM)


def _merge_kernel(x_ref, ya_ref, yb_ref, wg_ref, pa_ref, pb_ref, wo_ref, g_ref, b_ref, o_ref):
    bf16 = jnp.bfloat16
    mm = lambda a, w: jnp.dot(a.astype(bf16), w, preferred_element_type=jnp.float32)
    x = x_ref[...]
    gates = jax.nn.sigmoid(mm(x, wg_ref[...]))
    mixed = gates[:, :D_MODEL] * mm(ya_ref[...], pa_ref[...]) + gates[:, D_MODEL:] * mm(yb_ref[...], pb_ref[...])
    o_ref[...] = _layer_norm(ALPHA * x + mm(mixed, wo_ref[...]), g_ref[...], b_ref[...])


def _merge(x2d, y_a, y_b, w_gate, p_a, p_b, w_o, ln_g, ln_b, tm=512):
    M = x2d.shape[0]
    bf16 = jnp.bfloat16
    rows = lambda w: pl.BlockSpec((tm, w), lambda i: (i, 0))
    return pl.pallas_call(
        _merge_kernel,
        grid=(M // tm,),
        in_specs=[rows(D_MODEL), rows(RWKV_DIM), rows(NSA_DIM), _full((D_MODEL, 2 * D_MODEL)),
                  _full((RWKV_DIM, D_MODEL)), _full((NSA_DIM, D_MODEL)), _full((D_MODEL, D_MODEL)),
                  _full((1, D_MODEL)), _full((1, D_MODEL))],
        out_specs=rows(D_MODEL),
        out_shape=jax.ShapeDtypeStruct((M, D_MODEL), jnp.float32),
        compiler_params=_params("parallel"),
        name="merge",
    )(x2d, y_a, y_b, w_gate, p_a.astype(bf16), p_b.astype(bf16), w_o.astype(bf16),
      ln_g.reshape(1, -1), ln_b.reshape(1, -1))


def _mem_kv_kernel(mem_ref, wk_ref, wv_ref, k_out, v_out):
    m = mem_ref[...].astype(jnp.bfloat16)
    k_out[...] = jnp.dot(m, wk_ref[...], preferred_element_type=jnp.float32).astype(k_out.dtype)
    v_out[...] = jnp.dot(m, wv_ref[...], preferred_element_type=jnp.float32).astype(v_out.dtype)


def _mem_kv(mem2d, wk, wv):
    M = mem2d.shape[0]
    bf16 = jnp.bfloat16
    out = jax.ShapeDtypeStruct((M, D_MODEL), bf16)
    return pl.pallas_call(
        _mem_kv_kernel,
        grid=(1,),
        in_specs=[_full((M, D_MODEL)), _full((D_MODEL, D_MODEL)), _full((D_MODEL, D_MODEL))],
        out_specs=[_full((M, D_MODEL))] * 2,
        out_shape=[out, out],
        compiler_params=_params("arbitrary"),
        name="mem_kv",
    )(mem2d, wk.astype(bf16), wv.astype(bf16))


def _xattn_kernel(x_ref, k_ref, v_ref, wq_ref, wo_ref, g_ref, b_ref, o_ref):
    bf16, f32 = jnp.bfloat16, jnp.float32
    x = x_ref[...]
    q = jnp.dot(x.astype(bf16), wq_ref[...], preferred_element_type=f32).astype(bf16)
    cols = [slice(h * X_HEAD_DIM, (h + 1) * X_HEAD_DIM) for h in range(X_HEADS)]
    scores = [lax.dot_general(q[:, sl], k_ref[:, sl], (((1,), (1,)), ((), ())),
                              preferred_element_type=f32) * (X_HEAD_DIM ** -0.5) for sl in cols]
    heads = []
    for s, sl in zip(scores, cols):
        p = jnp.exp(s - jnp.max(s, axis=-1, keepdims=True))
        p = p / jnp.sum(p, axis=-1, keepdims=True)
        heads.append(jnp.dot(p.astype(bf16), v_ref[:, sl], preferred_element_type=f32))
    o = jnp.concatenate(heads, axis=1).astype(bf16)
    xa = jnp.dot(o, wo_ref[...], preferred_element_type=f32)
    o_ref[...] = _layer_norm(ALPHA * x + xa, g_ref[...], b_ref[...])


def _xattn(x2d, k_mem, v_mem, wq, wo, ln_g, ln_b, seq_len, mem_len, tm=512):
    M = x2d.shape[0]
    bf16 = jnp.bfloat16
    seq_tiles = seq_len // tm
    rows = pl.BlockSpec((tm, D_MODEL), lambda i: (i, 0))
    mem_spec = pl.BlockSpec((mem_len, D_MODEL), lambda i: (i // seq_tiles, 0))
    return pl.pallas_call(
        _xattn_kernel,
        grid=(M // tm,),
        in_specs=[rows, mem_spec, mem_spec, _full((D_MODEL, D_MODEL)), _full((D_MODEL, D_MODEL)),
                  _full((1, D_MODEL)), _full((1, D_MODEL))],
        out_specs=rows,
        out_shape=jax.ShapeDtypeStruct((M, D_MODEL), jnp.float32),
        compiler_params=_params("parallel"),
        name="xattn",
    )(x2d, k_mem, v_mem, wq.astype(bf16), wo.astype(bf16), ln_g.reshape(1, -1), ln_b.reshape(1, -1))


FFN_CHUNK = 1408


def _ffn_kernel(seq_tiles, x_ref, xp_ref, wup_ref, cw_ref, cb_ref, wdn_ref, g_ref, b_ref, o_ref):
    bf16, f32 = jnp.bfloat16, jnp.float32
    i = pl.program_id(0)
    tm = x_ref.shape[0]
    H = V7X_SUBLANES
    x = x_ref[...]
    xprev = jnp.where(i % seq_tiles == 0, 0.0, xp_ref[...])
    xe = jnp.concatenate([xprev, x], axis=0).astype(bf16)

    def conv(cols):
        h = jnp.dot(xe, wup_ref[:, cols], preferred_element_type=f32)
        w = cw_ref[:, cols]
        return (h[H - 2:H - 2 + tm] * w[0:1] + h[H - 1:H - 1 + tm] * w[1:2] + h[H:] * w[2:3]
                + cb_ref[:, cols])

    acc = jnp.zeros((tm, D_MODEL), f32)
    for c in range(D_FF // FFN_CHUNK):
        gate = conv(slice(c * FFN_CHUNK, (c + 1) * FFN_CHUNK))
        val = conv(slice(D_FF + c * FFN_CHUNK, D_FF + (c + 1) * FFN_CHUNK))
        act = (gate * jax.nn.sigmoid(gate) * val).astype(bf16)
        acc = acc + jnp.dot(act, wdn_ref[c * FFN_CHUNK:(c + 1) * FFN_CHUNK, :], preferred_element_type=f32)
    o_ref[...] = _layer_norm(ALPHA * x + acc, g_ref[...], b_ref[...])


def _ffn(x2d, w_up, conv_w, conv_b, w_down, ln_g, ln_b, seq_len, tm=512):
    M = x2d.shape[0]
    bf16 = jnp.bfloat16
    seq_tiles = seq_len // tm
    blocks_per_tile = tm // V7X_SUBLANES
    rows = pl.BlockSpec((tm, D_MODEL), lambda i: (i, 0))
    once = lambda shape: pl.BlockSpec(shape, lambda i: (0,) * len(shape), pipeline_mode=pl.Buffered(1))
    return pl.pallas_call(
        functools.partial(_ffn_kernel, seq_tiles),
        grid=(M // tm,),
        in_specs=[rows,
                  pl.BlockSpec((V7X_SUBLANES, D_MODEL), lambda i: (jnp.maximum(i * blocks_per_tile - 1, 0), 0)),
                  once((D_MODEL, 2 * D_FF)), _full((3, 2 * D_FF)), _full((1, 2 * D_FF)),
                  once((D_FF, D_MODEL)), _full((1, D_MODEL)), _full((1, D_MODEL))],
        out_specs=rows,
        out_shape=jax.ShapeDtypeStruct((M, D_MODEL), jnp.float32),
        compiler_params=_params("parallel"),
        name="ffn",
    )(x2d, x2d, w_up.astype(bf16), conv_w, conv_b.reshape(1, -1), w_down.astype(bf16),
      ln_g.reshape(1, -1), ln_b.reshape(1, -1))


def kernel(x, mem, w_in, rwkv_mu, rwkv_w0, rwkv_w2, rwkv_a0, rwkv_a2, rwkv_k_k, rwkv_k_a, rwkv_r_k, rwkv_gn_g, rwkv_gn_b, nsa_pe_k, nsa_pe_v, nsa_ck_w1, nsa_ck_w2, nsa_cv_w1, nsa_cv_w2, merge_p_a, merge_p_b, mix_w_o, ln1_g, ln1_b, xa_wq, xa_wk, xa_wv, xa_wo, ln2_g, ln2_b, ffn_w_up, ffn_conv_w, ffn_conv_b, ffn_w_down, ln3_g, ln3_b):
    B, S, _ = x.shape
    mem_len = mem.shape[1]
    bf16 = jnp.bfloat16
    x2d = x.reshape(B * S, D_MODEL)
    for l in range(DEPTH):
        w = w_in[l]
        w_rwkv = w[:, :RWKV_COLS].astype(bf16)
        w_nsa = jnp.pad(w[:, RWKV_COLS:RWKV_COLS + NSA_COLS], ((0, 0), (0, NSA_COLS_PAD - NSA_COLS))).astype(bf16)
        w_gate = w[:, RWKV_COLS + NSA_COLS:].astype(bf16)
        y_a = _rwkv_time_mix(x2d, w_rwkv, rwkv_mu[l], rwkv_w0[l], rwkv_w2[l], rwkv_a0[l], rwkv_a2[l],
                             rwkv_k_k[l], rwkv_k_a[l], rwkv_r_k[l], rwkv_gn_g[l], rwkv_gn_b[l], B, S)
        y_b = _nsa_branch(x2d, w_nsa, nsa_pe_k[l], nsa_pe_v[l], nsa_ck_w1[l], nsa_ck_w2[l],
                          nsa_cv_w1[l], nsa_cv_w2[l], B, S)
        x2d = _merge(x2d, y_a, y_b, w_gate, merge_p_a[l], merge_p_b[l], mix_w_o[l], ln1_g[l], ln1_b[l])
        k_mem, v_mem = _mem_kv(mem.reshape(B * mem_len, D_MODEL), xa_wk[l], xa_wv[l])
        x2d = _xattn(x2d, k_mem, v_mem, xa_wq[l], xa_wo[l], ln2_g[l], ln2_b[l], S, mem_len)
        x2d = _ffn(x2d, ffn_w_up[l], ffn_conv_w[l], ffn_conv_b[l], ffn_w_down[l], ln3_g[l], ln3_b[l], S)
    return x2d.reshape(B, S, D_MODEL)
```

```python
import functools
import math

import jax
import jax.numpy as jnp
import numpy as np
from jax import lax
from jax.experimental import pallas as pl
from jax.experimental.pallas import tpu as pltpu

D_MODEL = 1024
HEAD_DIM = 64
RWKV_DIM = 512
RWKV_HEADS = 8
LORA = 64
RWKV_COLS = 3 * RWKV_DIM + 2 * LORA
GN_EPS = 64e-5
NSA_DIM = 512
NSA_Q_HEADS = 8
NSA_GROUPS = 2
NSA_R = NSA_Q_HEADS // NSA_GROUPS
NSA_KV_DIM = NSA_GROUPS * HEAD_DIM
NSA_COLS = NSA_DIM + 6 * NSA_KV_DIM + 3 * NSA_Q_HEADS
NSA_COLS_PAD = 1408
CMP_BLOCK = 32
CMP_STRIDE = 16
CMP_HIDDEN = 256
SEL_BLOCK = 64
N_SELECT = 16
WINDOW = 512
Q_BLOCK = 128
ROPE_THETA = 500000.0
ROPE_DIM = 16
X_HEADS = 4
X_HEAD_DIM = 256
D_FF = 2816
LN_EPS = 1e-5
DEPTH = 1
ALPHA = (2 * DEPTH) ** 0.25
NEG = -1e30
BIG = 1e30
LOG2E = math.log2(math.e)

V7X_LANES = 128
V7X_SUBLANES = 8
V7X_VMEM_LIMIT_BYTES = 56 * 1024 * 1024

HI = lax.Precision.HIGHEST
RWKV_CHUNK = 64


def _params(*sem):
    return pltpu.CompilerParams(dimension_semantics=sem, vmem_limit_bytes=V7X_VMEM_LIMIT_BYTES)


def _full(shape):
    n = len(shape)
    return pl.BlockSpec(shape, lambda *_: (0,) * n)


def _head_ones(width):
    r = lax.broadcasted_iota(jnp.int32, (width, width), 0) // HEAD_DIM
    c = lax.broadcasted_iota(jnp.int32, (width, width), 1) // HEAD_DIM
    return (r == c).astype(jnp.float32)


def _layer_norm(y, g, b):
    mu = jnp.mean(y, axis=-1, keepdims=True)
    d = y - mu
    var = jnp.mean(d * d, axis=-1, keepdims=True)
    return d * lax.rsqrt(var + LN_EPS) * g + b


def _rwkv_prep_kernel(seq_tiles, x_ref, xp_ref, w_ref, mu_ref, w0_ref, w2_ref, a0_ref, a2_ref,
                      kk_ref, ka_ref, rk_ref,
                      r_out, k_out, v_out, lw_out, a_out, b_out, bonus_out):
    i = pl.program_id(0)
    tm = x_ref.shape[0]
    C = RWKV_DIM
    w = w_ref[...]
    p = jnp.dot(x_ref[...].astype(jnp.bfloat16), w, preferred_element_type=jnp.float32)
    xprev = xp_ref[...].astype(jnp.bfloat16)
    pprev = jnp.dot(xprev, w, preferred_element_type=jnp.float32)[V7X_SUBLANES - 1:V7X_SUBLANES, :]
    pprev = jnp.where(i % seq_tiles == 0, 0.0, pprev)
    row = lax.broadcasted_iota(jnp.int32, (tm, 1), 0)
    shifted = jnp.where(row == 0, pprev, pltpu.roll(p, 1, 0))
    p = p + (shifted - p) * mu_ref[...]
    r, k, v = p[:, :C], p[:, C:2 * C], p[:, 2 * C:3 * C]
    wl = p[:, 3 * C:3 * C + LORA]
    al = p[:, 3 * C + LORA:]
    z = -(w0_ref[...] + jnp.dot(jnp.tanh(wl).astype(jnp.bfloat16), w2_ref[...],
                                preferred_element_type=jnp.float32))
    softplus = jnp.maximum(z, 0.0) + jnp.log(1.0 + jnp.exp(-jnp.abs(z)))
    w_log = -softplus - 0.5
    lw_out[...] = -jnp.exp(w_log)
    a = jax.nn.sigmoid(a0_ref[...] + jnp.dot(al.astype(jnp.bfloat16), a2_ref[...],
                                             preferred_element_type=jnp.float32))
    ones = _head_ones(C).astype(jnp.bfloat16)
    ones2 = jnp.concatenate([ones, ones], axis=0)

    def head_sum(t):
        return jnp.dot(jnp.concatenate(_split(t), axis=1), ones2, preferred_element_type=jnp.float32)

    kk = k * kk_ref[...]
    kk = kk / jnp.maximum(jnp.sqrt(head_sum(kk * kk)), 1e-12)
    kmod = k * (1.0 + (a - 1.0) * ka_ref[...])
    bonus = head_sum(r * kmod * rk_ref[...]) * v
    r_out[...] = r
    k_out[...] = kmod
    v_out[...] = v
    a_out[...] = -kk
    b_out[...] = kk * a
    bonus_out[...] = bonus


def _rwkv_prep(x2d, w_rwkv, mu, w0, w2, a0, a2, k_k, k_a, r_k, seq_len, tm=512):
    M = x2d.shape[0]
    C = RWKV_DIM
    seq_tiles = seq_len // tm
    row = lambda a: a.reshape(1, -1)
    out = jax.ShapeDtypeStruct((M, C), jnp.float32)
    tile = pl.BlockSpec((tm, C), lambda i: (i, 0))
    blocks_per_tile = tm // V7X_SUBLANES
    return pl.pallas_call(
        functools.partial(_rwkv_prep_kernel, seq_tiles),
        grid=(M // tm,),
        in_specs=[
            pl.BlockSpec((tm, D_MODEL), lambda i: (i, 0)),
            pl.BlockSpec((V7X_SUBLANES, D_MODEL), lambda i: (jnp.maximum(i * blocks_per_tile - 1, 0), 0)),
            _full((D_MODEL, RWKV_COLS)), _full((1, RWKV_COLS)), _full((1, C)), _full((LORA, C)),
            _full((1, C)), _full((LORA, C)), _full((1, C)), _full((1, C)), _full((1, C)),
        ],
        out_specs=[tile] * 7,
        out_shape=[out] * 7,
        compiler_params=_params("parallel"),
        name="rwkv_prep",
    )(x2d, x2d, w_rwkv, row(mu), row(w0), w2.astype(jnp.bfloat16), row(a0), a2.astype(jnp.bfloat16),
      row(k_k), row(k_a), row(r_k))


RWKV_GROUP = 4


def _split(x):
    hi = x.astype(jnp.bfloat16)
    return hi, (x - hi.astype(jnp.float32)).astype(jnp.bfloat16)


def _mm(a, b):
    return jnp.dot(a.astype(jnp.bfloat16), b.astype(jnp.bfloat16), preferred_element_type=jnp.float32)


def _rwkv_mix_kernel(r_ref, k_ref, v_ref, lw_ref, a_ref, b_ref, bonus_ref, gng_ref, gnb_ref,
                     y_out, state):
    T, N, GH = RWKV_CHUNK, HEAD_DIM, RWKV_GROUP
    W = GH * N
    f32 = jnp.float32

    @pl.when(pl.program_id(1) == 0)
    def _():
        state[...] = jnp.zeros_like(state)

    ri = lax.broadcasted_iota(jnp.int32, (W, W), 0)
    ci = lax.broadcasted_iota(jnp.int32, (W, W), 1)
    same_head = ri // N == ci // N
    strict, incl = ri > ci, ri >= ci
    eye = (ri == ci).astype(f32)
    tri = (lax.broadcasted_iota(jnp.int32, (T, T), 0)
           >= lax.broadcasted_iota(jnp.int32, (T, T), 1)).astype(jnp.bfloat16)
    mean_w = (_head_ones(RWKV_DIM) * (1.0 / N)).astype(jnp.bfloat16)

    def expand(x):
        return jnp.where(same_head, jnp.concatenate([x] * GH, axis=0), jnp.zeros((), x.dtype))

    def collapse(x):
        out = x[:T]
        for h in range(1, GH):
            out = out + x[h * T:(h + 1) * T]
        return out

    bf = lambda t: t.astype(jnp.bfloat16)
    n_chunks = r_ref.shape[0] // T
    n_groups = RWKV_HEADS // GH
    chains = [(ch, g) for ch in range(n_chunks) for g in range(n_groups)]
    each = lambda fn, *lists: [fn(*args) for args in zip(*lists)]

    def cumsum(x):
        hi, lo = _split(x)
        lo2 = (x - hi.astype(f32) - lo.astype(f32)).astype(jnp.bfloat16)
        return jnp.dot(jnp.concatenate([tri] * 3, axis=1), jnp.concatenate([hi, lo, lo2], axis=0),
                       preferred_element_type=f32)

    c_all = [cumsum(lw_ref[ch * T:(ch + 1) * T, :]) for ch in range(n_chunks)]

    def operands(ch, g):
        rows, cols = slice(ch * T, (ch + 1) * T), slice(g * W, (g + 1) * W)
        lw, c = lw_ref[rows, cols], c_all[ch][:, cols]
        r, k, v = r_ref[rows, cols], k_ref[rows, cols], v_ref[rows, cols]
        a, b = a_ref[rows, cols], b_ref[rows, cols]
        c_last = c[T - 1:T, :]
        e_neg, e_end = jnp.exp(-c), jnp.exp(c_last - c)
        r_t = r * jnp.exp(c)
        return dict(a_x=expand(bf(a * jnp.exp(c - lw))), r_x=expand(bf(r_t)), b_x=expand(bf(b * e_neg)),
                    k_x=expand(bf(k * e_neg)), v_x=expand(bf(v)), r_t=r_t, b_end=b * e_end, k_end=k * e_end,
                    w_end=jnp.exp(c_last))

    ops = [operands(ch, g) for ch, g in chains]
    P = [lax.dot_general(jnp.concatenate([o["a_x"], o["r_x"]], axis=0),
                         jnp.concatenate([o["b_x"], o["k_x"]], axis=0),
                         (((1,), (1,)), ((), ())), preferred_element_type=f32) for o in ops]
    L_ab = [jnp.where(strict, p[:W, :W], 0.0) for p in P]
    L_ak = [bf(jnp.where(strict, p[:W, W:], 0.0)) for p in P]
    M_rb = [bf(jnp.where(incl, p[W:, :W], 0.0)) for p in P]
    M_rk = [bf(jnp.where(incl, p[W:, W:], 0.0)) for p in P]
    v_x = [o["v_x"] for o in ops]

    base = V7X_SUBLANES
    D = [bf(jnp.where(ri // base == ci // base, l, 0.0)) for l in L_ab]
    D2 = each(_mm, D, D)
    D4 = each(_mm, D2, D2)
    X = each(_mm, each(lambda d, d2: _mm(eye + d.astype(f32), eye + d2), D, D2), [eye + d4 for d4 in D4])
    blk = base
    while blk < T:
        pair = (ri // (2 * blk) == ci // (2 * blk)) & (ri // blk != ci // blk)
        Xb = [bf(x) for x in X]
        step = each(_mm, each(_mm, Xb, [jnp.where(pair, l, 0.0) for l in L_ab]), Xb)
        X = [x + s for x, s in zip(X, step)]
        blk *= 2

    lakv = each(_mm, L_ak, v_x)
    AV = [bf(_mm(x, jnp.concatenate([o["a_x"], bf(t)], axis=1))) for x, o, t in zip(X, ops, lakv)]
    ry = each(_mm, M_rb, AV)
    mrkv = each(_mm, M_rk, v_x)
    gh = [_mm(expand(o["b_end"]).T, av) for o, av in zip(ops, AV)]
    kv = [_mm(expand(o["k_end"]).T, vx) for o, vx in zip(ops, v_x)]
    rp = [bf(expand(o["r_t"]) + t[:, :W]) for o, t in zip(ops, ry)]
    yp = [t[:, W:] + u for t, u in zip(ry, mrkv)]
    G = [bf(eye * o["w_end"] + t[:, :W]) for o, t in zip(ops, gh)]
    H = [t[:, W:] + u for t, u in zip(gh, kv)]

    y_rows = []
    for ch in range(n_chunks):
        ys = []
        for g in range(n_groups):
            i = chains.index((ch, g))
            s0 = bf(state[g])
            ys.append(collapse(_mm(rp[i], s0) + yp[i]))
            state[g] = _mm(G[i], s0) + H[i]
        y_rows.append(jnp.concatenate(ys, axis=1))
    y = jnp.concatenate(y_rows, axis=0)
    y_hi, y_lo = _split(y)
    ym = jnp.dot(jnp.concatenate([y_hi, y_lo], axis=1), jnp.concatenate([mean_w, mean_w], axis=0),
                 preferred_element_type=f32)
    d = y - ym
    yv = _mm(d * d, mean_w)
    y_out[...] = d * lax.rsqrt(yv + GN_EPS) * gng_ref[...] + gnb_ref[...] + bonus_ref[...]


def _rwkv_mix(r, k, v, lw, a, b, bonus, gn_g, gn_b, batch, chunks_per_step=4):
    M, C = r.shape
    rows = RWKV_CHUNK * chunks_per_step
    steps = M // rows // batch
    W = RWKV_GROUP * HEAD_DIM
    tile = pl.BlockSpec((rows, C), lambda bi, ci: (bi * steps + ci, 0))
    return pl.pallas_call(
        _rwkv_mix_kernel,
        grid=(batch, steps),
        in_specs=[tile] * 7 + [_full((1, C)), _full((1, C))],
        out_specs=tile,
        out_shape=jax.ShapeDtypeStruct((M, C), jnp.float32),
        scratch_shapes=[pltpu.VMEM((RWKV_HEADS // RWKV_GROUP, W, W), jnp.float32)],
        compiler_params=_params("parallel", "arbitrary"),
        name="rwkv_mix",
    )(r, k, v, lw, a, b, bonus, gn_g.reshape(1, C), gn_b.reshape(1, C))


def _rwkv_time_mix(x2d, w_rwkv, mu, w0, w2, a0, a2, k_k, k_a, r_k, gn_g, gn_b, batch, seq_len):
    r, k, v, lw, a, b, bonus = _rwkv_prep(x2d, w_rwkv, mu, w0, w2, a0, a2, k_k, k_a, r_k, seq_len)
    return _rwkv_mix(r, k, v, lw, a, b, bonus, gn_g, gn_b, batch)


def _rope_tables(seq_len):
    half = ROPE_DIM // 2
    inv = ROPE_THETA ** (-np.arange(half, dtype=np.float64) * 2.0 / ROPE_DIM)
    ang = np.arange(seq_len, dtype=np.float64)[:, None] * inv[None, :]
    cos, sin = np.cos(ang).astype(np.float32), np.sin(ang).astype(np.float32)
    pad = np.zeros((seq_len, HEAD_DIM - ROPE_DIM), np.float32)
    zero = np.zeros_like(sin)
    c = np.concatenate([cos, cos, pad + 1.0], axis=1)
    s_lo = np.concatenate([-sin, zero, pad], axis=1)
    s_hi = np.concatenate([zero, sin, pad], axis=1)
    two = lambda t: jnp.asarray(np.concatenate([t, t], axis=1))
    return two(c), two(s_lo), two(s_hi)


def _rope_pair(x, c, s_lo, s_hi):
    return x * c + pltpu.roll(x, V7X_LANES - ROPE_DIM // 2, 1) * s_lo + pltpu.roll(x, ROPE_DIM // 2, 1) * s_hi


GATE_SLOTS = V7X_SUBLANES


def _nsa_prep_kernel(x_ref, w_ref, c_ref, slo_ref, shi_ref,
                     qT_out, kc_out, vc_out, ks_out, vsT_out, kw_out, vwT_out, gT_out):
    f32 = jnp.float32
    L, N, G, R, Q = V7X_LANES, HEAD_DIM, NSA_GROUPS, NSA_R, Q_BLOCK
    tm = x_ref.shape[0]
    p = jnp.dot(x_ref[...].astype(jnp.bfloat16), w_ref[...], preferred_element_type=f32)
    c, s_lo, s_hi = c_ref[...], slo_ref[...], shi_ref[...]
    rope = lambda t: _rope_pair(t, c, s_lo, s_hi)
    kv = lambda i: p[:, NSA_DIM + i * L:NSA_DIM + (i + 1) * L]

    qT = [(rope(p[:, j * L:(j + 1) * L]) * (N ** -0.5 * LOG2E)).T for j in range(NSA_DIM // L)]
    gT = jax.nn.sigmoid(kv(6)).T
    for g in range(G):
        for qb in range(tm // Q):
            blk = slice(qb * Q, (qb + 1) * Q)
            heads = [qT[(g * R + r) // 2][((g * R + r) % 2) * N:((g * R + r) % 2 + 1) * N, blk]
                     for r in range(R)]
            qT_out[0, g, qb] = jnp.concatenate(heads, axis=1).astype(qT_out.dtype)
            gates = [gT[(g * R + r) * GATE_SLOTS:(g * R + r + 1) * GATE_SLOTS, blk] for r in range(R)]
            gT_out[0, g, qb] = jnp.concatenate(gates, axis=1)

    k_c, k_s, k_w = rope(kv(0)), rope(kv(2)), rope(kv(4))
    v_c = kv(1)
    v_sT, v_wT = kv(3).T, kv(5).T
    slot = (lax.broadcasted_iota(jnp.int32, (tm, L - N), 0) // SEL_BLOCK) % SEL_SLOTS
    onehot = (slot == lax.broadcasted_iota(jnp.int32, (tm, L - N), 1)).astype(f32)
    for g in range(G):
        cols = slice(g * N, (g + 1) * N)
        kc_out[0, g] = k_c[:, cols]
        vc_out[0, g] = v_c[:, cols]
        ks_out[0, g] = jnp.concatenate([k_s[:, cols], onehot], axis=1).astype(ks_out.dtype)
        kw_out[0, g] = k_w[:, cols].astype(kw_out.dtype)
        vsT_out[0, g] = v_sT[cols, :].astype(vsT_out.dtype)
        vwT_out[0, g] = v_wT[cols, :].astype(vwT_out.dtype)


def _nsa_weight(w):
    n_qkv = NSA_DIM + 6 * NSA_KV_DIM
    gates = w[:, n_qkv:].reshape(-1, NSA_Q_HEADS, 3)
    gates = jnp.pad(gates, ((0, 0), (0, 0), (0, GATE_SLOTS - 3))).reshape(-1, NSA_Q_HEADS * GATE_SLOTS)
    gates = jnp.pad(gates, ((0, 0), (0, NSA_COLS_PAD - n_qkv - NSA_Q_HEADS * GATE_SLOTS)))
    return jnp.concatenate([w[:, :n_qkv], gates], axis=1).astype(jnp.bfloat16)


def _nsa_prep(x2d, w_nsa, batch, seq_len, tm=512):
    B, S, G, N, L, Q = batch, seq_len, NSA_GROUPS, HEAD_DIM, V7X_LANES, Q_BLOCK
    RQ = NSA_R * Q
    tiles = S // tm
    tabs = _rope_tables(S)
    tab_spec = pl.BlockSpec((tm, L), lambda i: (i % tiles, 0))
    f32, bf16 = jnp.float32, jnp.bfloat16
    per_q = lambda rows: pl.BlockSpec((1, G, tm // Q, rows, RQ), lambda i: (i // tiles, 0, i % tiles, 0, 0))
    by_row = lambda w: pl.BlockSpec((1, G, tm, w), lambda i: (i // tiles, 0, i % tiles, 0))
    by_col = pl.BlockSpec((1, G, N, tm), lambda i: (i // tiles, 0, 0, i % tiles))
    sds = jax.ShapeDtypeStruct
    return pl.pallas_call(
        _nsa_prep_kernel,
        grid=(B * tiles,),
        in_specs=[pl.BlockSpec((tm, D_MODEL), lambda i: (i, 0)), _full((D_MODEL, NSA_COLS_PAD)),
                  tab_spec, tab_spec, tab_spec],
        out_specs=[per_q(N), by_row(N), by_row(N), by_row(L), by_col, by_row(N), by_col, per_q(GATE_SLOTS)],
        out_shape=[sds((B, G, S // Q, N, RQ), bf16), sds((B, G, S, N), f32), sds((B, G, S, N), f32),
                   sds((B, G, S, L), bf16), sds((B, G, N, S), bf16), sds((B, G, S, N), bf16),
                   sds((B, G, N, S), bf16), sds((B, G, S // Q, GATE_SLOTS, RQ), f32)],
        compiler_params=_params("parallel"),
        name="nsa_prep",
    )(x2d, w_nsa, *tabs)


def _gelu_tanh(x):
    return 0.5 * x * (1.0 + jnp.tanh(math.sqrt(2.0 / math.pi) * (x + 0.044715 * x * x * x)))


def _nsa_compress_kernel(xk_ref, xv_ref, pek_ref, pev_ref, kw1_ref, kw2_ref, vw1_ref, vw2_ref,
                         kc_out, vcT_out):
    N, bf16, f32 = HEAD_DIM, jnp.bfloat16, jnp.float32
    n_blocks = kc_out.shape[2]

    def mlp(x_ref, pe_ref, w1_ref, w2_ref):
        lo = jnp.zeros((n_blocks, CMP_HIDDEN), f32)
        hi = jnp.zeros((n_blocks, CMP_HIDDEN), f32)
        for l in range(CMP_STRIDE):
            rows = x_ref[0, 0, pl.ds(l, n_blocks, stride=CMP_STRIDE), :]
            lo = lo + jnp.dot((rows + pe_ref[l:l + 1, :]).astype(bf16), w1_ref[l * N:(l + 1) * N, :],
                              preferred_element_type=f32)
            m = CMP_STRIDE + l
            hi = hi + jnp.dot((rows + pe_ref[m:m + 1, :]).astype(bf16), w1_ref[m * N:(m + 1) * N, :],
                              preferred_element_type=f32)
        pre = lo + pltpu.roll(hi, n_blocks - 1, 0)
        return jnp.dot(_gelu_tanh(pre).astype(bf16), w2_ref[...], preferred_element_type=f32)

    kc_out[0, 0] = mlp(xk_ref, pek_ref, kw1_ref, kw2_ref).astype(kc_out.dtype)
    vc = mlp(xv_ref, pev_ref, vw1_ref, vw2_ref)
    vcT_out[0, 0] = jnp.concatenate([vc, jnp.zeros_like(vc)], axis=1).T[:N].astype(vcT_out.dtype)


def _nsa_compress(xk, xv, pe_k, pe_v, ck_w1, ck_w2, cv_w1, cv_w2):
    B, G, S, N = xk.shape
    NC = S // CMP_STRIDE
    bf16 = jnp.bfloat16
    xin = pl.BlockSpec((1, 1, S, N), lambda b, g: (b, g, 0, 0))
    return pl.pallas_call(
        _nsa_compress_kernel,
        grid=(B, G),
        in_specs=[xin, xin, _full((CMP_BLOCK, N)), _full((CMP_BLOCK, N)), _full((CMP_BLOCK * N, CMP_HIDDEN)),
                  _full((CMP_HIDDEN, N)), _full((CMP_BLOCK * N, CMP_HIDDEN)), _full((CMP_HIDDEN, N))],
        out_specs=[pl.BlockSpec((1, 1, NC, N), lambda b, g: (b, g, 0, 0)),
                   pl.BlockSpec((1, 1, N, NC), lambda b, g: (b, g, 0, 0))],
        out_shape=[jax.ShapeDtypeStruct((B, G, NC, N), bf16), jax.ShapeDtypeStruct((B, G, N, NC), bf16)],
        compiler_params=_params("parallel", "parallel"),
        name="nsa_compress",
    )(xk, xv, pe_k, pe_v, ck_w1.astype(bf16), ck_w2.astype(bf16), cv_w1.astype(bf16), cv_w2.astype(bf16))


SEL_KEYS = 512
SEL_SLOTS = V7X_SUBLANES


def _nsa_attn_kernel(qT_ref, kc_ref, vcT_ref, ovT_ref, ks_ref, vsT_ref, kw_ref, vwT_ref, gT_ref,
                     o_ref, val_ref, cnt_ref, sel_ref, qa_ref, s_ref, cm_ref, m_ref, l_ref, acc_ref, part_ref):
    qb = pl.program_id(2)
    t0 = qb * Q_BLOCK
    R, Q = NSA_R, Q_BLOCK
    RQ = R * Q
    f32, bf16 = jnp.float32, jnp.bfloat16
    qT = qT_ref[0, 0, 0]
    lane = lax.broadcasted_iota(jnp.int32, (1, RQ), 1)
    tpos = t0 + lane % Q
    mm = lambda a, b: jnp.dot(a, b, preferred_element_type=f32)

    NC = kc_ref.shape[2]
    span = WINDOW + Q
    w0 = pl.multiple_of(jnp.maximum(t0 - WINDOW, 0), Q)
    s_cmp = mm(kc_ref[0, 0], qT)
    s_win = mm(kw_ref[0, 0, pl.ds(w0, span), :], qT)

    cend = lax.broadcasted_iota(jnp.int32, (NC, 1), 0) * CMP_STRIDE + (CMP_BLOCK - 1)
    mask = cend <= tpos
    sm = jnp.where(mask, s_cmp, NEG)
    e = jnp.where(mask, jnp.exp2(sm - jnp.max(sm, axis=0, keepdims=True)), 0.0)
    den = jnp.sum(e, axis=0, keepdims=True)
    p = e / jnp.where(den > 0.0, den, 1.0)
    o_c = mm(vcT_ref[0, 0], p.astype(bf16))
    psum = p[:, :Q]
    for r in range(1, R):
        psum = psum + p[:, r * Q:(r + 1) * Q]
    ps_hi, ps_lo = _split(psum)
    ov = ovT_ref[...]
    imp = mm(jnp.concatenate([ov, ov], axis=1), jnp.concatenate([ps_hi, ps_lo], axis=0))

    diff = tpos - (w0 + lax.broadcasted_iota(jnp.int32, (span, 1), 0))
    sm = jnp.where((diff >= 0) & (diff < WINDOW), s_win, NEG)
    p = jnp.exp2(sm - jnp.max(sm, axis=0, keepdims=True))
    den = jnp.sum(p, axis=0, keepdims=True)
    o_w =mm(vwT_ref[0, 0, :, pl.ds(w0, span)], p.astype(bf16)) / den

    g = gT_ref[0, 0, 0]
    part_ref[...] = g[0:1, :] * o_c + g[2:3, :] * o_w

    NSB = ovT_ref.shape[0]
    tq = t0 + lax.broadcasted_iota(jnp.int32, (1, Q), 1)
    jblk = lax.broadcasted_iota(jnp.int32, (NSB, 1), 0)
    cur = tq // SEL_BLOCK
    forced = (jblk == 0) | (jblk == cur) | (jblk == cur - 1)
    valid = jblk * SEL_BLOCK <= tq
    val = jnp.where(valid, jnp.where(forced, BIG, imp), NEG)
    val_ref[...] = val
    n_live = (t0 + Q - 1) // SEL_BLOCK + 1

    def strict_body(u, cnt):
        for i in (2 * u, 2 * u + 1):
            cnt = cnt + jnp.where(val_ref[pl.ds(i, 1), :] > val, 1, 0)
        return cnt

    cnt_ref[...] = lax.fori_loop(0, n_live // 2, strict_body, jnp.zeros((NSB, Q), jnp.int32))
    taken = jnp.sum(jnp.where(valid & (cnt_ref[...] < N_SELECT), 1, 0), axis=0, keepdims=True)

    @pl.when(jnp.max(taken) > N_SELECT)
    def _():
        def tie_body(i, cnt):
            row = val_ref[pl.ds(i, 1), :]
            ge = jnp.where(row >= val, 1, 0)
            gt = jnp.where(row > val, 1, 0)
            return cnt + jnp.where(jblk > i, ge, gt)

        cnt_ref[...] = lax.fori_loop(0, n_live, tie_body, jnp.zeros((NSB, Q), jnp.int32))

    bias = jnp.where(cnt_ref[...] < N_SELECT, 0.0, NEG)
    sel_ref[...] = jnp.concatenate([bias] * R, axis=1)

    bps = SEL_SLOTS
    for buf in range(qa_ref.shape[0]):
        qa_ref[buf, :HEAD_DIM, :] = qT
        qa_ref[buf, HEAD_DIM:, :] = jnp.zeros((qa_ref.shape[1] - HEAD_DIM, RQ), bf16)
    m_ref[...] = jnp.full(m_ref.shape, NEG, f32)
    l_ref[...] = jnp.zeros(l_ref.shape, f32)
    acc_ref[...] = jnp.zeros(acc_ref.shape, f32)
    krow = lax.broadcasted_iota(jnp.int32, (SEL_KEYS, 1), 0)

    def scores(kc, slot):
        k0 = pl.multiple_of(kc * SEL_KEYS, SEL_KEYS)
        grp = pl.multiple_of((kc * (SEL_KEYS // SEL_BLOCK)) // bps * bps, bps)
        rows = sel_ref[pl.ds(grp, bps), :]
        qa_ref[slot, HEAD_DIM:HEAD_DIM + 2 * bps, :] = (
            jnp.concatenate([rows, jnp.zeros_like(rows)], axis=0).astype(bf16))
        s = mm(ks_ref[0, 0, pl.ds(k0, SEL_KEYS), :], qa_ref[slot])
        s_ref[slot] = s
        cm_ref[slot] = jnp.max(s, axis=0, keepdims=True)

    def accumulate(kc, slot, diagonal):
        k0 = pl.multiple_of(kc * SEL_KEYS, SEL_KEYS)
        s, cm = s_ref[slot], cm_ref[slot]
        if diagonal:
            s = jnp.where(k0 + krow <= tpos, s, NEG)
            cm = jnp.max(s, axis=0, keepdims=True)
        m = m_ref[...]
        m_new = jnp.maximum(m, cm)
        alpha = jnp.exp2(m - m_new)
        p = jnp.exp2(s - m_new)
        l_ref[...] = alpha * l_ref[...] + jnp.sum(p, axis=0, keepdims=True)
        m_ref[...] = m_new
        acc_ref[...] = alpha * acc_ref[...] + mm(vsT_ref[0, 0, :, pl.ds(k0, SEL_KEYS)], p.astype(bf16))

    last = (t0 + Q - 1) // SEL_KEYS
    scores(0, 0)

    def sel_body(j, carry):
        scores(2 * j + 1, 1)
        accumulate(2 * j, 0, False)
        scores(2 * j + 2, 0)
        accumulate(2 * j + 1, 1, False)
        return carry

    lax.fori_loop(0, last // 2, sel_body, 0)

    @pl.when(last % 2 == 1)
    def _():
        scores(last, 1)
        accumulate(last - 1, 0, False)
        accumulate(last, 1, True)

    @pl.when(last % 2 == 0)
    def _():
        accumulate(last, 0, True)

    oT = part_ref[...] + g[1:2, :] * (acc_ref[...] / l_ref[...])
    pairs = [jnp.concatenate([oT[:, (2 * i) * Q:(2 * i + 1) * Q], oT[:, (2 * i + 1) * Q:(2 * i + 2) * Q]],
                             axis=0).T for i in range(R // 2)]
    o_ref[...] = jnp.concatenate(pairs, axis=1)


def _nsa_attention(qT, kc, vcT, ks, vsT, kw, vwT, gT):
    B, G, NQB, _, RQ = qT.shape
    S = ks.shape[2]
    NC = kc.shape[2]
    NSB = S // SEL_BLOCK
    c = np.arange(NC)[None, :] * CMP_STRIDE
    j = np.arange(NSB)[:, None] * SEL_BLOCK
    ovT = jnp.asarray((c <= j + SEL_BLOCK - 1) & (c + CMP_BLOCK - 1 >= j), jnp.bfloat16)
    per_q = lambda rows: pl.BlockSpec((1, 1, 1, rows, RQ), lambda b, g, q: (b, g, q, 0, 0))
    per_g = lambda d0, d1: pl.BlockSpec((1, 1, d0, d1), lambda b, g, q: (b, g, 0, 0))
    return pl.pallas_call(
        _nsa_attn_kernel,
        grid=(B, G, NQB),
        in_specs=[per_q(HEAD_DIM), per_g(NC, HEAD_DIM), per_g(HEAD_DIM, NC), _full((NSB, NC)),
                  per_g(S, V7X_LANES), per_g(HEAD_DIM, S), per_g(S, HEAD_DIM), per_g(HEAD_DIM, S),
                  per_q(V7X_SUBLANES)],
        out_specs=pl.BlockSpec((Q_BLOCK, NSA_R * HEAD_DIM), lambda b, g, q: (b * NQB + q, g)),
        out_shape=jax.ShapeDtypeStruct((B * S, NSA_DIM), jnp.float32),
        scratch_shapes=[pltpu.VMEM((NSB, Q_BLOCK), jnp.float32), pltpu.VMEM((NSB, Q_BLOCK), jnp.int32),
                        pltpu.VMEM((NSB, RQ), jnp.float32),
                        pltpu.VMEM((2, V7X_LANES, RQ), jnp.bfloat16),
                        pltpu.VMEM((2, SEL_KEYS, RQ), jnp.float32), pltpu.VMEM((2, 1, RQ), jnp.float32),
                        pltpu.VMEM((1, RQ), jnp.float32), pltpu.VMEM((1, RQ), jnp.float32),
                        pltpu.VMEM((HEAD_DIM, RQ), jnp.float32), pltpu.VMEM((HEAD_DIM, RQ), jnp.float32)],
        compiler_params=_params("parallel", "parallel", "arbitrary"),
        name="nsa_attention",
    )(qT, kc, vcT, ovT, ks, vsT, kw, vwT, gT)


def _nsa_branch(x2d, w_nsa, pe_k, pe_v, ck_w1, ck_w2, cv_w1, cv_w2, batch, seq_len):
    qT, kc_in, vc_in, ks, vsT, kw, vwT, gT = _nsa_prep(x2d, w_nsa, batch, seq_len)
    kc, vcT = _nsa_compress(kc_in, vc_in, pe_k, pe_v, ck_w1, ck_w2, cv_w1, cv_w2)
    return _nsa_attention(qT, kc, vcT, ks, vsT, kw, vwT, gT)


def _merge_kernel(x_ref, ya_ref, yb_ref, wg_ref, pa_ref, pb_ref, wo_ref, g_ref, b_ref, o_ref):
    bf16 = jnp.bfloat16
    mm = lambda a, w: jnp.dot(a.astype(bf16), w, preferred_element_type=jnp.float32)
    x = x_ref[...]
    gates = jax.nn.sigmoid(mm(x, wg_ref[...]))
    mixed = gates[:, :D_MODEL] * mm(ya_ref[...], pa_ref[...]) + gates[:, D_MODEL:] * mm(yb_ref[...], pb_ref[...])
    o_ref[...] = _layer_norm(ALPHA * x + mm(mixed, wo_ref[...]), g_ref[...], b_ref[...])


def _merge(x2d, y_a, y_b, w_gate, p_a, p_b, w_o, ln_g, ln_b, tm=512):
    M = x2d.shape[0]
    bf16 = jnp.bfloat16
    rows = lambda w: pl.BlockSpec((tm, w), lambda i: (i, 0))
    return pl.pallas_call(
        _merge_kernel,
        grid=(M // tm,),
        in_specs=[rows(D_MODEL), rows(RWKV_DIM), rows(NSA_DIM), _full((D_MODEL, 2 * D_MODEL)),
                  _full((RWKV_DIM, D_MODEL)), _full((NSA_DIM, D_MODEL)), _full((D_MODEL, D_MODEL)),
                  _full((1, D_MODEL)), _full((1, D_MODEL))],
        out_specs=rows(D_MODEL),
        out_shape=jax.ShapeDtypeStruct((M, D_MODEL), jnp.float32),
        compiler_params=_params("parallel"),
        name="merge",
    )(x2d, y_a, y_b, w_gate, p_a.astype(bf16), p_b.astype(bf16), w_o.astype(bf16),
      ln_g.reshape(1, -1), ln_b.reshape(1, -1))


def _mem_kv_kernel(mem_ref, wk_ref, wv_ref, k_out, v_out):
    m = mem_ref[...].astype(jnp.bfloat16)
    k_out[...] = jnp.dot(m, wk_ref[...], preferred_element_type=jnp.float32).astype(k_out.dtype)
    v_out[...] = jnp.dot(m, wv_ref[...], preferred_element_type=jnp.float32).astype(v_out.dtype)


def _mem_kv(mem2d, wk, wv):
    M = mem2d.shape[0]
    bf16 = jnp.bfloat16
    out = jax.ShapeDtypeStruct((M, D_MODEL), bf16)
    return pl.pallas_call(
        _mem_kv_kernel,
        grid=(1,),
        in_specs=[_full((M, D_MODEL)), _full((D_MODEL, D_MODEL)), _full((D_MODEL, D_MODEL))],
        out_specs=[_full((M, D_MODEL))] * 2,
        out_shape=[out, out],
        compiler_params=_params("arbitrary"),
        name="mem_kv",
    )(mem2d, wk.astype(bf16), wv.astype(bf16))


def _xattn_kernel(x_ref, k_ref, v_ref, wq_ref, wo_ref, g_ref, b_ref, o_ref):
    bf16, f32 = jnp.bfloat16, jnp.float32
    x = x_ref[...]
    q = jnp.dot(x.astype(bf16), wq_ref[...], preferred_element_type=f32).astype(bf16)
    cols = [slice(h * X_HEAD_DIM, (h + 1) * X_HEAD_DIM) for h in range(X_HEADS)]
    scores = [lax.dot_general(q[:, sl], k_ref[:, sl], (((1,), (1,)), ((), ())),
                              preferred_element_type=f32) * (X_HEAD_DIM ** -0.5) for sl in cols]
    heads = []
    for s, sl in zip(scores, cols):
        p = jnp.exp(s - jnp.max(s, axis=-1, keepdims=True))
        p = p / jnp.sum(p, axis=-1, keepdims=True)
        heads.append(jnp.dot(p.astype(bf16), v_ref[:, sl], preferred_element_type=f32))
    o = jnp.concatenate(heads, axis=1).astype(bf16)
    xa = jnp.dot(o, wo_ref[...], preferred_element_type=f32)
    o_ref[...] = _layer_norm(ALPHA * x + xa, g_ref[...], b_ref[...])


def _xattn(x2d, k_mem, v_mem, wq, wo, ln_g, ln_b, seq_len, mem_len, tm=512):
    M = x2d.shape[0]
    bf16 = jnp.bfloat16
    seq_tiles = seq_len // tm
    rows = pl.BlockSpec((tm, D_MODEL), lambda i: (i, 0))
    mem_spec = pl.BlockSpec((mem_len, D_MODEL), lambda i: (i // seq_tiles, 0))
    return pl.pallas_call(
        _xattn_kernel,
        grid=(M // tm,),
        in_specs=[rows, mem_spec, mem_spec, _full((D_MODEL, D_MODEL)), _full((D_MODEL, D_MODEL)),
                  _full((1, D_MODEL)), _full((1, D_MODEL))],
        out_specs=rows,
        out_shape=jax.ShapeDtypeStruct((M, D_MODEL), jnp.float32),
        compiler_params=_params("parallel"),
        name="xattn",
    )(x2d, k_mem, v_mem, wq.astype(bf16), wo.astype(bf16), ln_g.reshape(1, -1), ln_b.reshape(1, -1))


FFN_CHUNK = 1408


def _ffn_kernel(seq_tiles, x_ref, xp_ref, wup_ref, cw_ref, cb_ref, wdn_ref, g_ref, b_ref, o_ref):
    bf16, f32 = jnp.bfloat16, jnp.float32
    i = pl.program_id(0)
    tm = x_ref.shape[0]
    H = V7X_SUBLANES
    x = x_ref[...]
    xprev = jnp.where(i % seq_tiles == 0, 0.0, xp_ref[...])
    xe = jnp.concatenate([xprev, x], axis=0).astype(bf16)

    def conv(cols):
        h = jnp.dot(xe, wup_ref[:, cols], preferred_element_type=f32)
        w = cw_ref[:, cols]
        return (h[H - 2:H - 2 + tm] * w[0:1] + h[H - 1:H - 1 + tm] * w[1:2] + h[H:] * w[2:3]
                + cb_ref[:, cols])

    acc = jnp.zeros((tm, D_MODEL), f32)
    for c in range(D_FF // FFN_CHUNK):
        gate = conv(slice(c * FFN_CHUNK, (c + 1) * FFN_CHUNK))
        val = conv(slice(D_FF + c * FFN_CHUNK, D_FF + (c + 1) * FFN_CHUNK))
        act = (gate * jax.nn.sigmoid(gate) * val).astype(bf16)
        acc = acc + jnp.dot(act, wdn_ref[c * FFN_CHUNK:(c + 1) * FFN_CHUNK, :], preferred_element_type=f32)
    o_ref[...] = _layer_norm(ALPHA * x + acc, g_ref[...], b_ref[...])


def _ffn(x2d, w_up, conv_w, conv_b, w_down, ln_g, ln_b, seq_len, tm=512):
    M = x2d.shape[0]
    bf16 = jnp.bfloat16
    seq_tiles = seq_len // tm
    blocks_per_tile = tm // V7X_SUBLANES
    rows = pl.BlockSpec((tm, D_MODEL), lambda i: (i, 0))
    once = lambda shape: pl.BlockSpec(shape, lambda i: (0,) * len(shape), pipeline_mode=pl.Buffered(1))
    return pl.pallas_call(
        functools.partial(_ffn_kernel, seq_tiles),
        grid=(M // tm,),
        in_specs=[rows,
                  pl.BlockSpec((V7X_SUBLANES, D_MODEL), lambda i: (jnp.maximum(i * blocks_per_tile - 1, 0), 0)),
                  once((D_MODEL, 2 * D_FF)), _full((3, 2 * D_FF)), _full((1, 2 * D_FF)),
                  once((D_FF, D_MODEL)), _full((1, D_MODEL)), _full((1, D_MODEL))],
        out_specs=rows,
        out_shape=jax.ShapeDtypeStruct((M, D_MODEL), jnp.float32),
        compiler_params=_params("parallel"),
        name="ffn",
    )(x2d, x2d, w_up.astype(bf16), conv_w, conv_b.reshape(1, -1), w_down.astype(bf16),
      ln_g.reshape(1, -1), ln_b.reshape(1, -1))


def kernel(x, mem, w_in, rwkv_mu, rwkv_w0, rwkv_w2, rwkv_a0, rwkv_a2, rwkv_k_k, rwkv_k_a, rwkv_r_k, rwkv_gn_g, rwkv_gn_b, nsa_pe_k, nsa_pe_v, nsa_ck_w1, nsa_ck_w2, nsa_cv_w1, nsa_cv_w2, merge_p_a, merge_p_b, mix_w_o, ln1_g, ln1_b, xa_wq, xa_wk, xa_wv, xa_wo, ln2_g, ln2_b, ffn_w_up, ffn_conv_w, ffn_conv_b, ffn_w_down, ln3_g, ln3_b):
    B, S, _ = x.shape
    mem_len = mem.shape[1]
    bf16 = jnp.bfloat16
    x2d = x.reshape(B * S, D_MODEL)
    for l in range(DEPTH):
        w = w_in[l]
        w_rwkv = w[:, :RWKV_COLS].astype(bf16)
        w_nsa = _nsa_weight(w[:, RWKV_COLS:RWKV_COLS + NSA_COLS])
        w_gate = w[:, RWKV_COLS + NSA_COLS:].astype(bf16)
        y_a = _rwkv_time_mix(x2d, w_rwkv, rwkv_mu[l], rwkv_w0[l], rwkv_w2[l], rwkv_a0[l], rwkv_a2[l],
                             rwkv_k_k[l], rwkv_k_a[l], rwkv_r_k[l], rwkv_gn_g[l], rwkv_gn_b[l], B, S)
        y_b = _nsa_branch(x2d, w_nsa, nsa_pe_k[l], nsa_pe_v[l], nsa_ck_w1[l], nsa_ck_w2[l],
                          nsa_cv_w1[l], nsa_cv_w2[l], B, S)
        x2d = _merge(x2d, y_a, y_b, w_gate, merge_p_a[l], merge_p_b[l], mix_w_o[l], ln1_g[l], ln1_b[l])
        k_mem, v_mem = _mem_kv(mem.reshape(B * mem_len, D_MODEL), xa_wk[l], xa_wv[l])
        x2d = _xattn(x2d, k_mem, v_mem, xa_wq[l], xa_wo[l], ln2_g[l], ln2_b[l], S, mem_len)
        x2d = _ffn(x2d, ffn_w_up[l], ffn_conv_w[l], ffn_conv_b[l], ffn_w_down[l], ln3_g[l], ln3_b[l], S)
    return x2d.reshape(B, S, D_MODEL)
```

```python
import functools
import math

import jax
import jax.numpy as jnp
import numpy as np
from jax import lax
from jax.experimental import pallas as pl
from jax.experimental.pallas import tpu as pltpu

D_MODEL = 1024
HEAD_DIM = 64
RWKV_DIM = 512
RWKV_HEADS = 8
LORA = 64
RWKV_COLS = 3 * RWKV_DIM + 2 * LORA
GN_EPS = 64e-5
NSA_DIM = 512
NSA_Q_HEADS = 8
NSA_GROUPS = 2
NSA_R = NSA_Q_HEADS // NSA_GROUPS
NSA_KV_DIM = NSA_GROUPS * HEAD_DIM
NSA_COLS = NSA_DIM + 6 * NSA_KV_DIM + 3 * NSA_Q_HEADS
NSA_COLS_PAD = 1408
CMP_BLOCK = 32
CMP_STRIDE = 16
CMP_HIDDEN = 256
SEL_BLOCK = 64
N_SELECT = 16
WINDOW = 512
Q_BLOCK = 128
ROPE_THETA = 500000.0
ROPE_DIM = 16
X_HEADS = 4
X_HEAD_DIM = 256
D_FF = 2816
LN_EPS = 1e-5
DEPTH = 1
ALPHA = (2 * DEPTH) ** 0.25
NEG = -1e30
BIG = 1e30
LOG2E = math.log2(math.e)

V7X_LANES = 128
V7X_SUBLANES = 8
V7X_VMEM_LIMIT_BYTES = 56 * 1024 * 1024

HI = lax.Precision.HIGHEST
RWKV_CHUNK = 64


def _params(*sem):
    return pltpu.CompilerParams(dimension_semantics=sem, vmem_limit_bytes=V7X_VMEM_LIMIT_BYTES)


def _full(shape):
    n = len(shape)
    return pl.BlockSpec(shape, lambda *_: (0,) * n)


def _head_ones(width):
    r = lax.broadcasted_iota(jnp.int32, (width, width), 0) // HEAD_DIM
    c = lax.broadcasted_iota(jnp.int32, (width, width), 1) // HEAD_DIM
    return (r == c).astype(jnp.float32)


def _layer_norm(y, g, b):
    mu = jnp.mean(y, axis=-1, keepdims=True)
    d = y - mu
    var = jnp.mean(d * d, axis=-1, keepdims=True)
    return d * lax.rsqrt(var + LN_EPS) * g + b


def _rwkv_prep_kernel(seq_tiles, x_ref, xp_ref, w_ref, mu_ref, w0_ref, w2_ref, a0_ref, a2_ref,
                      kk_ref, ka_ref, rk_ref,
                      r_out, k_out, v_out, lw_out, a_out, b_out, bonus_out):
    i = pl.program_id(0)
    tm = x_ref.shape[0]
    C = RWKV_DIM
    w = w_ref[...]
    p = jnp.dot(x_ref[...].astype(jnp.bfloat16), w, preferred_element_type=jnp.float32)
    xprev = xp_ref[...].astype(jnp.bfloat16)
    pprev = jnp.dot(xprev, w, preferred_element_type=jnp.float32)[V7X_SUBLANES - 1:V7X_SUBLANES, :]
    pprev = jnp.where(i % seq_tiles == 0, 0.0, pprev)
    row = lax.broadcasted_iota(jnp.int32, (tm, 1), 0)
    shifted = jnp.where(row == 0, pprev, pltpu.roll(p, 1, 0))
    p = p + (shifted - p) * mu_ref[...]
    r, k, v = p[:, :C], p[:, C:2 * C], p[:, 2 * C:3 * C]
    wl = p[:, 3 * C:3 * C + LORA]
    al = p[:, 3 * C + LORA:]
    z = -(w0_ref[...] + jnp.dot(jnp.tanh(wl).astype(jnp.bfloat16), w2_ref[...],
                                preferred_element_type=jnp.float32))
    softplus = jnp.maximum(z, 0.0) + jnp.log(1.0 + jnp.exp(-jnp.abs(z)))
    w_log = -softplus - 0.5
    lw_out[...] = -jnp.exp(w_log)
    a = jax.nn.sigmoid(a0_ref[...] + jnp.dot(al.astype(jnp.bfloat16), a2_ref[...],
                                             preferred_element_type=jnp.float32))
    ones = _head_ones(C).astype(jnp.bfloat16)
    ones2 = jnp.concatenate([ones, ones], axis=0)

    def head_sum(t):
        return jnp.dot(jnp.concatenate(_split(t), axis=1), ones2, preferred_element_type=jnp.float32)

    kk = k * kk_ref[...]
    kk = kk / jnp.maximum(jnp.sqrt(head_sum(kk * kk)), 1e-12)
    kmod = k * (1.0 + (a - 1.0) * ka_ref[...])
    bonus = head_sum(r * kmod * rk_ref[...]) * v
    r_out[...] = r
    k_out[...] = kmod
    v_out[...] = v
    a_out[...] = -kk
    b_out[...] = kk * a
    bonus_out[...] = bonus


def _rwkv_prep(x2d, w_rwkv, mu, w0, w2, a0, a2, k_k, k_a, r_k, seq_len, tm=512):
    M = x2d.shape[0]
    C = RWKV_DIM
    seq_tiles = seq_len // tm
    row = lambda a: a.reshape(1, -1)
    out = jax.ShapeDtypeStruct((M, C), jnp.float32)
    tile = pl.BlockSpec((tm, C), lambda i: (i, 0))
    blocks_per_tile = tm // V7X_SUBLANES
    return pl.pallas_call(
        functools.partial(_rwkv_prep_kernel, seq_tiles),
        grid=(M // tm,),
        in_specs=[
            pl.BlockSpec((tm, D_MODEL), lambda i: (i, 0)),
            pl.BlockSpec((V7X_SUBLANES, D_MODEL), lambda i: (jnp.maximum(i * blocks_per_tile - 1, 0), 0)),
            _full((D_MODEL, RWKV_COLS)), _full((1, RWKV_COLS)), _full((1, C)), _full((LORA, C)),
            _full((1, C)), _full((LORA, C)), _full((1, C)), _full((1, C)), _full((1, C)),
        ],
        out_specs=[tile] * 7,
        out_shape=[out] * 7,
        compiler_params=_params("parallel"),
        name="rwkv_prep",
    )(x2d, x2d, w_rwkv, row(mu), row(w0), w2.astype(jnp.bfloat16), row(a0), a2.astype(jnp.bfloat16),
      row(k_k), row(k_a), row(r_k))


RWKV_GROUP = 4


def _split(x):
    hi = x.astype(jnp.bfloat16)
    return hi, (x - hi.astype(jnp.float32)).astype(jnp.bfloat16)


def _mm(a, b):
    return jnp.dot(a.astype(jnp.bfloat16), b.astype(jnp.bfloat16), preferred_element_type=jnp.float32)


def _rwkv_mix_kernel(r_ref, k_ref, v_ref, lw_ref, a_ref, b_ref, bonus_ref, gng_ref, gnb_ref,
                     y_out, state):
    T, N, GH = RWKV_CHUNK, HEAD_DIM, RWKV_GROUP
    W = GH * N
    f32 = jnp.float32

    @pl.when(pl.program_id(1) == 0)
    def _():
        state[...] = jnp.zeros_like(state)

    ri = lax.broadcasted_iota(jnp.int32, (W, W), 0)
    ci = lax.broadcasted_iota(jnp.int32, (W, W), 1)
    same_head = ri // N == ci // N
    strict, incl = ri > ci, ri >= ci
    eye = (ri == ci).astype(f32)
    tri = (lax.broadcasted_iota(jnp.int32, (T, T), 0)
           >= lax.broadcasted_iota(jnp.int32, (T, T), 1)).astype(jnp.bfloat16)
    mean_w = (_head_ones(RWKV_DIM) * (1.0 / N)).astype(jnp.bfloat16)

    def expand(x):
        return jnp.where(same_head, jnp.concatenate([x] * GH, axis=0), jnp.zeros((), x.dtype))

    def collapse(x):
        out = x[:T]
        for h in range(1, GH):
            out = out + x[h * T:(h + 1) * T]
        return out

    bf = lambda t: t.astype(jnp.bfloat16)
    n_chunks = r_ref.shape[0] // T
    n_groups = RWKV_HEADS // GH
    chains = [(ch, g) for ch in range(n_chunks) for g in range(n_groups)]
    each = lambda fn, *lists: [fn(*args) for args in zip(*lists)]

    def cumsum(x):
        hi, lo = _split(x)
        lo2 = (x - hi.astype(f32) - lo.astype(f32)).astype(jnp.bfloat16)
        return jnp.dot(jnp.concatenate([tri] * 3, axis=1), jnp.concatenate([hi, lo, lo2], axis=0),
                       preferred_element_type=f32)

    c_all = [cumsum(lw_ref[ch * T:(ch + 1) * T, :]) for ch in range(n_chunks)]

    def operands(ch, g):
        rows, cols = slice(ch * T, (ch + 1) * T), slice(g * W, (g + 1) * W)
        lw, c = lw_ref[rows, cols], c_all[ch][:, cols]
        r, k, v = r_ref[rows, cols], k_ref[rows, cols], v_ref[rows, cols]
        a, b = a_ref[rows, cols], b_ref[rows, cols]
        c_last = c[T - 1:T, :]
        e_neg, e_end = jnp.exp(-c), jnp.exp(c_last - c)
        r_t = r * jnp.exp(c)
        return dict(a_x=expand(bf(a * jnp.exp(c - lw))), r_x=expand(bf(r_t)), b_x=expand(bf(b * e_neg)),
                    k_x=expand(bf(k * e_neg)), v_x=expand(bf(v)), r_t=r_t, b_end=b * e_end, k_end=k * e_end,
                    w_end=jnp.exp(c_last))

    ops = [operands(ch, g) for ch, g in chains]
    P = [lax.dot_general(jnp.concatenate([o["a_x"], o["r_x"]], axis=0),
                         jnp.concatenate([o["b_x"], o["k_x"]], axis=0),
                         (((1,), (1,)), ((), ())), preferred_element_type=f32) for o in ops]
    L_ab = [jnp.where(strict, p[:W, :W], 0.0) for p in P]
    L_ak = [bf(jnp.where(strict, p[:W, W:], 0.0)) for p in P]
    M_rb = [bf(jnp.where(incl, p[W:, :W], 0.0)) for p in P]
    M_rk = [bf(jnp.where(incl, p[W:, W:], 0.0)) for p in P]
    v_x = [o["v_x"] for o in ops]

    base = V7X_SUBLANES
    D = [bf(jnp.where(ri // base == ci // base, l, 0.0)) for l in L_ab]
    D2 = each(_mm, D, D)
    D4 = each(_mm, D2, D2)
    X = each(_mm, each(lambda d, d2: _mm(eye + d.astype(f32), eye + d2), D, D2), [eye + d4 for d4 in D4])
    blk = base
    while blk < T:
        pair = (ri // (2 * blk) == ci // (2 * blk)) & (ri // blk != ci // blk)
        Xb = [bf(x) for x in X]
        step = each(_mm, each(_mm, Xb, [jnp.where(pair, l, 0.0) for l in L_ab]), Xb)
        X = [x + s for x, s in zip(X, step)]
        blk *= 2

    lakv = each(_mm, L_ak, v_x)
    AV = [bf(_mm(x, jnp.concatenate([o["a_x"], bf(t)], axis=1))) for x, o, t in zip(X, ops, lakv)]
    ry = each(_mm, M_rb, AV)
    mrkv = each(_mm, M_rk, v_x)
    gh = [_mm(expand(o["b_end"]).T, av) for o, av in zip(ops, AV)]
    kv = [_mm(expand(o["k_end"]).T, vx) for o, vx in zip(ops, v_x)]
    rp = [bf(expand(o["r_t"]) + t[:, :W]) for o, t in zip(ops, ry)]
    yp = [t[:, W:] + u for t, u in zip(ry, mrkv)]
    G = [bf(eye * o["w_end"] + t[:, :W]) for o, t in zip(ops, gh)]
    H = [t[:, W:] + u for t, u in zip(gh, kv)]

    y_rows = []
    for ch in range(n_chunks):
        ys = []
        for g in range(n_groups):
            i = chains.index((ch, g))
            s0 = bf(state[g])
            ys.append(collapse(_mm(rp[i], s0) + yp[i]))
            state[g] = _mm(G[i], s0) + H[i]
        y_rows.append(jnp.concatenate(ys, axis=1))
    y = jnp.concatenate(y_rows, axis=0)
    y_hi, y_lo = _split(y)
    ym = jnp.dot(jnp.concatenate([y_hi, y_lo], axis=1), jnp.concatenate([mean_w, mean_w], axis=0),
                 preferred_element_type=f32)
    d = y - ym
    yv = _mm(d * d, mean_w)
    y_out[...] = d * lax.rsqrt(yv + GN_EPS) * gng_ref[...] + gnb_ref[...] + bonus_ref[...]


def _rwkv_mix(r, k, v, lw, a, b, bonus, gn_g, gn_b, batch, chunks_per_step=4):
    M, C = r.shape
    rows = RWKV_CHUNK * chunks_per_step
    steps = M // rows // batch
    W = RWKV_GROUP * HEAD_DIM
    tile = pl.BlockSpec((rows, C), lambda bi, ci: (bi * steps + ci, 0))
    return pl.pallas_call(
        _rwkv_mix_kernel,
        grid=(batch, steps),
        in_specs=[tile] * 7 + [_full((1, C)), _full((1, C))],
        out_specs=tile,
        out_shape=jax.ShapeDtypeStruct((M, C), jnp.float32),
        scratch_shapes=[pltpu.VMEM((RWKV_HEADS // RWKV_GROUP, W, W), jnp.float32)],
        compiler_params=_params("parallel", "arbitrary"),
        name="rwkv_mix",
    )(r, k, v, lw, a, b, bonus, gn_g.reshape(1, C), gn_b.reshape(1, C))


def _rwkv_time_mix(x2d, w_rwkv, mu, w0, w2, a0, a2, k_k, k_a, r_k, gn_g, gn_b, batch, seq_len):
    r, k, v, lw, a, b, bonus = _rwkv_prep(x2d, w_rwkv, mu, w0, w2, a0, a2, k_k, k_a, r_k, seq_len)
    return _rwkv_mix(r, k, v, lw, a, b, bonus, gn_g, gn_b, batch)


def _rope_tables(seq_len):
    half = ROPE_DIM // 2
    inv = ROPE_THETA ** (-np.arange(half, dtype=np.float64) * 2.0 / ROPE_DIM)
    ang = np.arange(seq_len, dtype=np.float64)[:, None] * inv[None, :]
    cos, sin = np.cos(ang).astype(np.float32), np.sin(ang).astype(np.float32)
    pad = np.zeros((seq_len, HEAD_DIM - ROPE_DIM), np.float32)
    zero = np.zeros_like(sin)
    c = np.concatenate([cos, cos, pad + 1.0], axis=1)
    s_lo = np.concatenate([-sin, zero, pad], axis=1)
    s_hi = np.concatenate([zero, sin, pad], axis=1)
    two = lambda t: jnp.asarray(np.concatenate([t, t], axis=1))
    return two(c), two(s_lo), two(s_hi)


def _rope_pair(x, c, s_lo, s_hi):
    return x * c + pltpu.roll(x, V7X_LANES - ROPE_DIM // 2, 1) * s_lo + pltpu.roll(x, ROPE_DIM // 2, 1) * s_hi


GATE_SLOTS = V7X_SUBLANES


def _nsa_prep_kernel(x_ref, w_ref, c_ref, slo_ref, shi_ref,
                     qT_out, kc_out, vc_out, ks_out, vsT_out, kw_out, vwT_out, gT_out):
    f32 = jnp.float32
    L, N, G, R, Q = V7X_LANES, HEAD_DIM, NSA_GROUPS, NSA_R, Q_BLOCK
    tm = x_ref.shape[0]
    p = jnp.dot(x_ref[...].astype(jnp.bfloat16), w_ref[...], preferred_element_type=f32)
    c, s_lo, s_hi = c_ref[...], slo_ref[...], shi_ref[...]
    rope = lambda t: _rope_pair(t, c, s_lo, s_hi)
    kv = lambda i: p[:, NSA_DIM + i * L:NSA_DIM + (i + 1) * L]

    qT = [(rope(p[:, j * L:(j + 1) * L]) * (N ** -0.5 * LOG2E)).T for j in range(NSA_DIM // L)]
    gT = jax.nn.sigmoid(kv(6)).T
    for g in range(G):
        for qb in range(tm // Q):
            blk = slice(qb * Q, (qb + 1) * Q)
            heads = [qT[(g * R + r) // 2][((g * R + r) % 2) * N:((g * R + r) % 2 + 1) * N, blk]
                     for r in range(R)]
            qT_out[0, g, qb] = jnp.concatenate(heads, axis=1).astype(qT_out.dtype)
            gates = [gT[(g * R + r) * GATE_SLOTS:(g * R + r + 1) * GATE_SLOTS, blk] for r in range(R)]
            gT_out[0, g, qb] = jnp.concatenate(gates, axis=1)

    k_c, k_s, k_w = rope(kv(0)), rope(kv(2)), rope(kv(4))
    v_c = kv(1)
    v_sT, v_wT = kv(3).T, kv(5).T
    slot = (lax.broadcasted_iota(jnp.int32, (tm, L - N), 0) // SEL_BLOCK) % SEL_SLOTS
    onehot = (slot == lax.broadcasted_iota(jnp.int32, (tm, L - N), 1)).astype(f32)
    for g in range(G):
        cols = slice(g * N, (g + 1) * N)
        kc_out[0, g] = k_c[:, cols]
        vc_out[0, g] = v_c[:, cols]
        ks_out[0, g] = jnp.concatenate([k_s[:, cols], onehot], axis=1).astype(ks_out.dtype)
        kw_out[0, g] = k_w[:, cols].astype(kw_out.dtype)
        vsT_out[0, g] = v_sT[cols, :].astype(vsT_out.dtype)
        vwT_out[0, g] = v_wT[cols, :].astype(vwT_out.dtype)


def _nsa_weight(w):
    n_qkv = NSA_DIM + 6 * NSA_KV_DIM
    gates = w[:, n_qkv:].reshape(-1, NSA_Q_HEADS, 3)
    gates = jnp.pad(gates, ((0, 0), (0, 0), (0, GATE_SLOTS - 3))).reshape(-1, NSA_Q_HEADS * GATE_SLOTS)
    gates = jnp.pad(gates, ((0, 0), (0, NSA_COLS_PAD - n_qkv - NSA_Q_HEADS * GATE_SLOTS)))
    return jnp.concatenate([w[:, :n_qkv], gates], axis=1).astype(jnp.bfloat16)


def _nsa_prep(x2d, w_nsa, batch, seq_len, tm=512):
    B, S, G, N, L, Q = batch, seq_len, NSA_GROUPS, HEAD_DIM, V7X_LANES, Q_BLOCK
    RQ = NSA_R * Q
    tiles = S // tm
    tabs = _rope_tables(S)
    tab_spec = pl.BlockSpec((tm, L), lambda i: (i % tiles, 0))
    f32, bf16 = jnp.float32, jnp.bfloat16
    per_q = lambda rows: pl.BlockSpec((1, G, tm // Q, rows, RQ), lambda i: (i // tiles, 0, i % tiles, 0, 0))
    by_row = lambda w: pl.BlockSpec((1, G, tm, w), lambda i: (i // tiles, 0, i % tiles, 0))
    by_col = pl.BlockSpec((1, G, N, tm), lambda i: (i // tiles, 0, 0, i % tiles))
    sds = jax.ShapeDtypeStruct
    return pl.pallas_call(
        _nsa_prep_kernel,
        grid=(B * tiles,),
        in_specs=[pl.BlockSpec((tm, D_MODEL), lambda i: (i, 0)), _full((D_MODEL, NSA_COLS_PAD)),
                  tab_spec, tab_spec, tab_spec],
        out_specs=[per_q(N), by_row(N), by_row(N), by_row(L), by_col, by_row(N), by_col, per_q(GATE_SLOTS)],
        out_shape=[sds((B, G, S // Q, N, RQ), bf16), sds((B, G, S, N), f32), sds((B, G, S, N), f32),
                   sds((B, G, S, L), bf16), sds((B, G, N, S), bf16), sds((B, G, S, N), bf16),
                   sds((B, G, N, S), bf16), sds((B, G, S // Q, GATE_SLOTS, RQ), f32)],
        compiler_params=_params("parallel"),
        name="nsa_prep",
    )(x2d, w_nsa, *tabs)


def _gelu_tanh(x):
    return 0.5 * x * (1.0 + jnp.tanh(math.sqrt(2.0 / math.pi) * (x + 0.044715 * x * x * x)))


def _nsa_compress_kernel(xk_ref, xv_ref, pek_ref, pev_ref, kw1_ref, kw2_ref, vw1_ref, vw2_ref,
                         kc_out, vcT_out):
    N, bf16, f32 = HEAD_DIM, jnp.bfloat16, jnp.float32
    n_blocks = kc_out.shape[2]

    def mlp(x_ref, pe_ref, w1_ref, w2_ref):
        lo = jnp.zeros((n_blocks, CMP_HIDDEN), f32)
        hi = jnp.zeros((n_blocks, CMP_HIDDEN), f32)
        for l in range(CMP_STRIDE):
            rows = x_ref[0, 0, pl.ds(l, n_blocks, stride=CMP_STRIDE), :]
            lo = lo + jnp.dot((rows + pe_ref[l:l + 1, :]).astype(bf16), w1_ref[l * N:(l + 1) * N, :],
                              preferred_element_type=f32)
            m = CMP_STRIDE + l
            hi = hi + jnp.dot((rows + pe_ref[m:m + 1, :]).astype(bf16), w1_ref[m * N:(m + 1) * N, :],
                              preferred_element_type=f32)
        pre = lo + pltpu.roll(hi, n_blocks - 1, 0)
        return jnp.dot(_gelu_tanh(pre).astype(bf16), w2_ref[...], preferred_element_type=f32)

    kc_out[0, 0] = mlp(xk_ref, pek_ref, kw1_ref, kw2_ref).astype(kc_out.dtype)
    vc = mlp(xv_ref, pev_ref, vw1_ref, vw2_ref)
    vcT_out[0, 0] = jnp.concatenate([vc, jnp.zeros_like(vc)], axis=1).T[:N].astype(vcT_out.dtype)


def _nsa_compress(xk, xv, pe_k, pe_v, ck_w1, ck_w2, cv_w1, cv_w2):
    B, G, S, N = xk.shape
    NC = S // CMP_STRIDE
    bf16 = jnp.bfloat16
    xin = pl.BlockSpec((1, 1, S, N), lambda b, g: (b, g, 0, 0))
    return pl.pallas_call(
        _nsa_compress_kernel,
        grid=(B, G),
        in_specs=[xin, xin, _full((CMP_BLOCK, N)), _full((CMP_BLOCK, N)), _full((CMP_BLOCK * N, CMP_HIDDEN)),
                  _full((CMP_HIDDEN, N)), _full((CMP_BLOCK * N, CMP_HIDDEN)), _full((CMP_HIDDEN, N))],
        out_specs=[pl.BlockSpec((1, 1, NC, N), lambda b, g: (b, g, 0, 0)),
                   pl.BlockSpec((1, 1, N, NC), lambda b, g: (b, g, 0, 0))],
        out_shape=[jax.ShapeDtypeStruct((B, G, NC, N), bf16), jax.ShapeDtypeStruct((B, G, N, NC), bf16)],
        compiler_params=_params("parallel", "parallel"),
        name="nsa_compress",
    )(xk, xv, pe_k, pe_v, ck_w1.astype(bf16), ck_w2.astype(bf16), cv_w1.astype(bf16), cv_w2.astype(bf16))


SEL_KEYS = 512
SEL_SLOTS = V7X_SUBLANES
CMP_TILE = 128


def _nsa_attn_kernel(qT_ref, kc_ref, vcT_ref, ovT_ref, ks_ref, vsT_ref, kw_ref, vwT_ref, gT_ref,
                     o_ref, val_ref, cnt_ref, sel_ref, qa_ref, s_ref, cm_ref, m_ref, l_ref, acc_ref, part_ref,
                     imp_ref):
    qb = pl.program_id(2)
    t0 = qb * Q_BLOCK
    R, Q = NSA_R, Q_BLOCK
    RQ = R * Q
    f32, bf16 = jnp.float32, jnp.bfloat16
    qT = qT_ref[0, 0, 0]
    lane = lax.broadcasted_iota(jnp.int32, (1, RQ), 1)
    tpos = t0 + lane % Q
    mm = lambda a, b: jnp.dot(a, b, preferred_element_type=f32)

    g = gT_ref[0, 0, 0]
    NC = kc_ref.shape[2]
    NSB = ovT_ref.shape[0]

    span = WINDOW + Q
    sub = lax.broadcasted_iota(jnp.int32, (Q, 1), 0)
    qpos = lane % Q

    def local_branches(rows, aligned_window):
        w0 = pl.multiple_of(t0 - WINDOW, Q) if aligned_window else 0
        s_cmp = mm(kc_ref[0, 0, :rows, :], qT)
        s_win = mm(kw_ref[0, 0, pl.ds(w0, span), :], qT)

        cend = lax.broadcasted_iota(jnp.int32, (rows, 1), 0) * CMP_STRIDE + (CMP_BLOCK - 1)
        mask = cend <= tpos
        sm = jnp.where(mask, s_cmp, NEG)
        e = jnp.where(mask, jnp.exp2(sm - jnp.max(sm, axis=0, keepdims=True)), 0.0)
        den = jnp.sum(e, axis=0, keepdims=True)
        p = e / jnp.where(den > 0.0, den, 1.0)
        o_c = mm(vcT_ref[0, 0, :, :rows], p.astype(bf16))
        psum = p[:, :Q]
        for r in range(1, R):
            psum = psum + p[:, r * Q:(r + 1) * Q]
        ps_hi, ps_lo = _split(psum)
        ov = ovT_ref[:, :rows]
        imp_ref[...] = mm(jnp.concatenate([ov, ov], axis=1), jnp.concatenate([ps_hi, ps_lo], axis=0))

        if aligned_window:
            head = jnp.where(sub > qpos, s_win[:Q], NEG)
            tail = jnp.where(sub <= qpos, s_win[WINDOW:], NEG)
            sm = jnp.concatenate([head, s_win[Q:WINDOW], tail], axis=0)
        else:
            diff = tpos - lax.broadcasted_iota(jnp.int32, (span, 1), 0)
            sm = jnp.where((diff >= 0) & (diff < WINDOW), s_win, NEG)
        p = jnp.exp2(sm - jnp.max(sm, axis=0, keepdims=True))
        den = jnp.sum(p, axis=0, keepdims=True)
        o_w = mm(vwT_ref[0, 0, :, pl.ds(w0, span)], p.astype(bf16)) / den
        part_ref[...] = g[0:1, :] * o_c + g[2:3, :] * o_w

    tile = min(CMP_TILE, NC)
    per_tile = tile * CMP_STRIDE // Q
    n_tiles = NC // tile
    first_aligned = WINDOW // Q
    for i in range(n_tiles):
        lo, hi = i * per_tile, (i + 1) * per_tile
        cuts = [lo, hi] if not lo < first_aligned < hi else [lo, first_aligned, hi]
        for a, b in zip(cuts[:-1], cuts[1:]):
            in_range = (qb >= a) if (i == n_tiles - 1 and b == hi) else ((qb >= a) & (qb < b))
            pl.when(in_range)(functools.partial(local_branches, (i + 1) * tile, a >= first_aligned))
    imp = imp_ref[...]

    tq = t0 + lax.broadcasted_iota(jnp.int32, (1, Q), 1)
    jblk = lax.broadcasted_iota(jnp.int32, (NSB, 1), 0)
    cur = tq // SEL_BLOCK
    forced = (jblk == 0) | (jblk == cur) | (jblk == cur - 1)
    valid = jblk * SEL_BLOCK <= tq
    val = jnp.where(valid, jnp.where(forced, BIG, imp), NEG)
    val_ref[...] = val
    n_live = (t0 + Q - 1) // SEL_BLOCK + 1

    def strict_body(u, cnt):
        for i in (2 * u, 2 * u + 1):
            cnt = cnt + jnp.where(val_ref[pl.ds(i, 1), :] > val, 1, 0)
        return cnt

    cnt_ref[...] = lax.fori_loop(0, n_live // 2, strict_body, jnp.zeros((NSB, Q), jnp.int32))
    taken = jnp.sum(jnp.where(valid & (cnt_ref[...] < N_SELECT), 1, 0), axis=0, keepdims=True)

    @pl.when(jnp.max(taken) > N_SELECT)
    def _():
        def tie_body(i, cnt):
            row = val_ref[pl.ds(i, 1), :]
            ge = jnp.where(row >= val, 1, 0)
            gt = jnp.where(row > val, 1, 0)
            return cnt + jnp.where(jblk > i, ge, gt)

        cnt_ref[...] = lax.fori_loop(0, n_live, tie_body, jnp.zeros((NSB, Q), jnp.int32))

    bias = jnp.where(cnt_ref[...] < N_SELECT, 0.0, NEG)
    sel_ref[...] = jnp.concatenate([bias] * R, axis=1)

    bps = SEL_SLOTS
    for buf in range(qa_ref.shape[0]):
        qa_ref[buf, :HEAD_DIM, :] = qT
        qa_ref[buf, HEAD_DIM:, :] = jnp.zeros((qa_ref.shape[1] - HEAD_DIM, RQ), bf16)
    m_ref[...] = jnp.full(m_ref.shape, NEG, f32)
    l_ref[...] = jnp.zeros(l_ref.shape, f32)
    acc_ref[...] = jnp.zeros(acc_ref.shape, f32)
    krow = lax.broadcasted_iota(jnp.int32, (SEL_KEYS, 1), 0)

    def scores(kc, slot):
        k0 = pl.multiple_of(kc * SEL_KEYS, SEL_KEYS)
        grp = pl.multiple_of((kc * (SEL_KEYS // SEL_BLOCK)) // bps * bps, bps)
        rows = sel_ref[pl.ds(grp, bps), :]
        qa_ref[slot, HEAD_DIM:HEAD_DIM + 2 * bps, :] = (
            jnp.concatenate([rows, jnp.zeros_like(rows)], axis=0).astype(bf16))
        s = mm(ks_ref[0, 0, pl.ds(k0, SEL_KEYS), :], qa_ref[slot])
        s_ref[slot] = s
        cm_ref[slot] = jnp.max(s, axis=0, keepdims=True)

    def accumulate(kc, slot, diagonal):
        k0 = pl.multiple_of(kc * SEL_KEYS, SEL_KEYS)
        s, cm = s_ref[slot], cm_ref[slot]
        if diagonal:
            s = jnp.where(k0 + krow <= tpos, s, NEG)
            cm = jnp.max(s, axis=0, keepdims=True)
        m = m_ref[...]
        m_new = jnp.maximum(m, cm)
        alpha = jnp.exp2(m - m_new)
        p = jnp.exp2(s - m_new)
        l_ref[...] = alpha * l_ref[...] + jnp.sum(p, axis=0, keepdims=True)
        m_ref[...] = m_new
        acc_ref[...] = alpha * acc_ref[...] + mm(vsT_ref[0, 0, :, pl.ds(k0, SEL_KEYS)], p.astype(bf16))

    last = (t0 + Q - 1) // SEL_KEYS
    scores(0, 0)

    def sel_body(j, carry):
        scores(2 * j + 1, 1)
        accumulate(2 * j, 0, False)
        scores(2 * j + 2, 0)
        accumulate(2 * j + 1, 1, False)
        return carry

    lax.fori_loop(0, last // 2, sel_body, 0)

    @pl.when(last % 2 == 1)
    def _():
        scores(last, 1)
        accumulate(last - 1, 0, False)
        accumulate(last, 1, True)

    @pl.when(last % 2 == 0)
    def _():
        accumulate(last, 0, True)

    oT = part_ref[...] + g[1:2, :] * (acc_ref[...] / l_ref[...])
    pairs = [jnp.concatenate([oT[:, (2 * i) * Q:(2 * i + 1) * Q], oT[:, (2 * i + 1) * Q:(2 * i + 2) * Q]],
                             axis=0).T for i in range(R // 2)]
    o_ref[...] = jnp.concatenate(pairs, axis=1)


def _nsa_attention(qT, kc, vcT, ks, vsT, kw, vwT, gT):
    B, G, NQB, _, RQ = qT.shape
    S = ks.shape[2]
    NC = kc.shape[2]
    NSB = S // SEL_BLOCK
    c = np.arange(NC)[None, :] * CMP_STRIDE
    j = np.arange(NSB)[:, None] * SEL_BLOCK
    ovT = jnp.asarray((c <= j + SEL_BLOCK - 1) & (c + CMP_BLOCK - 1 >= j), jnp.bfloat16)
    per_q = lambda rows: pl.BlockSpec((1, 1, 1, rows, RQ), lambda b, g, q: (b, g, q, 0, 0))
    per_g = lambda d0, d1: pl.BlockSpec((1, 1, d0, d1), lambda b, g, q: (b, g, 0, 0))
    return pl.pallas_call(
        _nsa_attn_kernel,
        grid=(B, G, NQB),
        in_specs=[per_q(HEAD_DIM), per_g(NC, HEAD_DIM), per_g(HEAD_DIM, NC), _full((NSB, NC)),
                  per_g(S, V7X_LANES), per_g(HEAD_DIM, S), per_g(S, HEAD_DIM), per_g(HEAD_DIM, S),
                  per_q(V7X_SUBLANES)],
        out_specs=pl.BlockSpec((Q_BLOCK, NSA_R * HEAD_DIM), lambda b, g, q: (b * NQB + q, g)),
        out_shape=jax.ShapeDtypeStruct((B * S, NSA_DIM), jnp.float32),
        scratch_shapes=[pltpu.VMEM((NSB, Q_BLOCK), jnp.float32), pltpu.VMEM((NSB, Q_BLOCK), jnp.int32),
                        pltpu.VMEM((NSB, RQ), jnp.float32),
                        pltpu.VMEM((2, V7X_LANES, RQ), jnp.bfloat16),
                        pltpu.VMEM((2, SEL_KEYS, RQ), jnp.float32), pltpu.VMEM((2, 1, RQ), jnp.float32),
                        pltpu.VMEM((1, RQ), jnp.float32), pltpu.VMEM((1, RQ), jnp.float32),
                        pltpu.VMEM((HEAD_DIM, RQ), jnp.float32), pltpu.VMEM((HEAD_DIM, RQ), jnp.float32),
                        pltpu.VMEM((NSB, Q_BLOCK), jnp.float32)],
        compiler_params=_params("parallel", "parallel", "arbitrary"),
        name="nsa_attention",
    )(qT, kc, vcT, ovT, ks, vsT, kw, vwT, gT)


def _nsa_branch(x2d, w_nsa, pe_k, pe_v, ck_w1, ck_w2, cv_w1, cv_w2, batch, seq_len):
    qT, kc_in, vc_in, ks, vsT, kw, vwT, gT = _nsa_prep(x2d, w_nsa, batch, seq_len)
    kc, vcT = _nsa_compress(kc_in, vc_in, pe_k, pe_v, ck_w1, ck_w2, cv_w1, cv_w2)
    return _nsa_attention(qT, kc, vcT, ks, vsT, kw, vwT, gT)


def _merge_kernel(x_ref, ya_ref, yb_ref, wg_ref, pa_ref, pb_ref, wo_ref, g_ref, b_ref, o_ref):
    bf16 = jnp.bfloat16
    mm = lambda a, w: jnp.dot(a.astype(bf16), w, preferred_element_type=jnp.float32)
    x = x_ref[...]
    gates = jax.nn.sigmoid(mm(x, wg_ref[...]))
    mixed = gates[:, :D_MODEL] * mm(ya_ref[...], pa_ref[...]) + gates[:, D_MODEL:] * mm(yb_ref[...], pb_ref[...])
    o_ref[...] = _layer_norm(ALPHA * x + mm(mixed, wo_ref[...]), g_ref[...], b_ref[...])


def _merge(x2d, y_a, y_b, w_gate, p_a, p_b, w_o, ln_g, ln_b, tm=512):
    M = x2d.shape[0]
    bf16 = jnp.bfloat16
    rows = lambda w: pl.BlockSpec((tm, w), lambda i: (i, 0))
    return pl.pallas_call(
        _merge_kernel,
        grid=(M // tm,),
        in_specs=[rows(D_MODEL), rows(RWKV_DIM), rows(NSA_DIM), _full((D_MODEL, 2 * D_MODEL)),
                  _full((RWKV_DIM, D_MODEL)), _full((NSA_DIM, D_MODEL)), _full((D_MODEL, D_MODEL)),
                  _full((1, D_MODEL)), _full((1, D_MODEL))],
        out_specs=rows(D_MODEL),
        out_shape=jax.ShapeDtypeStruct((M, D_MODEL), jnp.float32),
        compiler_params=_params("parallel"),
        name="merge",
    )(x2d, y_a, y_b, w_gate, p_a.astype(bf16), p_b.astype(bf16), w_o.astype(bf16),
      ln_g.reshape(1, -1), ln_b.reshape(1, -1))


def _mem_kv_kernel(mem_ref, wk_ref, wv_ref, k_out, v_out):
    m = mem_ref[...].astype(jnp.bfloat16)
    k_out[...] = jnp.dot(m, wk_ref[...], preferred_element_type=jnp.float32).astype(k_out.dtype)
    v_out[...] = jnp.dot(m, wv_ref[...], preferred_element_type=jnp.float32).astype(v_out.dtype)


def _mem_kv(mem2d, wk, wv):
    M = mem2d.shape[0]
    bf16 = jnp.bfloat16
    out = jax.ShapeDtypeStruct((M, D_MODEL), bf16)
    return pl.pallas_call(
        _mem_kv_kernel,
        grid=(1,),
        in_specs=[_full((M, D_MODEL)), _full((D_MODEL, D_MODEL)), _full((D_MODEL, D_MODEL))],
        out_specs=[_full((M, D_MODEL))] * 2,
        out_shape=[out, out],
        compiler_params=_params("arbitrary"),
        name="mem_kv",
    )(mem2d, wk.astype(bf16), wv.astype(bf16))


def _xattn_kernel(x_ref, k_ref, v_ref, wq_ref, wo_ref, g_ref, b_ref, o_ref):
    bf16, f32 = jnp.bfloat16, jnp.float32
    x = x_ref[...]
    q = jnp.dot(x.astype(bf16), wq_ref[...], preferred_element_type=f32).astype(bf16)
    cols = [slice(h * X_HEAD_DIM, (h + 1) * X_HEAD_DIM) for h in range(X_HEADS)]
    scores = [lax.dot_general(q[:, sl], k_ref[:, sl], (((1,), (1,)), ((), ())),
                              preferred_element_type=f32) * (X_HEAD_DIM ** -0.5) for sl in cols]
    heads = []
    for s, sl in zip(scores, cols):
        p = jnp.exp(s - jnp.max(s, axis=-1, keepdims=True))
        p = p / jnp.sum(p, axis=-1, keepdims=True)
        heads.append(jnp.dot(p.astype(bf16), v_ref[:, sl], preferred_element_type=f32))
    o = jnp.concatenate(heads, axis=1).astype(bf16)
    xa = jnp.dot(o, wo_ref[...], preferred_element_type=f32)
    o_ref[...] = _layer_norm(ALPHA * x + xa, g_ref[...], b_ref[...])


def _xattn(x2d, k_mem, v_mem, wq, wo, ln_g, ln_b, seq_len, mem_len, tm=512):
    M = x2d.shape[0]
    bf16 = jnp.bfloat16
    seq_tiles = seq_len // tm
    rows = pl.BlockSpec((tm, D_MODEL), lambda i: (i, 0))
    mem_spec = pl.BlockSpec((mem_len, D_MODEL), lambda i: (i // seq_tiles, 0))
    return pl.pallas_call(
        _xattn_kernel,
        grid=(M // tm,),
        in_specs=[rows, mem_spec, mem_spec, _full((D_MODEL, D_MODEL)), _full((D_MODEL, D_MODEL)),
                  _full((1, D_MODEL)), _full((1, D_MODEL))],
        out_specs=rows,
        out_shape=jax.ShapeDtypeStruct((M, D_MODEL), jnp.float32),
        compiler_params=_params("parallel"),
        name="xattn",
    )(x2d, k_mem, v_mem, wq.astype(bf16), wo.astype(bf16), ln_g.reshape(1, -1), ln_b.reshape(1, -1))


FFN_CHUNK = 1408


def _ffn_kernel(seq_tiles, x_ref, xp_ref, wup_ref, cw_ref, cb_ref, wdn_ref, g_ref, b_ref, o_ref):
    bf16, f32 = jnp.bfloat16, jnp.float32
    i = pl.program_id(0)
    tm = x_ref.shape[0]
    H = V7X_SUBLANES
    x = x_ref[...]
    xprev = jnp.where(i % seq_tiles == 0, 0.0, xp_ref[...])
    xe = jnp.concatenate([xprev, x], axis=0).astype(bf16)

    def conv(cols):
        h = jnp.dot(xe, wup_ref[:, cols], preferred_element_type=f32)
        w = cw_ref[:, cols]
        return (h[H - 2:H - 2 + tm] * w[0:1] + h[H - 1:H - 1 + tm] * w[1:2] + h[H:] * w[2:3]
                + cb_ref[:, cols])

    acc = jnp.zeros((tm, D_MODEL), f32)
    for c in range(D_FF // FFN_CHUNK):
        gate = conv(slice(c * FFN_CHUNK, (c + 1) * FFN_CHUNK))
        val = conv(slice(D_FF + c * FFN_CHUNK, D_FF + (c + 1) * FFN_CHUNK))
        act = (gate * jax.nn.sigmoid(gate) * val).astype(bf16)
        acc = acc + jnp.dot(act, wdn_ref[c * FFN_CHUNK:(c + 1) * FFN_CHUNK, :], preferred_element_type=f32)
    o_ref[...] = _layer_norm(ALPHA * x + acc, g_ref[...], b_ref[...])


def _ffn(x2d, w_up, conv_w, conv_b, w_down, ln_g, ln_b, seq_len, tm=512):
    M = x2d.shape[0]
    bf16 = jnp.bfloat16
    seq_tiles = seq_len // tm
    blocks_per_tile = tm // V7X_SUBLANES
    rows = pl.BlockSpec((tm, D_MODEL), lambda i: (i, 0))
    once = lambda shape: pl.BlockSpec(shape, lambda i: (0,) * len(shape), pipeline_mode=pl.Buffered(1))
    return pl.pallas_call(
        functools.partial(_ffn_kernel, seq_tiles),
        grid=(M // tm,),
        in_specs=[rows,
                  pl.BlockSpec((V7X_SUBLANES, D_MODEL), lambda i: (jnp.maximum(i * blocks_per_tile - 1, 0), 0)),
                  once((D_MODEL, 2 * D_FF)), _full((3, 2 * D_FF)), _full((1, 2 * D_FF)),
                  once((D_FF, D_MODEL)), _full((1, D_MODEL)), _full((1, D_MODEL))],
        out_specs=rows,
        out_shape=jax.ShapeDtypeStruct((M, D_MODEL), jnp.float32),
        compiler_params=_params("parallel"),
        name="ffn",
    )(x2d, x2d, w_up.astype(bf16), conv_w, conv_b.reshape(1, -1), w_down.astype(bf16),
      ln_g.reshape(1, -1), ln_b.reshape(1, -1))


def kernel(x, mem, w_in, rwkv_mu, rwkv_w0, rwkv_w2, rwkv_a0, rwkv_a2, rwkv_k_k, rwkv_k_a, rwkv_r_k, rwkv_gn_g, rwkv_gn_b, nsa_pe_k, nsa_pe_v, nsa_ck_w1, nsa_ck_w2, nsa_cv_w1, nsa_cv_w2, merge_p_a, merge_p_b, mix_w_o, ln1_g, ln1_b, xa_wq, xa_wk, xa_wv, xa_wo, ln2_g, ln2_b, ffn_w_up, ffn_conv_w, ffn_conv_b, ffn_w_down, ln3_g, ln3_b):
    B, S, _ = x.shape
    mem_len = mem.shape[1]
    bf16 = jnp.bfloat16
    x2d = x.reshape(B * S, D_MODEL)
    for l in range(DEPTH):
        w = w_in[l]
        w_rwkv = w[:, :RWKV_COLS].astype(bf16)
        w_nsa = _nsa_weight(w[:, RWKV_COLS:RWKV_COLS + NSA_COLS])
        w_gate = w[:, RWKV_COLS + NSA_COLS:].astype(bf16)
        y_a = _rwkv_time_mix(x2d, w_rwkv, rwkv_mu[l], rwkv_w0[l], rwkv_w2[l], rwkv_a0[l], rwkv_a2[l],
                             rwkv_k_k[l], rwkv_k_a[l], rwkv_r_k[l], rwkv_gn_g[l], rwkv_gn_b[l], B, S)
        y_b = _nsa_branch(x2d, w_nsa, nsa_pe_k[l], nsa_pe_v[l], nsa_ck_w1[l], nsa_ck_w2[l],
                          nsa_cv_w1[l], nsa_cv_w2[l], B, S)
        x2d = _merge(x2d, y_a, y_b, w_gate, merge_p_a[l], merge_p_b[l], mix_w_o[l], ln1_g[l], ln1_b[l])
        k_mem, v_mem = _mem_kv(mem.reshape(B * mem_len, D_MODEL), xa_wk[l], xa_wv[l])
        x2d = _xattn(x2d, k_mem, v_mem, xa_wq[l], xa_wo[l], ln2_g[l], ln2_b[l], S, mem_len)
        x2d = _ffn(x2d, ffn_w_up[l], ffn_conv_w[l], ffn_conv_b[l], ffn_w_down[l], ln3_g[l], ln3_b[l], S)
    return x2d.reshape(B, S, D_MODEL)
```

```python
import functools
import math

import jax
import jax.numpy as jnp
import numpy as np
from jax import lax
from jax.experimental import pallas as pl
from jax.experimental.pallas import tpu as pltpu

D_MODEL = 1024
HEAD_DIM = 64
RWKV_DIM = 512
RWKV_HEADS = 8
LORA = 64
RWKV_COLS = 3 * RWKV_DIM + 2 * LORA
GN_EPS = 64e-5
NSA_DIM = 512
NSA_Q_HEADS = 8
NSA_GROUPS = 2
NSA_R = NSA_Q_HEADS // NSA_GROUPS
NSA_KV_DIM = NSA_GROUPS * HEAD_DIM
NSA_COLS = NSA_DIM + 6 * NSA_KV_DIM + 3 * NSA_Q_HEADS
NSA_COLS_PAD = 1408
CMP_BLOCK = 32
CMP_STRIDE = 16
CMP_HIDDEN = 256
SEL_BLOCK = 64
N_SELECT = 16
WINDOW = 512
Q_BLOCK = 128
ROPE_THETA = 500000.0
ROPE_DIM = 16
X_HEADS = 4
X_HEAD_DIM = 256
D_FF = 2816
LN_EPS = 1e-5
DEPTH = 1
ALPHA = (2 * DEPTH) ** 0.25
NEG = -1e30
BIG = 1e30
LOG2E = math.log2(math.e)

V7X_LANES = 128
V7X_SUBLANES = 8
V7X_VMEM_LIMIT_BYTES = 56 * 1024 * 1024

HI = lax.Precision.HIGHEST
RWKV_CHUNK = 64


def _params(*sem):
    return pltpu.CompilerParams(dimension_semantics=sem, vmem_limit_bytes=V7X_VMEM_LIMIT_BYTES)


def _full(shape):
    n = len(shape)
    return pl.BlockSpec(shape, lambda *_: (0,) * n)


def _head_ones(width):
    r = lax.broadcasted_iota(jnp.int32, (width, width), 0) // HEAD_DIM
    c = lax.broadcasted_iota(jnp.int32, (width, width), 1) // HEAD_DIM
    return (r == c).astype(jnp.float32)


def _layer_norm(y, g, b):
    mu = jnp.mean(y, axis=-1, keepdims=True)
    d = y - mu
    var = jnp.mean(d * d, axis=-1, keepdims=True)
    return d * lax.rsqrt(var + LN_EPS) * g + b


def _rwkv_prep_kernel(seq_tiles, x_ref, xp_ref, w_ref, mu_ref, w0_ref, w2_ref, a0_ref, a2_ref,
                      kk_ref, ka_ref, rk_ref,
                      r_out, k_out, v_out, lw_out, a_out, b_out, bonus_out):
    i = pl.program_id(0)
    tm = x_ref.shape[0]
    C = RWKV_DIM
    w = w_ref[...]
    p = jnp.dot(x_ref[...].astype(jnp.bfloat16), w, preferred_element_type=jnp.float32)
    xprev = xp_ref[...].astype(jnp.bfloat16)
    pprev = jnp.dot(xprev, w, preferred_element_type=jnp.float32)[V7X_SUBLANES - 1:V7X_SUBLANES, :]
    pprev = jnp.where(i % seq_tiles == 0, 0.0, pprev)
    row = lax.broadcasted_iota(jnp.int32, (tm, 1), 0)
    shifted = jnp.where(row == 0, pprev, pltpu.roll(p, 1, 0))
    p = p + (shifted - p) * mu_ref[...]
    r, k, v = p[:, :C], p[:, C:2 * C], p[:, 2 * C:3 * C]
    wl = p[:, 3 * C:3 * C + LORA]
    al = p[:, 3 * C + LORA:]
    z = -(w0_ref[...] + jnp.dot(jnp.tanh(wl).astype(jnp.bfloat16), w2_ref[...],
                                preferred_element_type=jnp.float32))
    softplus = jnp.maximum(z, 0.0) + jnp.log(1.0 + jnp.exp(-jnp.abs(z)))
    w_log = -softplus - 0.5
    lw_out[...] = -jnp.exp(w_log)
    a = jax.nn.sigmoid(a0_ref[...] + jnp.dot(al.astype(jnp.bfloat16), a2_ref[...],
                                             preferred_element_type=jnp.float32))
    ones = _head_ones(C).astype(jnp.bfloat16)
    ones2 = jnp.concatenate([ones, ones], axis=0)

    def head_sum(t):
        return jnp.dot(jnp.concatenate(_split(t), axis=1), ones2, preferred_element_type=jnp.float32)

    kk = k * kk_ref[...]
    kk = kk / jnp.maximum(jnp.sqrt(head_sum(kk * kk)), 1e-12)
    kmod = k * (1.0 + (a - 1.0) * ka_ref[...])
    bonus = head_sum(r * kmod * rk_ref[...]) * v
    r_out[...] = r
    k_out[...] = kmod
    v_out[...] = v
    a_out[...] = -kk
    b_out[...] = kk * a
    bonus_out[...] = bonus


def _rwkv_prep(x2d, w_rwkv, mu, w0, w2, a0, a2, k_k, k_a, r_k, seq_len, tm=512):
    M = x2d.shape[0]
    C = RWKV_DIM
    seq_tiles = seq_len // tm
    row = lambda a: a.reshape(1, -1)
    out = jax.ShapeDtypeStruct((M, C), jnp.float32)
    tile = pl.BlockSpec((tm, C), lambda i: (i, 0))
    blocks_per_tile = tm // V7X_SUBLANES
    return pl.pallas_call(
        functools.partial(_rwkv_prep_kernel, seq_tiles),
        grid=(M // tm,),
        in_specs=[
            pl.BlockSpec((tm, D_MODEL), lambda i: (i, 0)),
            pl.BlockSpec((V7X_SUBLANES, D_MODEL), lambda i: (jnp.maximum(i * blocks_per_tile - 1, 0), 0)),
            _full((D_MODEL, RWKV_COLS)), _full((1, RWKV_COLS)), _full((1, C)), _full((LORA, C)),
            _full((1, C)), _full((LORA, C)), _full((1, C)), _full((1, C)), _full((1, C)),
        ],
        out_specs=[tile] * 7,
        out_shape=[out] * 7,
        compiler_params=_params("parallel"),
        name="rwkv_prep",
    )(x2d, x2d, w_rwkv, row(mu), row(w0), w2.astype(jnp.bfloat16), row(a0), a2.astype(jnp.bfloat16),
      row(k_k), row(k_a), row(r_k))


RWKV_GROUP = 4


def _split(x):
    hi = x.astype(jnp.bfloat16)
    return hi, (x - hi.astype(jnp.float32)).astype(jnp.bfloat16)


def _mm(a, b):
    return jnp.dot(a.astype(jnp.bfloat16), b.astype(jnp.bfloat16), preferred_element_type=jnp.float32)


def _rwkv_mix_kernel(r_ref, k_ref, v_ref, lw_ref, a_ref, b_ref, bonus_ref, gng_ref, gnb_ref,
                     y_out, state):
    T, N, GH = RWKV_CHUNK, HEAD_DIM, RWKV_GROUP
    W = GH * N
    f32 = jnp.float32

    @pl.when(pl.program_id(1) == 0)
    def _():
        state[...] = jnp.zeros_like(state)

    ri = lax.broadcasted_iota(jnp.int32, (W, W), 0)
    ci = lax.broadcasted_iota(jnp.int32, (W, W), 1)
    same_head = ri // N == ci // N
    strict, incl = ri > ci, ri >= ci
    eye = (ri == ci).astype(f32)
    tri = (lax.broadcasted_iota(jnp.int32, (T, T), 0)
           >= lax.broadcasted_iota(jnp.int32, (T, T), 1)).astype(jnp.bfloat16)
    mean_w = (_head_ones(RWKV_DIM) * (1.0 / N)).astype(jnp.bfloat16)

    def expand(x):
        return jnp.where(same_head, jnp.concatenate([x] * GH, axis=0), jnp.zeros((), x.dtype))

    def collapse(x):
        out = x[:T]
        for h in range(1, GH):
            out = out + x[h * T:(h + 1) * T]
        return out

    bf = lambda t: t.astype(jnp.bfloat16)
    n_chunks = r_ref.shape[0] // T
    n_groups = RWKV_HEADS // GH
    chains = [(ch, g) for ch in range(n_chunks) for g in range(n_groups)]
    each = lambda fn, *lists: [fn(*args) for args in zip(*lists)]

    def cumsum(x):
        hi, lo = _split(x)
        lo2 = (x - hi.astype(f32) - lo.astype(f32)).astype(jnp.bfloat16)
        return jnp.dot(jnp.concatenate([tri] * 3, axis=1), jnp.concatenate([hi, lo, lo2], axis=0),
                       preferred_element_type=f32)

    c_all = [cumsum(lw_ref[ch * T:(ch + 1) * T, :]) for ch in range(n_chunks)]

    def operands(ch, g):
        rows, cols = slice(ch * T, (ch + 1) * T), slice(g * W, (g + 1) * W)
        lw, c = lw_ref[rows, cols], c_all[ch][:, cols]
        r, k, v = r_ref[rows, cols], k_ref[rows, cols], v_ref[rows, cols]
        a, b = a_ref[rows, cols], b_ref[rows, cols]
        c_last = c[T - 1:T, :]
        e_neg, e_end = jnp.exp(-c), jnp.exp(c_last - c)
        r_t = r * jnp.exp(c)
        return dict(a_x=expand(bf(a * jnp.exp(c - lw))), r_x=expand(bf(r_t)), b_x=expand(bf(b * e_neg)),
                    k_x=expand(bf(k * e_neg)), v_x=expand(bf(v)), r_t=r_t, b_end=b * e_end, k_end=k * e_end,
                    w_end=jnp.exp(c_last))

    ops = [operands(ch, g) for ch, g in chains]
    P = [lax.dot_general(jnp.concatenate([o["a_x"], o["r_x"]], axis=0),
                         jnp.concatenate([o["b_x"], o["k_x"]], axis=0),
                         (((1,), (1,)), ((), ())), preferred_element_type=f32) for o in ops]
    L_ab = [jnp.where(strict, p[:W, :W], 0.0) for p in P]
    L_ak = [bf(jnp.where(strict, p[:W, W:], 0.0)) for p in P]
    M_rb = [bf(jnp.where(incl, p[W:, :W], 0.0)) for p in P]
    M_rk = [bf(jnp.where(incl, p[W:, W:], 0.0)) for p in P]
    v_x = [o["v_x"] for o in ops]

    base = V7X_SUBLANES
    D = [bf(jnp.where(ri // base == ci // base, l, 0.0)) for l in L_ab]
    D2 = each(_mm, D, D)
    D4 = each(_mm, D2, D2)
    X = each(_mm, each(lambda d, d2: _mm(eye + d.astype(f32), eye + d2), D, D2), [eye + d4 for d4 in D4])
    def second_rows(x, blk):
        return jnp.concatenate([x[s:s + blk] for s in range(blk, W, 2 * blk)], axis=0)

    def merge_rows(x, u, blk):
        pieces = []
        for i, s in enumerate(range(0, W, 2 * blk)):
            pieces += [x[s:s + blk], x[s + blk:s + 2 * blk] + u[i * blk:(i + 1) * blk]]
        return jnp.concatenate(pieces, axis=0)

    blk = base
    while blk < T:
        pair = (ri // (2 * blk) == ci // (2 * blk)) & (ri // blk != ci // blk)
        Xb = [bf(x) for x in X]
        lower = [bf(second_rows(x, blk)) for x in X]
        step = each(_mm, each(_mm, lower, [jnp.where(pair, l, 0.0) for l in L_ab]), Xb)
        X = [merge_rows(x, u, blk) for x, u in zip(X, step)]
        blk *= 2

    lakv = each(_mm, L_ak, v_x)
    AV = [bf(_mm(x, jnp.concatenate([o["a_x"], bf(t)], axis=1))) for x, o, t in zip(X, ops, lakv)]
    ry = each(_mm, M_rb, AV)
    mrkv = each(_mm, M_rk, v_x)
    gh = [_mm(expand(o["b_end"]).T, av) for o, av in zip(ops, AV)]
    kv = [_mm(expand(o["k_end"]).T, vx) for o, vx in zip(ops, v_x)]
    rp = [bf(expand(o["r_t"]) + t[:, :W]) for o, t in zip(ops, ry)]
    yp = [t[:, W:] + u for t, u in zip(ry, mrkv)]
    G = [bf(eye * o["w_end"] + t[:, :W]) for o, t in zip(ops, gh)]
    H = [t[:, W:] + u for t, u in zip(gh, kv)]

    y_rows = []
    for ch in range(n_chunks):
        ys = []
        for g in range(n_groups):
            i = chains.index((ch, g))
            s0 = bf(state[g])
            ys.append(collapse(_mm(rp[i], s0) + yp[i]))
            state[g] = _mm(G[i], s0) + H[i]
        y_rows.append(jnp.concatenate(ys, axis=1))
    y = jnp.concatenate(y_rows, axis=0)
    y_hi, y_lo = _split(y)
    ym = jnp.dot(jnp.concatenate([y_hi, y_lo], axis=1), jnp.concatenate([mean_w, mean_w], axis=0),
                 preferred_element_type=f32)
    d = y - ym
    yv = _mm(d * d, mean_w)
    y_out[...] = d * lax.rsqrt(yv + GN_EPS) * gng_ref[...] + gnb_ref[...] + bonus_ref[...]


def _rwkv_mix(r, k, v, lw, a, b, bonus, gn_g, gn_b, batch, chunks_per_step=4):
    M, C = r.shape
    rows = RWKV_CHUNK * chunks_per_step
    steps = M // rows // batch
    W = RWKV_GROUP * HEAD_DIM
    tile = pl.BlockSpec((rows, C), lambda bi, ci: (bi * steps + ci, 0))
    return pl.pallas_call(
        _rwkv_mix_kernel,
        grid=(batch, steps),
        in_specs=[tile] * 7 + [_full((1, C)), _full((1, C))],
        out_specs=tile,
        out_shape=jax.ShapeDtypeStruct((M, C), jnp.float32),
        scratch_shapes=[pltpu.VMEM((RWKV_HEADS // RWKV_GROUP, W, W), jnp.float32)],
        compiler_params=_params("parallel", "arbitrary"),
        name="rwkv_mix",
    )(r, k, v, lw, a, b, bonus, gn_g.reshape(1, C), gn_b.reshape(1, C))


def _rwkv_time_mix(x2d, w_rwkv, mu, w0, w2, a0, a2, k_k, k_a, r_k, gn_g, gn_b, batch, seq_len):
    r, k, v, lw, a, b, bonus = _rwkv_prep(x2d, w_rwkv, mu, w0, w2, a0, a2, k_k, k_a, r_k, seq_len)
    return _rwkv_mix(r, k, v, lw, a, b, bonus, gn_g, gn_b, batch)


def _rope_tables(seq_len):
    half = ROPE_DIM // 2
    inv = ROPE_THETA ** (-np.arange(half, dtype=np.float64) * 2.0 / ROPE_DIM)
    ang = np.arange(seq_len, dtype=np.float64)[:, None] * inv[None, :]
    cos, sin = np.cos(ang).astype(np.float32), np.sin(ang).astype(np.float32)
    pad = np.zeros((seq_len, HEAD_DIM - ROPE_DIM), np.float32)
    zero = np.zeros_like(sin)
    c = np.concatenate([cos, cos, pad + 1.0], axis=1)
    s_lo = np.concatenate([-sin, zero, pad], axis=1)
    s_hi = np.concatenate([zero, sin, pad], axis=1)
    two = lambda t: jnp.asarray(np.concatenate([t, t], axis=1))
    return two(c), two(s_lo), two(s_hi)


def _rope_pair(x, c, s_lo, s_hi):
    return x * c + pltpu.roll(x, V7X_LANES - ROPE_DIM // 2, 1) * s_lo + pltpu.roll(x, ROPE_DIM // 2, 1) * s_hi


GATE_SLOTS = V7X_SUBLANES
V_ROWS = HEAD_DIM + 2 * V7X_SUBLANES


def _nsa_prep_kernel(x_ref, w_ref, c_ref, slo_ref, shi_ref,
                     qT_out, kc_out, vc_out, ks_out, vsT_out, kw_out, vwT_out, gT_out):
    f32 = jnp.float32
    L, N, G, R, Q = V7X_LANES, HEAD_DIM, NSA_GROUPS, NSA_R, Q_BLOCK
    tm = x_ref.shape[0]
    p = jnp.dot(x_ref[...].astype(jnp.bfloat16), w_ref[...], preferred_element_type=f32)
    c, s_lo, s_hi = c_ref[...], slo_ref[...], shi_ref[...]
    rope = lambda t: _rope_pair(t, c, s_lo, s_hi)
    kv = lambda i: p[:, NSA_DIM + i * L:NSA_DIM + (i + 1) * L]

    qT = [(rope(p[:, j * L:(j + 1) * L]) * (N ** -0.5 * LOG2E)).T for j in range(NSA_DIM // L)]
    gT = jax.nn.sigmoid(kv(6)).T
    for g in range(G):
        for qb in range(tm // Q):
            blk = slice(qb * Q, (qb + 1) * Q)
            heads = [qT[(g * R + r) // 2][((g * R + r) % 2) * N:((g * R + r) % 2 + 1) * N, blk]
                     for r in range(R)]
            qT_out[0, g, qb] = jnp.concatenate(heads, axis=1).astype(qT_out.dtype)
            gates = [gT[(g * R + r) * GATE_SLOTS:(g * R + r + 1) * GATE_SLOTS, blk] for r in range(R)]
            gT_out[0, g, qb] = jnp.concatenate(gates, axis=1)

    k_c, k_s, k_w = rope(kv(0)), rope(kv(2)), rope(kv(4))
    v_c = kv(1)
    v_sT, v_wT = kv(3).T, kv(5).T
    slot = (lax.broadcasted_iota(jnp.int32, (tm, L - N), 0) // SEL_BLOCK) % SEL_SLOTS
    onehot = (slot == lax.broadcasted_iota(jnp.int32, (tm, L - N), 1)).astype(f32)
    for g in range(G):
        cols = slice(g * N, (g + 1) * N)
        kc_out[0, g] = k_c[:, cols]
        vc_out[0, g] = v_c[:, cols]
        ks_out[0, g] = jnp.concatenate([k_s[:, cols], onehot], axis=1).astype(ks_out.dtype)
        kw_out[0, g] = k_w[:, cols].astype(kw_out.dtype)
        extra = (lax.broadcasted_iota(jnp.int32, (V_ROWS - N, tm), 0) == 0).astype(f32)
        vsT_out[0, g] = jnp.concatenate([v_sT[cols, :], extra], axis=0).astype(vsT_out.dtype)
        vwT_out[0, g] = jnp.concatenate([v_wT[cols, :], extra], axis=0).astype(vwT_out.dtype)


def _nsa_weight(w):
    n_qkv = NSA_DIM + 6 * NSA_KV_DIM
    gates = w[:, n_qkv:].reshape(-1, NSA_Q_HEADS, 3)
    gates = jnp.pad(gates, ((0, 0), (0, 0), (0, GATE_SLOTS - 3))).reshape(-1, NSA_Q_HEADS * GATE_SLOTS)
    gates = jnp.pad(gates, ((0, 0), (0, NSA_COLS_PAD - n_qkv - NSA_Q_HEADS * GATE_SLOTS)))
    return jnp.concatenate([w[:, :n_qkv], gates], axis=1).astype(jnp.bfloat16)


def _nsa_prep(x2d, w_nsa, batch, seq_len, tm=512):
    B, S, G, N, L, Q = batch, seq_len, NSA_GROUPS, HEAD_DIM, V7X_LANES, Q_BLOCK
    RQ = NSA_R * Q
    tiles = S // tm
    tabs = _rope_tables(S)
    tab_spec = pl.BlockSpec((tm, L), lambda i: (i % tiles, 0))
    f32, bf16 = jnp.float32, jnp.bfloat16
    per_q = lambda rows: pl.BlockSpec((1, G, tm // Q, rows, RQ), lambda i: (i // tiles, 0, i % tiles, 0, 0))
    by_row = lambda w: pl.BlockSpec((1, G, tm, w), lambda i: (i // tiles, 0, i % tiles, 0))
    by_col = pl.BlockSpec((1, G, V_ROWS, tm), lambda i: (i // tiles, 0, 0, i % tiles))
    sds = jax.ShapeDtypeStruct
    return pl.pallas_call(
        _nsa_prep_kernel,
        grid=(B * tiles,),
        in_specs=[pl.BlockSpec((tm, D_MODEL), lambda i: (i, 0)), _full((D_MODEL, NSA_COLS_PAD)),
                  tab_spec, tab_spec, tab_spec],
        out_specs=[per_q(N), by_row(N), by_row(N), by_row(L), by_col, by_row(N), by_col, per_q(GATE_SLOTS)],
        out_shape=[sds((B, G, S // Q, N, RQ), bf16), sds((B, G, S, N), f32), sds((B, G, S, N), f32),
                   sds((B, G, S, L), bf16), sds((B, G, V_ROWS, S), bf16), sds((B, G, S, N), bf16),
                   sds((B, G, V_ROWS, S), bf16), sds((B, G, S // Q, GATE_SLOTS, RQ), f32)],
        compiler_params=_params("parallel"),
        name="nsa_prep",
    )(x2d, w_nsa, *tabs)


def _gelu_tanh(x):
    return 0.5 * x * (1.0 + jnp.tanh(math.sqrt(2.0 / math.pi) * (x + 0.044715 * x * x * x)))


def _nsa_compress_kernel(xk_ref, xv_ref, pek_ref, pev_ref, kw1_ref, kw2_ref, vw1_ref, vw2_ref,
                         kc_out, vcT_out):
    N, bf16, f32 = HEAD_DIM, jnp.bfloat16, jnp.float32
    n_blocks = kc_out.shape[2]

    def mlp(x_ref, pe_ref, w1_ref, w2_ref):
        lo = jnp.zeros((n_blocks, CMP_HIDDEN), f32)
        hi = jnp.zeros((n_blocks, CMP_HIDDEN), f32)
        for l in range(CMP_STRIDE):
            rows = x_ref[0, 0, pl.ds(l, n_blocks, stride=CMP_STRIDE), :]
            lo = lo + jnp.dot((rows + pe_ref[l:l + 1, :]).astype(bf16), w1_ref[l * N:(l + 1) * N, :],
                              preferred_element_type=f32)
            m = CMP_STRIDE + l
            hi = hi + jnp.dot((rows + pe_ref[m:m + 1, :]).astype(bf16), w1_ref[m * N:(m + 1) * N, :],
                              preferred_element_type=f32)
        pre = lo + pltpu.roll(hi, n_blocks - 1, 0)
        return jnp.dot(_gelu_tanh(pre).astype(bf16), w2_ref[...], preferred_element_type=f32)

    kc_out[0, 0] = mlp(xk_ref, pek_ref, kw1_ref, kw2_ref).astype(kc_out.dtype)
    vc = mlp(xv_ref, pev_ref, vw1_ref, vw2_ref)
    vcT_out[0, 0] = jnp.concatenate([vc, jnp.zeros_like(vc)], axis=1).T[:N].astype(vcT_out.dtype)


def _nsa_compress(xk, xv, pe_k, pe_v, ck_w1, ck_w2, cv_w1, cv_w2):
    B, G, S, N = xk.shape
    NC = S // CMP_STRIDE
    bf16 = jnp.bfloat16
    xin = pl.BlockSpec((1, 1, S, N), lambda b, g: (b, g, 0, 0))
    return pl.pallas_call(
        _nsa_compress_kernel,
        grid=(B, G),
        in_specs=[xin, xin, _full((CMP_BLOCK, N)), _full((CMP_BLOCK, N)), _full((CMP_BLOCK * N, CMP_HIDDEN)),
                  _full((CMP_HIDDEN, N)), _full((CMP_BLOCK * N, CMP_HIDDEN)), _full((CMP_HIDDEN, N))],
        out_specs=[pl.BlockSpec((1, 1, NC, N), lambda b, g: (b, g, 0, 0)),
                   pl.BlockSpec((1, 1, N, NC), lambda b, g: (b, g, 0, 0))],
        out_shape=[jax.ShapeDtypeStruct((B, G, NC, N), bf16), jax.ShapeDtypeStruct((B, G, N, NC), bf16)],
        compiler_params=_params("parallel", "parallel"),
        name="nsa_compress",
    )(xk, xv, pe_k, pe_v, ck_w1.astype(bf16), ck_w2.astype(bf16), cv_w1.astype(bf16), cv_w2.astype(bf16))


SEL_KEYS = 512
SEL_SLOTS = V7X_SUBLANES
CMP_TILE = 128


def _nsa_attn_kernel(qT_ref, kc_ref, vcT_ref, ovT_ref, ks_ref, vsT_ref, kw_ref, vwT_ref, gT_ref,
                     o_ref, val_ref, cnt_ref, sel_ref, qa_ref, s_ref, cm_ref, m_ref, acc_ref, part_ref,
                     imp_ref):
    qb = pl.program_id(2)
    t0 = qb * Q_BLOCK
    R, Q = NSA_R, Q_BLOCK
    RQ = R * Q
    f32, bf16 = jnp.float32, jnp.bfloat16
    qT = qT_ref[0, 0, 0]
    lane = lax.broadcasted_iota(jnp.int32, (1, RQ), 1)
    tpos = t0 + lane % Q
    mm = lambda a, b: jnp.dot(a, b, preferred_element_type=f32)

    g = gT_ref[0, 0, 0]
    NC = kc_ref.shape[2]
    NSB = ovT_ref.shape[0]

    span = WINDOW + Q
    sub = lax.broadcasted_iota(jnp.int32, (Q, 1), 0)
    qpos = lane % Q

    def local_branches(rows, aligned_window):
        w0 = pl.multiple_of(t0 - WINDOW, Q) if aligned_window else 0
        s_cmp = mm(kc_ref[0, 0, :rows, :], qT)
        s_win = mm(kw_ref[0, 0, pl.ds(w0, span), :], qT)

        cend = lax.broadcasted_iota(jnp.int32, (rows, 1), 0) * CMP_STRIDE + (CMP_BLOCK - 1)
        mask = cend <= tpos
        sm = jnp.where(mask, s_cmp, NEG)
        e = jnp.where(mask, jnp.exp2(sm - jnp.max(sm, axis=0, keepdims=True)), 0.0)
        den = jnp.sum(e, axis=0, keepdims=True)
        p = e / jnp.where(den > 0.0, den, 1.0)
        o_c = mm(vcT_ref[0, 0, :, :rows], p.astype(bf16))
        psum = p[:, :Q]
        for r in range(1, R):
            psum = psum + p[:, r * Q:(r + 1) * Q]
        ps_hi, ps_lo = _split(psum)
        ov = ovT_ref[:, :rows]
        imp_ref[...] = mm(jnp.concatenate([ov, ov], axis=1), jnp.concatenate([ps_hi, ps_lo], axis=0))

        if aligned_window:
            head = jnp.where(sub > qpos, s_win[:Q], NEG)
            tail = jnp.where(sub <= qpos, s_win[WINDOW:], NEG)
            sm = jnp.concatenate([head, s_win[Q:WINDOW], tail], axis=0)
        else:
            diff = tpos - lax.broadcasted_iota(jnp.int32, (span, 1), 0)
            sm = jnp.where((diff >= 0) & (diff < WINDOW), s_win, NEG)
        p = jnp.exp2(sm - jnp.max(sm, axis=0, keepdims=True))
        pv = mm(vwT_ref[0, 0, :, pl.ds(w0, span)], p.astype(bf16))
        o_w = pv[:HEAD_DIM] / pv[HEAD_DIM:HEAD_DIM + 1]
        part_ref[...] = g[0:1, :] * o_c + g[2:3, :] * o_w

    tile = min(CMP_TILE, NC)
    per_tile = tile * CMP_STRIDE // Q
    n_tiles = NC // tile
    first_aligned = WINDOW // Q
    for i in range(n_tiles):
        lo, hi = i * per_tile, (i + 1) * per_tile
        cuts = [lo, hi] if not lo < first_aligned < hi else [lo, first_aligned, hi]
        for a, b in zip(cuts[:-1], cuts[1:]):
            in_range = (qb >= a) if (i == n_tiles - 1 and b == hi) else ((qb >= a) & (qb < b))
            pl.when(in_range)(functools.partial(local_branches, (i + 1) * tile, a >= first_aligned))
    imp = imp_ref[...]

    tq = t0 + lax.broadcasted_iota(jnp.int32, (1, Q), 1)
    jblk = lax.broadcasted_iota(jnp.int32, (NSB, 1), 0)
    cur = tq // SEL_BLOCK
    forced = (jblk == 0) | (jblk == cur) | (jblk == cur - 1)
    valid = jblk * SEL_BLOCK <= tq
    val = jnp.where(valid, jnp.where(forced, BIG, imp), NEG)
    val_ref[...] = val
    n_live = (t0 + Q - 1) // SEL_BLOCK + 1

    def strict_body(u, cnt):
        for i in (2 * u, 2 * u + 1):
            cnt = cnt + jnp.where(val_ref[pl.ds(i, 1), :] > val, 1, 0)
        return cnt

    cnt_ref[...] = lax.fori_loop(0, n_live // 2, strict_body, jnp.zeros((NSB, Q), jnp.int32))
    taken = jnp.sum(jnp.where(valid & (cnt_ref[...] < N_SELECT), 1, 0), axis=0, keepdims=True)

    @pl.when(jnp.max(taken) > N_SELECT)
    def _():
        def tie_body(i, cnt):
            row = val_ref[pl.ds(i, 1), :]
            ge = jnp.where(row >= val, 1, 0)
            gt = jnp.where(row > val, 1, 0)
            return cnt + jnp.where(jblk > i, ge, gt)

        cnt_ref[...] = lax.fori_loop(0, n_live, tie_body, jnp.zeros((NSB, Q), jnp.int32))

    bias = jnp.where(cnt_ref[...] < N_SELECT, 0.0, NEG)
    sel_ref[...] = jnp.concatenate([bias] * R, axis=1)

    bps = SEL_SLOTS
    for buf in range(qa_ref.shape[0]):
        qa_ref[buf, :HEAD_DIM, :] = qT
        qa_ref[buf, HEAD_DIM:, :] = jnp.zeros((qa_ref.shape[1] - HEAD_DIM, RQ), bf16)
    m_ref[...] = jnp.full(m_ref.shape, NEG, f32)
    acc_ref[...] = jnp.zeros(acc_ref.shape, f32)
    krow = lax.broadcasted_iota(jnp.int32, (SEL_KEYS, 1), 0)

    def scores(kc, slot):
        k0 = pl.multiple_of(kc * SEL_KEYS, SEL_KEYS)
        grp = pl.multiple_of((kc * (SEL_KEYS // SEL_BLOCK)) // bps * bps, bps)
        rows = sel_ref[pl.ds(grp, bps), :]
        qa_ref[slot, HEAD_DIM:HEAD_DIM + 2 * bps, :] = (
            jnp.concatenate([rows, jnp.zeros_like(rows)], axis=0).astype(bf16))
        s = mm(ks_ref[0, 0, pl.ds(k0, SEL_KEYS), :], qa_ref[slot])
        s_ref[slot] = s
        cm_ref[slot] = jnp.max(s, axis=0, keepdims=True)

    def softmax(kc, slot, diagonal):
        s, cm = s_ref[slot], cm_ref[slot]
        if diagonal:
            s = jnp.where(pl.multiple_of(kc * SEL_KEYS, SEL_KEYS) + krow <= tpos, s, NEG)
            cm = jnp.max(s, axis=0, keepdims=True)
        m = m_ref[...]
        m_new = jnp.maximum(m, cm)
        m_ref[...] = m_new
        return jnp.exp2(s - m_new).astype(bf16), jnp.exp2(m - m_new)

    def values(kc, p):
        return mm(vsT_ref[0, 0, :, pl.ds(pl.multiple_of(kc * SEL_KEYS, SEL_KEYS), SEL_KEYS)], p)

    def fold(alpha, pv):
        acc_ref[...] = alpha * acc_ref[...] + pv

    def accumulate(kc, slot, diagonal):
        p, alpha = softmax(kc, slot, diagonal)
        fold(alpha, values(kc, p))

    last = (t0 + Q - 1) // SEL_KEYS
    scores(0, 0)

    def sel_body(j, carry):
        scores(2 * j + 1, 1)
        accumulate(2 * j, 0, False)
        scores(2 * j + 2, 0)
        accumulate(2 * j + 1, 1, False)
        return carry

    lax.fori_loop(0, last // 2, sel_body, 0)

    @pl.when(last % 2 == 1)
    def _():
        scores(last, 1)
        accumulate(last - 1, 0, False)
        accumulate(last, 1, True)

    @pl.when(last % 2 == 0)
    def _():
        accumulate(last, 0, True)

    o_s = acc_ref[:HEAD_DIM, :] / acc_ref[HEAD_DIM:HEAD_DIM + 1, :]
    oT = part_ref[...] + g[1:2, :] * o_s
    pairs = [jnp.concatenate([oT[:, (2 * i) * Q:(2 * i + 1) * Q], oT[:, (2 * i + 1) * Q:(2 * i + 2) * Q]],
                             axis=0).T for i in range(R // 2)]
    o_ref[...] = jnp.concatenate(pairs, axis=1)


def _nsa_attention(qT, kc, vcT, ks, vsT, kw, vwT, gT):
    B, G, NQB, _, RQ = qT.shape
    S = ks.shape[2]
    NC = kc.shape[2]
    NSB = S // SEL_BLOCK
    c = np.arange(NC)[None, :] * CMP_STRIDE
    j = np.arange(NSB)[:, None] * SEL_BLOCK
    ovT = jnp.asarray((c <= j + SEL_BLOCK - 1) & (c + CMP_BLOCK - 1 >= j), jnp.bfloat16)
    per_q = lambda rows: pl.BlockSpec((1, 1, 1, rows, RQ), lambda b, g, q: (b, g, q, 0, 0))
    per_g = lambda d0, d1: pl.BlockSpec((1, 1, d0, d1), lambda b, g, q: (b, g, 0, 0))
    return pl.pallas_call(
        _nsa_attn_kernel,
        grid=(B, G, NQB),
        in_specs=[per_q(HEAD_DIM), per_g(NC, HEAD_DIM), per_g(HEAD_DIM, NC), _full((NSB, NC)),
                  per_g(S, V7X_LANES), per_g(V_ROWS, S), per_g(S, HEAD_DIM), per_g(V_ROWS, S),
                  per_q(V7X_SUBLANES)],
        out_specs=pl.BlockSpec((Q_BLOCK, NSA_R * HEAD_DIM), lambda b, g, q: (b * NQB + q, g)),
        out_shape=jax.ShapeDtypeStruct((B * S, NSA_DIM), jnp.float32),
        scratch_shapes=[pltpu.VMEM((NSB, Q_BLOCK), jnp.float32), pltpu.VMEM((NSB, Q_BLOCK), jnp.int32),
                        pltpu.VMEM((NSB, RQ), jnp.float32),
                        pltpu.VMEM((2, V7X_LANES, RQ), jnp.bfloat16),
                        pltpu.VMEM((2, SEL_KEYS, RQ), jnp.float32), pltpu.VMEM((2, 1, RQ), jnp.float32),
                        pltpu.VMEM((1, RQ), jnp.float32),
                        pltpu.VMEM((V_ROWS, RQ), jnp.float32), pltpu.VMEM((HEAD_DIM, RQ), jnp.float32),
                        pltpu.VMEM((NSB, Q_BLOCK), jnp.float32)],
        compiler_params=_params("parallel", "parallel", "arbitrary"),
        name="nsa_attention",
    )(qT, kc, vcT, ovT, ks, vsT, kw, vwT, gT)


def _nsa_branch(x2d, w_nsa, pe_k, pe_v, ck_w1, ck_w2, cv_w1, cv_w2, batch, seq_len):
    qT, kc_in, vc_in, ks, vsT, kw, vwT, gT = _nsa_prep(x2d, w_nsa, batch, seq_len)
    kc, vcT = _nsa_compress(kc_in, vc_in, pe_k, pe_v, ck_w1, ck_w2, cv_w1, cv_w2)
    return _nsa_attention(qT, kc, vcT, ks, vsT, kw, vwT, gT)


def _merge_kernel(x_ref, ya_ref, yb_ref, wg_ref, pa_ref, pb_ref, wo_ref, g_ref, b_ref, o_ref):
    bf16 = jnp.bfloat16
    mm = lambda a, w: jnp.dot(a.astype(bf16), w, preferred_element_type=jnp.float32)
    x = x_ref[...]
    gates = jax.nn.sigmoid(mm(x, wg_ref[...]))
    mixed = gates[:, :D_MODEL] * mm(ya_ref[...], pa_ref[...]) + gates[:, D_MODEL:] * mm(yb_ref[...], pb_ref[...])
    o_ref[...] = _layer_norm(ALPHA * x + mm(mixed, wo_ref[...]), g_ref[...], b_ref[...])


def _merge(x2d, y_a, y_b, w_gate, p_a, p_b, w_o, ln_g, ln_b, tm=512):
    M = x2d.shape[0]
    bf16 = jnp.bfloat16
    rows = lambda w: pl.BlockSpec((tm, w), lambda i: (i, 0))
    return pl.pallas_call(
        _merge_kernel,
        grid=(M // tm,),
        in_specs=[rows(D_MODEL), rows(RWKV_DIM), rows(NSA_DIM), _full((D_MODEL, 2 * D_MODEL)),
                  _full((RWKV_DIM, D_MODEL)), _full((NSA_DIM, D_MODEL)), _full((D_MODEL, D_MODEL)),
                  _full((1, D_MODEL)), _full((1, D_MODEL))],
        out_specs=rows(D_MODEL),
        out_shape=jax.ShapeDtypeStruct((M, D_MODEL), jnp.float32),
        compiler_params=_params("parallel"),
        name="merge",
    )(x2d, y_a, y_b, w_gate, p_a.astype(bf16), p_b.astype(bf16), w_o.astype(bf16),
      ln_g.reshape(1, -1), ln_b.reshape(1, -1))


def _mem_kv_kernel(mem_ref, wk_ref, wv_ref, k_out, v_out):
    m = mem_ref[...].astype(jnp.bfloat16)
    k_out[...] = jnp.dot(m, wk_ref[...], preferred_element_type=jnp.float32).astype(k_out.dtype)
    v_out[...] = jnp.dot(m, wv_ref[...], preferred_element_type=jnp.float32).astype(v_out.dtype)


def _mem_kv(mem2d, wk, wv):
    M = mem2d.shape[0]
    bf16 = jnp.bfloat16
    out = jax.ShapeDtypeStruct((M, D_MODEL), bf16)
    return pl.pallas_call(
        _mem_kv_kernel,
        grid=(1,),
        in_specs=[_full((M, D_MODEL)), _full((D_MODEL, D_MODEL)), _full((D_MODEL, D_MODEL))],
        out_specs=[_full((M, D_MODEL))] * 2,
        out_shape=[out, out],
        compiler_params=_params("arbitrary"),
        name="mem_kv",
    )(mem2d, wk.astype(bf16), wv.astype(bf16))


def _xattn_kernel(x_ref, k_ref, v_ref, wq_ref, wo_ref, g_ref, b_ref, o_ref):
    bf16, f32 = jnp.bfloat16, jnp.float32
    x = x_ref[...]
    q = jnp.dot(x.astype(bf16), wq_ref[...], preferred_element_type=f32).astype(bf16)
    cols = [slice(h * X_HEAD_DIM, (h + 1) * X_HEAD_DIM) for h in range(X_HEADS)]
    scores = [lax.dot_general(q[:, sl], k_ref[:, sl], (((1,), (1,)), ((), ())),
                              preferred_element_type=f32) * (X_HEAD_DIM ** -0.5) for sl in cols]
    heads = []
    for s, sl in zip(scores, cols):
        p = jnp.exp(s - jnp.max(s, axis=-1, keepdims=True))
        p = p / jnp.sum(p, axis=-1, keepdims=True)
        heads.append(jnp.dot(p.astype(bf16), v_ref[:, sl], preferred_element_type=f32))
    o = jnp.concatenate(heads, axis=1).astype(bf16)
    xa = jnp.dot(o, wo_ref[...], preferred_element_type=f32)
    o_ref[...] = _layer_norm(ALPHA * x + xa, g_ref[...], b_ref[...])


def _xattn(x2d, k_mem, v_mem, wq, wo, ln_g, ln_b, seq_len, mem_len, tm=512):
    M = x2d.shape[0]
    bf16 = jnp.bfloat16
    seq_tiles = seq_len // tm
    rows = pl.BlockSpec((tm, D_MODEL), lambda i: (i, 0))
    mem_spec = pl.BlockSpec((mem_len, D_MODEL), lambda i: (i // seq_tiles, 0))
    return pl.pallas_call(
        _xattn_kernel,
        grid=(M // tm,),
        in_specs=[rows, mem_spec, mem_spec, _full((D_MODEL, D_MODEL)), _full((D_MODEL, D_MODEL)),
                  _full((1, D_MODEL)), _full((1, D_MODEL))],
        out_specs=rows,
        out_shape=jax.ShapeDtypeStruct((M, D_MODEL), jnp.float32),
        compiler_params=_params("parallel"),
        name="xattn",
    )(x2d, k_mem, v_mem, wq.astype(bf16), wo.astype(bf16), ln_g.reshape(1, -1), ln_b.reshape(1, -1))


FFN_CHUNK = 1408


def _ffn_kernel(seq_tiles, x_ref, xp_ref, wup_ref, cw_ref, cb_ref, wdn_ref, g_ref, b_ref, o_ref):
    bf16, f32 = jnp.bfloat16, jnp.float32
    i = pl.program_id(0)
    tm = x_ref.shape[0]
    H = V7X_SUBLANES
    x = x_ref[...]
    xprev = jnp.where(i % seq_tiles == 0, 0.0, xp_ref[...])
    xe = jnp.concatenate([xprev, x], axis=0).astype(bf16)

    def conv(cols):
        h = jnp.dot(xe, wup_ref[:, cols], preferred_element_type=f32)
        w = cw_ref[:, cols]
        return (h[H - 2:H - 2 + tm] * w[0:1] + h[H - 1:H - 1 + tm] * w[1:2] + h[H:] * w[2:3]
                + cb_ref[:, cols])

    acc = jnp.zeros((tm, D_MODEL), f32)
    for c in range(D_FF // FFN_CHUNK):
        gate = conv(slice(c * FFN_CHUNK, (c + 1) * FFN_CHUNK))
        val = conv(slice(D_FF + c * FFN_CHUNK, D_FF + (c + 1) * FFN_CHUNK))
        act = (gate * jax.nn.sigmoid(gate) * val).astype(bf16)
        acc = acc + jnp.dot(act, wdn_ref[c * FFN_CHUNK:(c + 1) * FFN_CHUNK, :], preferred_element_type=f32)
    o_ref[...] = _layer_norm(ALPHA * x + acc, g_ref[...], b_ref[...])


def _ffn(x2d, w_up, conv_w, conv_b, w_down, ln_g, ln_b, seq_len, tm=512):
    M = x2d.shape[0]
    bf16 = jnp.bfloat16
    seq_tiles = seq_len // tm
    blocks_per_tile = tm // V7X_SUBLANES
    rows = pl.BlockSpec((tm, D_MODEL), lambda i: (i, 0))
    once = lambda shape: pl.BlockSpec(shape, lambda i: (0,) * len(shape), pipeline_mode=pl.Buffered(1))
    return pl.pallas_call(
        functools.partial(_ffn_kernel, seq_tiles),
        grid=(M // tm,),
        in_specs=[rows,
                  pl.BlockSpec((V7X_SUBLANES, D_MODEL), lambda i: (jnp.maximum(i * blocks_per_tile - 1, 0), 0)),
                  once((D_MODEL, 2 * D_FF)), _full((3, 2 * D_FF)), _full((1, 2 * D_FF)),
                  once((D_FF, D_MODEL)), _full((1, D_MODEL)), _full((1, D_MODEL))],
        out_specs=rows,
        out_shape=jax.ShapeDtypeStruct((M, D_MODEL), jnp.float32),
        compiler_params=_params("parallel"),
        name="ffn",
    )(x2d, x2d, w_up.astype(bf16), conv_w, conv_b.reshape(1, -1), w_down.astype(bf16),
      ln_g.reshape(1, -1), ln_b.reshape(1, -1))


def kernel(x, mem, w_in, rwkv_mu, rwkv_w0, rwkv_w2, rwkv_a0, rwkv_a2, rwkv_k_k, rwkv_k_a, rwkv_r_k, rwkv_gn_g, rwkv_gn_b, nsa_pe_k, nsa_pe_v, nsa_ck_w1, nsa_ck_w2, nsa_cv_w1, nsa_cv_w2, merge_p_a, merge_p_b, mix_w_o, ln1_g, ln1_b, xa_wq, xa_wk, xa_wv, xa_wo, ln2_g, ln2_b, ffn_w_up, ffn_conv_w, ffn_conv_b, ffn_w_down, ln3_g, ln3_b):
    B, S, _ = x.shape
    mem_len = mem.shape[1]
    bf16 = jnp.bfloat16
    x2d = x.reshape(B * S, D_MODEL)
    for l in range(DEPTH):
        w = w_in[l]
        w_rwkv = w[:, :RWKV_COLS].astype(bf16)
        w_nsa = _nsa_weight(w[:, RWKV_COLS:RWKV_COLS + NSA_COLS])
        w_gate = w[:, RWKV_COLS + NSA_COLS:].astype(bf16)
        y_a = _rwkv_time_mix(x2d, w_rwkv, rwkv_mu[l], rwkv_w0[l], rwkv_w2[l], rwkv_a0[l], rwkv_a2[l],
                             rwkv_k_k[l], rwkv_k_a[l], rwkv_r_k[l], rwkv_gn_g[l], rwkv_gn_b[l], B, S)
        y_b = _nsa_branch(x2d, w_nsa, nsa_pe_k[l], nsa_pe_v[l], nsa_ck_w1[l], nsa_ck_w2[l],
                          nsa_cv_w1[l], nsa_cv_w2[l], B, S)
        x2d = _merge(x2d, y_a, y_b, w_gate, merge_p_a[l], merge_p_b[l], mix_w_o[l], ln1_g[l], ln1_b[l])
        k_mem, v_mem = _mem_kv(mem.reshape(B * mem_len, D_MODEL), xa_wk[l], xa_wv[l])
        x2d = _xattn(x2d, k_mem, v_mem, xa_wq[l], xa_wo[l], ln2_g[l], ln2_b[l], S, mem_len)
        x2d = _ffn(x2d, ffn_w_up[l], ffn_conv_w[l], ffn_conv_b[l], ffn_w_down[l], ln3_g[l], ln3_b[l], S)
    return x2d.reshape(B, S, D_MODEL)
```

```python
import functools
import math

import jax
import jax.numpy as jnp
import numpy as np
from jax import lax
from jax.experimental import pallas as pl
from jax.experimental.pallas import tpu as pltpu

D_MODEL = 1024
HEAD_DIM = 64
RWKV_DIM = 512
RWKV_HEADS = 8
LORA = 64
RWKV_COLS = 3 * RWKV_DIM + 2 * LORA
GN_EPS = 64e-5
NSA_DIM = 512
NSA_Q_HEADS = 8
NSA_GROUPS = 2
NSA_R = NSA_Q_HEADS // NSA_GROUPS
NSA_KV_DIM = NSA_GROUPS * HEAD_DIM
NSA_COLS = NSA_DIM + 6 * NSA_KV_DIM + 3 * NSA_Q_HEADS
NSA_COLS_PAD = 1408
CMP_BLOCK = 32
CMP_STRIDE = 16
CMP_HIDDEN = 256
SEL_BLOCK = 64
N_SELECT = 16
WINDOW = 512
Q_BLOCK = 128
ROPE_THETA = 500000.0
ROPE_DIM = 16
X_HEADS = 4
X_HEAD_DIM = 256
D_FF = 2816
LN_EPS = 1e-5
DEPTH = 1
ALPHA = (2 * DEPTH) ** 0.25
NEG = -1e30
BIG = 1e30
LOG2E = math.log2(math.e)

V7X_LANES = 128
V7X_SUBLANES = 8
V7X_VMEM_LIMIT_BYTES = 56 * 1024 * 1024

HI = lax.Precision.HIGHEST
RWKV_CHUNK = 64


def _params(*sem):
    return pltpu.CompilerParams(dimension_semantics=sem, vmem_limit_bytes=V7X_VMEM_LIMIT_BYTES)


def _full(shape):
    n = len(shape)
    return pl.BlockSpec(shape, lambda *_: (0,) * n)


def _head_ones(width):
    r = lax.broadcasted_iota(jnp.int32, (width, width), 0) // HEAD_DIM
    c = lax.broadcasted_iota(jnp.int32, (width, width), 1) // HEAD_DIM
    return (r == c).astype(jnp.float32)


def _layer_norm(y, g, b):
    mu = jnp.mean(y, axis=-1, keepdims=True)
    d = y - mu
    var = jnp.mean(d * d, axis=-1, keepdims=True)
    return d * lax.rsqrt(var + LN_EPS) * g + b


def _rwkv_prep_kernel(seq_tiles, x_ref, xp_ref, w_ref, mu_ref, w0_ref, w2_ref, a0_ref, a2_ref,
                      kk_ref, ka_ref, rk_ref,
                      r_out, k_out, v_out, lw_out, a_out, b_out, bonus_out):
    i = pl.program_id(0)
    tm = x_ref.shape[0]
    C = RWKV_DIM
    w = w_ref[...]
    p = jnp.dot(x_ref[...].astype(jnp.bfloat16), w, preferred_element_type=jnp.float32)
    xprev = xp_ref[...].astype(jnp.bfloat16)
    pprev = jnp.dot(xprev, w, preferred_element_type=jnp.float32)[V7X_SUBLANES - 1:V7X_SUBLANES, :]
    pprev = jnp.where(i % seq_tiles == 0, 0.0, pprev)
    row = lax.broadcasted_iota(jnp.int32, (tm, 1), 0)
    shifted = jnp.where(row == 0, pprev, pltpu.roll(p, 1, 0))
    p = p + (shifted - p) * mu_ref[...]
    r, k, v = p[:, :C], p[:, C:2 * C], p[:, 2 * C:3 * C]
    wl = p[:, 3 * C:3 * C + LORA]
    al = p[:, 3 * C + LORA:]
    z = -(w0_ref[...] + jnp.dot(jnp.tanh(wl).astype(jnp.bfloat16), w2_ref[...],
                                preferred_element_type=jnp.float32))
    softplus = jnp.maximum(z, 0.0) + jnp.log(1.0 + jnp.exp(-jnp.abs(z)))
    w_log = -softplus - 0.5
    lw_out[...] = -jnp.exp(w_log)
    a = jax.nn.sigmoid(a0_ref[...] + jnp.dot(al.astype(jnp.bfloat16), a2_ref[...],
                                             preferred_element_type=jnp.float32))
    ones = _head_ones(C).astype(jnp.bfloat16)
    ones2 = jnp.concatenate([ones, ones], axis=0)

    def head_sum(t):
        return jnp.dot(jnp.concatenate(_split(t), axis=1), ones2, preferred_element_type=jnp.float32)

    kk = k * kk_ref[...]
    kk = kk / jnp.maximum(jnp.sqrt(head_sum(kk * kk)), 1e-12)
    kmod = k * (1.0 + (a - 1.0) * ka_ref[...])
    bonus = head_sum(r * kmod * rk_ref[...]) * v
    r_out[...] = r
    k_out[...] = kmod
    v_out[...] = v
    a_out[...] = -kk
    b_out[...] = kk * a
    bonus_out[...] = bonus


def _rwkv_prep(x2d, w_rwkv, mu, w0, w2, a0, a2, k_k, k_a, r_k, seq_len, tm=512):
    M = x2d.shape[0]
    C = RWKV_DIM
    seq_tiles = seq_len // tm
    row = lambda a: a.reshape(1, -1)
    out = jax.ShapeDtypeStruct((M, C), jnp.float32)
    tile = pl.BlockSpec((tm, C), lambda i: (i, 0))
    blocks_per_tile = tm // V7X_SUBLANES
    return pl.pallas_call(
        functools.partial(_rwkv_prep_kernel, seq_tiles),
        grid=(M // tm,),
        in_specs=[
            pl.BlockSpec((tm, D_MODEL), lambda i: (i, 0)),
            pl.BlockSpec((V7X_SUBLANES, D_MODEL), lambda i: (jnp.maximum(i * blocks_per_tile - 1, 0), 0)),
            _full((D_MODEL, RWKV_COLS)), _full((1, RWKV_COLS)), _full((1, C)), _full((LORA, C)),
            _full((1, C)), _full((LORA, C)), _full((1, C)), _full((1, C)), _full((1, C)),
        ],
        out_specs=[tile] * 7,
        out_shape=[out] * 7,
        compiler_params=_params("parallel"),
        name="rwkv_prep",
    )(x2d, x2d, w_rwkv, row(mu), row(w0), w2.astype(jnp.bfloat16), row(a0), a2.astype(jnp.bfloat16),
      row(k_k), row(k_a), row(r_k))


RWKV_GROUP = 4


def _split(x):
    hi = x.astype(jnp.bfloat16)
    return hi, (x - hi.astype(jnp.float32)).astype(jnp.bfloat16)


def _mm(a, b):
    return jnp.dot(a.astype(jnp.bfloat16), b.astype(jnp.bfloat16), preferred_element_type=jnp.float32)


def _rwkv_mix_kernel(r_ref, k_ref, v_ref, lw_ref, a_ref, b_ref, bonus_ref, gng_ref, gnb_ref,
                     y_out, state):
    T, N, GH = RWKV_CHUNK, HEAD_DIM, RWKV_GROUP
    W = GH * N
    f32 = jnp.float32

    @pl.when(pl.program_id(1) == 0)
    def _():
        state[...] = jnp.zeros_like(state)

    ri = lax.broadcasted_iota(jnp.int32, (W, W), 0)
    ci = lax.broadcasted_iota(jnp.int32, (W, W), 1)
    same_head = ri // N == ci // N
    strict, incl = ri > ci, ri >= ci
    eye = (ri == ci).astype(f32)
    tri = (lax.broadcasted_iota(jnp.int32, (T, T), 0)
           >= lax.broadcasted_iota(jnp.int32, (T, T), 1)).astype(jnp.bfloat16)
    mean_w = (_head_ones(RWKV_DIM) * (1.0 / N)).astype(jnp.bfloat16)

    def expand(x):
        return jnp.where(same_head, jnp.concatenate([x] * GH, axis=0), jnp.zeros((), x.dtype))

    def collapse(x):
        out = x[:T]
        for h in range(1, GH):
            out = out + x[h * T:(h + 1) * T]
        return out

    bf = lambda t: t.astype(jnp.bfloat16)
    n_chunks = r_ref.shape[0] // T
    n_groups = RWKV_HEADS // GH
    chains = [(ch, g) for ch in range(n_chunks) for g in range(n_groups)]
    each = lambda fn, *lists: [fn(*args) for args in zip(*lists)]

    def cumsum(x):
        hi, lo = _split(x)
        lo2 = (x - hi.astype(f32) - lo.astype(f32)).astype(jnp.bfloat16)
        return jnp.dot(jnp.concatenate([tri] * 3, axis=1), jnp.concatenate([hi, lo, lo2], axis=0),
                       preferred_element_type=f32)

    c_all = [cumsum(lw_ref[ch * T:(ch + 1) * T, :]) for ch in range(n_chunks)]

    def operands(ch, g):
        rows, cols = slice(ch * T, (ch + 1) * T), slice(g * W, (g + 1) * W)
        lw, c = lw_ref[rows, cols], c_all[ch][:, cols]
        r, k, v = r_ref[rows, cols], k_ref[rows, cols], v_ref[rows, cols]
        a, b = a_ref[rows, cols], b_ref[rows, cols]
        c_last = c[T - 1:T, :]
        e_neg, e_end = jnp.exp(-c), jnp.exp(c_last - c)
        r_t = r * jnp.exp(c)
        return dict(a_x=expand(bf(a * jnp.exp(c - lw))), r_x=expand(bf(r_t)), b_x=expand(bf(b * e_neg)),
                    k_x=expand(bf(k * e_neg)), v_x=expand(bf(v)), r_t=r_t, b_end=b * e_end, k_end=k * e_end,
                    w_end=jnp.exp(c_last))

    ops = [operands(ch, g) for ch, g in chains]
    P = [lax.dot_general(jnp.concatenate([o["a_x"], o["r_x"]], axis=0),
                         jnp.concatenate([o["b_x"], o["k_x"]], axis=0),
                         (((1,), (1,)), ((), ())), preferred_element_type=f32) for o in ops]
    L_ab = [jnp.where(strict, p[:W, :W], 0.0) for p in P]
    L_ak = [bf(jnp.where(strict, p[:W, W:], 0.0)) for p in P]
    M_rb = [bf(jnp.where(incl, p[W:, :W], 0.0)) for p in P]
    M_rk = [bf(jnp.where(incl, p[W:, W:], 0.0)) for p in P]
    v_x = [o["v_x"] for o in ops]

    base = V7X_SUBLANES
    D = [bf(jnp.where(ri // base == ci // base, l, 0.0)) for l in L_ab]
    D2 = each(_mm, D, D)
    D4 = each(_mm, D2, D2)
    X = each(_mm, each(lambda d, d2: _mm(eye + d.astype(f32), eye + d2), D, D2), [eye + d4 for d4 in D4])
    def second_rows(x, blk):
        return jnp.concatenate([x[s:s + blk] for s in range(blk, W, 2 * blk)], axis=0)

    def merge_rows(x, u, blk):
        pieces = []
        for i, s in enumerate(range(0, W, 2 * blk)):
            pieces += [x[s:s + blk], x[s + blk:s + 2 * blk] + u[i * blk:(i + 1) * blk]]
        return jnp.concatenate(pieces, axis=0)

    blk = base
    while blk < T:
        pair = (ri // (2 * blk) == ci // (2 * blk)) & (ri // blk != ci // blk)
        Xb = [bf(x) for x in X]
        lower = [bf(second_rows(x, blk)) for x in X]
        step = each(_mm, each(_mm, lower, [jnp.where(pair, l, 0.0) for l in L_ab]), Xb)
        X = [merge_rows(x, u, blk) for x, u in zip(X, step)]
        blk *= 2

    lakv = each(_mm, L_ak, v_x)
    AV = [bf(_mm(x, jnp.concatenate([o["a_x"], bf(t)], axis=1))) for x, o, t in zip(X, ops, lakv)]
    ry = each(_mm, M_rb, AV)
    mrkv = each(_mm, M_rk, v_x)
    gh = [_mm(expand(o["b_end"]).T, av) for o, av in zip(ops, AV)]
    kv = [_mm(expand(o["k_end"]).T, vx) for o, vx in zip(ops, v_x)]
    rp = [bf(expand(o["r_t"]) + t[:, :W]) for o, t in zip(ops, ry)]
    yp = [t[:, W:] + u for t, u in zip(ry, mrkv)]
    G = [bf(eye * o["w_end"] + t[:, :W]) for o, t in zip(ops, gh)]
    H = [t[:, W:] + u for t, u in zip(gh, kv)]

    y_rows = []
    for ch in range(n_chunks):
        ys = []
        for g in range(n_groups):
            i = chains.index((ch, g))
            s0 = bf(state[g])
            ys.append(collapse(_mm(rp[i], s0) + yp[i]))
            state[g] = _mm(G[i], s0) + H[i]
        y_rows.append(jnp.concatenate(ys, axis=1))
    y = jnp.concatenate(y_rows, axis=0)
    y_hi, y_lo = _split(y)
    ym = jnp.dot(jnp.concatenate([y_hi, y_lo], axis=1), jnp.concatenate([mean_w, mean_w], axis=0),
                 preferred_element_type=f32)
    d = y - ym
    yv = _mm(d * d, mean_w)
    y_out[...] = d * lax.rsqrt(yv + GN_EPS) * gng_ref[...] + gnb_ref[...] + bonus_ref[...]


def _rwkv_mix(r, k, v, lw, a, b, bonus, gn_g, gn_b, batch, chunks_per_step=4):
    M, C = r.shape
    rows = RWKV_CHUNK * chunks_per_step
    steps = M // rows // batch
    W = RWKV_GROUP * HEAD_DIM
    tile = pl.BlockSpec((rows, C), lambda bi, ci: (bi * steps + ci, 0))
    return pl.pallas_call(
        _rwkv_mix_kernel,
        grid=(batch, steps),
        in_specs=[tile] * 7 + [_full((1, C)), _full((1, C))],
        out_specs=tile,
        out_shape=jax.ShapeDtypeStruct((M, C), jnp.float32),
        scratch_shapes=[pltpu.VMEM((RWKV_HEADS // RWKV_GROUP, W, W), jnp.float32)],
        compiler_params=_params("parallel", "arbitrary"),
        name="rwkv_mix",
    )(r, k, v, lw, a, b, bonus, gn_g.reshape(1, C), gn_b.reshape(1, C))


def _rwkv_time_mix(x2d, w_rwkv, mu, w0, w2, a0, a2, k_k, k_a, r_k, gn_g, gn_b, batch, seq_len):
    r, k, v, lw, a, b, bonus = _rwkv_prep(x2d, w_rwkv, mu, w0, w2, a0, a2, k_k, k_a, r_k, seq_len)
    return _rwkv_mix(r, k, v, lw, a, b, bonus, gn_g, gn_b, batch)


def _rope_tables(seq_len):
    half = ROPE_DIM // 2
    inv = ROPE_THETA ** (-np.arange(half, dtype=np.float64) * 2.0 / ROPE_DIM)
    ang = np.arange(seq_len, dtype=np.float64)[:, None] * inv[None, :]
    cos, sin = np.cos(ang).astype(np.float32), np.sin(ang).astype(np.float32)
    pad = np.zeros((seq_len, HEAD_DIM - ROPE_DIM), np.float32)
    zero = np.zeros_like(sin)
    c = np.concatenate([cos, cos, pad + 1.0], axis=1)
    s_lo = np.concatenate([-sin, zero, pad], axis=1)
    s_hi = np.concatenate([zero, sin, pad], axis=1)
    two = lambda t: jnp.asarray(np.concatenate([t, t], axis=1))
    return two(c), two(s_lo), two(s_hi)


def _rope_pair(x, c, s_lo, s_hi):
    return x * c + pltpu.roll(x, V7X_LANES - ROPE_DIM // 2, 1) * s_lo + pltpu.roll(x, ROPE_DIM // 2, 1) * s_hi


GATE_SLOTS = V7X_SUBLANES
V_ROWS = HEAD_DIM + 2 * V7X_SUBLANES


def _nsa_prep_kernel(x_ref, w_ref, c_ref, slo_ref, shi_ref,
                     qT_out, kc_out, vc_out, ks_out, vsT_out, kw_out, vwT_out, gT_out):
    f32 = jnp.float32
    L, N, G, R, Q = V7X_LANES, HEAD_DIM, NSA_GROUPS, NSA_R, Q_BLOCK
    tm = x_ref.shape[0]
    p = jnp.dot(x_ref[...].astype(jnp.bfloat16), w_ref[...], preferred_element_type=f32)
    c, s_lo, s_hi = c_ref[...], slo_ref[...], shi_ref[...]
    rope = lambda t: _rope_pair(t, c, s_lo, s_hi)
    kv = lambda i: p[:, NSA_DIM + i * L:NSA_DIM + (i + 1) * L]

    qT = [(rope(p[:, j * L:(j + 1) * L]) * (N ** -0.5 * LOG2E)).T for j in range(NSA_DIM // L)]
    gT = jax.nn.sigmoid(kv(6)).T
    for g in range(G):
        for qb in range(tm // Q):
            blk = slice(qb * Q, (qb + 1) * Q)
            heads = [qT[(g * R + r) // 2][((g * R + r) % 2) * N:((g * R + r) % 2 + 1) * N, blk]
                     for r in range(R)]
            qT_out[0, g, qb] = jnp.concatenate(heads, axis=1).astype(qT_out.dtype)
            gates = [gT[(g * R + r) * GATE_SLOTS:(g * R + r + 1) * GATE_SLOTS, blk] for r in range(R)]
            gT_out[0, g, qb] = jnp.concatenate(gates, axis=1)

    k_c, k_s, k_w = rope(kv(0)), rope(kv(2)), rope(kv(4))
    v_c = kv(1)
    v_sT, v_wT = kv(3).T, kv(5).T
    slot = (lax.broadcasted_iota(jnp.int32, (tm, L - N), 0) // SEL_BLOCK) % SEL_SLOTS
    onehot = (slot == lax.broadcasted_iota(jnp.int32, (tm, L - N), 1)).astype(f32)
    for g in range(G):
        cols = slice(g * N, (g + 1) * N)
        kc_out[0, g] = k_c[:, cols]
        vc_out[0, g] = v_c[:, cols]
        ks_out[0, g] = jnp.concatenate([k_s[:, cols], onehot], axis=1).astype(ks_out.dtype)
        kw_out[0, g] = k_w[:, cols].astype(kw_out.dtype)
        extra = (lax.broadcasted_iota(jnp.int32, (V_ROWS - N, tm), 0) == 0).astype(f32)
        vsT_out[0, g] = jnp.concatenate([v_sT[cols, :], extra], axis=0).astype(vsT_out.dtype)
        vwT_out[0, g] = jnp.concatenate([v_wT[cols, :], extra], axis=0).astype(vwT_out.dtype)


def _nsa_weight(w):
    n_qkv = NSA_DIM + 6 * NSA_KV_DIM
    gates = w[:, n_qkv:].reshape(-1, NSA_Q_HEADS, 3)
    gates = jnp.pad(gates, ((0, 0), (0, 0), (0, GATE_SLOTS - 3))).reshape(-1, NSA_Q_HEADS * GATE_SLOTS)
    gates = jnp.pad(gates, ((0, 0), (0, NSA_COLS_PAD - n_qkv - NSA_Q_HEADS * GATE_SLOTS)))
    return jnp.concatenate([w[:, :n_qkv], gates], axis=1).astype(jnp.bfloat16)


def _nsa_prep(x2d, w_nsa, batch, seq_len, tm=512):
    B, S, G, N, L, Q = batch, seq_len, NSA_GROUPS, HEAD_DIM, V7X_LANES, Q_BLOCK
    RQ = NSA_R * Q
    tiles = S // tm
    tabs = _rope_tables(S)
    tab_spec = pl.BlockSpec((tm, L), lambda i: (i % tiles, 0))
    f32, bf16 = jnp.float32, jnp.bfloat16
    per_q = lambda rows: pl.BlockSpec((1, G, tm // Q, rows, RQ), lambda i: (i // tiles, 0, i % tiles, 0, 0))
    by_row = lambda w: pl.BlockSpec((1, G, tm, w), lambda i: (i // tiles, 0, i % tiles, 0))
    by_col = pl.BlockSpec((1, G, V_ROWS, tm), lambda i: (i // tiles, 0, 0, i % tiles))
    sds = jax.ShapeDtypeStruct
    return pl.pallas_call(
        _nsa_prep_kernel,
        grid=(B * tiles,),
        in_specs=[pl.BlockSpec((tm, D_MODEL), lambda i: (i, 0)), _full((D_MODEL, NSA_COLS_PAD)),
                  tab_spec, tab_spec, tab_spec],
        out_specs=[per_q(N), by_row(N), by_row(N), by_row(L), by_col, by_row(N), by_col, per_q(GATE_SLOTS)],
        out_shape=[sds((B, G, S // Q, N, RQ), bf16), sds((B, G, S, N), f32), sds((B, G, S, N), f32),
                   sds((B, G, S, L), bf16), sds((B, G, V_ROWS, S), bf16), sds((B, G, S, N), bf16),
                   sds((B, G, V_ROWS, S), bf16), sds((B, G, S // Q, GATE_SLOTS, RQ), f32)],
        compiler_params=_params("parallel"),
        name="nsa_prep",
    )(x2d, w_nsa, *tabs)


def _gelu_tanh(x):
    return 0.5 * x * (1.0 + jnp.tanh(math.sqrt(2.0 / math.pi) * (x + 0.044715 * x * x * x)))


def _nsa_compress_kernel(xk_ref, xv_ref, pek_ref, pev_ref, kw1_ref, kw2_ref, vw1_ref, vw2_ref,
                         kc_out, vcT_out):
    N, bf16, f32 = HEAD_DIM, jnp.bfloat16, jnp.float32
    n_blocks = kc_out.shape[2]

    def mlp(x_ref, pe_ref, w1_ref, w2_ref):
        lo = jnp.zeros((n_blocks, CMP_HIDDEN), f32)
        hi = jnp.zeros((n_blocks, CMP_HIDDEN), f32)
        for l in range(CMP_STRIDE):
            rows = x_ref[0, 0, pl.ds(l, n_blocks, stride=CMP_STRIDE), :]
            lo = lo + jnp.dot((rows + pe_ref[l:l + 1, :]).astype(bf16), w1_ref[l * N:(l + 1) * N, :],
                              preferred_element_type=f32)
            m = CMP_STRIDE + l
            hi = hi + jnp.dot((rows + pe_ref[m:m + 1, :]).astype(bf16), w1_ref[m * N:(m + 1) * N, :],
                              preferred_element_type=f32)
        pre = lo + pltpu.roll(hi, n_blocks - 1, 0)
        return jnp.dot(_gelu_tanh(pre).astype(bf16), w2_ref[...], preferred_element_type=f32)

    kc_out[0, 0] = mlp(xk_ref, pek_ref, kw1_ref, kw2_ref).astype(kc_out.dtype)
    vc = mlp(xv_ref, pev_ref, vw1_ref, vw2_ref)
    vcT_out[0, 0] = jnp.concatenate([vc, jnp.zeros_like(vc)], axis=1).T[:N].astype(vcT_out.dtype)


def _nsa_compress(xk, xv, pe_k, pe_v, ck_w1, ck_w2, cv_w1, cv_w2):
    B, G, S, N = xk.shape
    NC = S // CMP_STRIDE
    bf16 = jnp.bfloat16
    xin = pl.BlockSpec((1, 1, S, N), lambda b, g: (b, g, 0, 0))
    return pl.pallas_call(
        _nsa_compress_kernel,
        grid=(B, G),
        in_specs=[xin, xin, _full((CMP_BLOCK, N)), _full((CMP_BLOCK, N)), _full((CMP_BLOCK * N, CMP_HIDDEN)),
                  _full((CMP_HIDDEN, N)), _full((CMP_BLOCK * N, CMP_HIDDEN)), _full((CMP_HIDDEN, N))],
        out_specs=[pl.BlockSpec((1, 1, NC, N), lambda b, g: (b, g, 0, 0)),
                   pl.BlockSpec((1, 1, N, NC), lambda b, g: (b, g, 0, 0))],
        out_shape=[jax.ShapeDtypeStruct((B, G, NC, N), bf16), jax.ShapeDtypeStruct((B, G, N, NC), bf16)],
        compiler_params=_params("parallel", "parallel"),
        name="nsa_compress",
    )(xk, xv, pe_k, pe_v, ck_w1.astype(bf16), ck_w2.astype(bf16), cv_w1.astype(bf16), cv_w2.astype(bf16))


SEL_KEYS = 512
SEL_SLOTS = V7X_SUBLANES
CMP_TILE = 128
RANK_ROWS = 32
RANK_UNROLL = 8


def _nsa_attn_kernel(qT_ref, kc_ref, vcT_ref, ovT_ref, ks_ref, vsT_ref, kw_ref, vwT_ref, gT_ref,
                     o_ref, val_ref, cnt_ref, sel_ref, qa_ref, s_ref, cm_ref, m_ref, acc_ref, part_ref,
                     imp_ref):
    qb = pl.program_id(2)
    t0 = qb * Q_BLOCK
    R, Q = NSA_R, Q_BLOCK
    RQ = R * Q
    f32, bf16 = jnp.float32, jnp.bfloat16
    qT = qT_ref[0, 0, 0]
    lane = lax.broadcasted_iota(jnp.int32, (1, RQ), 1)
    tpos = t0 + lane % Q
    mm = lambda a, b: jnp.dot(a, b, preferred_element_type=f32)

    g = gT_ref[0, 0, 0]
    NC = kc_ref.shape[2]
    NSB = ovT_ref.shape[0]

    span = WINDOW + Q
    sub = lax.broadcasted_iota(jnp.int32, (Q, 1), 0)
    qpos = lane % Q

    def local_branches(rows, aligned_window):
        w0 = pl.multiple_of(t0 - WINDOW, Q) if aligned_window else 0
        s_cmp = mm(kc_ref[0, 0, :rows, :], qT)
        s_win = mm(kw_ref[0, 0, pl.ds(w0, span), :], qT)

        cend = lax.broadcasted_iota(jnp.int32, (rows, 1), 0) * CMP_STRIDE + (CMP_BLOCK - 1)
        mask = cend <= tpos
        sm = jnp.where(mask, s_cmp, NEG)
        e = jnp.where(mask, jnp.exp2(sm - jnp.max(sm, axis=0, keepdims=True)), 0.0)
        den = jnp.sum(e, axis=0, keepdims=True)
        p = e / jnp.where(den > 0.0, den, 1.0)
        o_c = mm(vcT_ref[0, 0, :, :rows], p.astype(bf16))
        psum = p[:, :Q]
        for r in range(1, R):
            psum = psum + p[:, r * Q:(r + 1) * Q]
        ps_hi, ps_lo = _split(psum)
        ov = ovT_ref[:, :rows]
        imp_ref[...] = mm(jnp.concatenate([ov, ov], axis=1), jnp.concatenate([ps_hi, ps_lo], axis=0))

        if aligned_window:
            head = jnp.where(sub > qpos, s_win[:Q], NEG)
            tail = jnp.where(sub <= qpos, s_win[WINDOW:], NEG)
            sm = jnp.concatenate([head, s_win[Q:WINDOW], tail], axis=0)
        else:
            diff = tpos - lax.broadcasted_iota(jnp.int32, (span, 1), 0)
            sm = jnp.where((diff >= 0) & (diff < WINDOW), s_win, NEG)
        p = jnp.exp2(sm - jnp.max(sm, axis=0, keepdims=True))
        pv = mm(vwT_ref[0, 0, :, pl.ds(w0, span)], p.astype(bf16))
        o_w = pv[:HEAD_DIM] / pv[HEAD_DIM:HEAD_DIM + 1]
        part_ref[...] = g[0:1, :] * o_c + g[2:3, :] * o_w

    tile = min(CMP_TILE, NC)
    per_tile = tile * CMP_STRIDE // Q
    n_tiles = NC // tile
    first_aligned = WINDOW // Q
    for i in range(n_tiles):
        lo, hi = i * per_tile, (i + 1) * per_tile
        cuts = [lo, hi] if not lo < first_aligned < hi else [lo, first_aligned, hi]
        for a, b in zip(cuts[:-1], cuts[1:]):
            in_range = (qb >= a) if (i == n_tiles - 1 and b == hi) else ((qb >= a) & (qb < b))
            pl.when(in_range)(functools.partial(local_branches, (i + 1) * tile, a >= first_aligned))
    imp = imp_ref[...]

    tq = t0 + lax.broadcasted_iota(jnp.int32, (1, Q), 1)
    jblk = lax.broadcasted_iota(jnp.int32, (NSB, 1), 0)
    cur = tq // SEL_BLOCK
    forced = (jblk == 0) | (jblk == cur) | (jblk == cur - 1)
    valid = jblk * SEL_BLOCK <= tq
    val = jnp.where(valid, jnp.where(forced, BIG, imp), NEG)
    val_ref[...] = val
    n_live = (t0 + Q - 1) // SEL_BLOCK + 1

    chunk = min(RANK_ROWS, NSB)
    n_rounds = (n_live + RANK_UNROLL - 1) // RANK_UNROLL
    for k in range(NSB // chunk):
        rows = slice(k * chunk, (k + 1) * chunk)

        @pl.when(k * chunk < n_live)
        def _():
            target = val_ref[rows, :]

            def strict_body(u, cnt):
                base = u * RANK_UNROLL
                for i in range(RANK_UNROLL):
                    cnt = cnt + jnp.where(val_ref[pl.ds(base + i, 1), :] > target, 1, 0)
                return cnt

            cnt_ref[rows, :] = lax.fori_loop(0, n_rounds, strict_body, jnp.zeros((chunk, Q), jnp.int32))

        @pl.when(k * chunk >= n_live)
        def _():
            cnt_ref[rows, :] = jnp.full((chunk, Q), N_SELECT, jnp.int32)

    taken =jnp.sum(jnp.where(valid & (cnt_ref[...] < N_SELECT), 1, 0), axis=0, keepdims=True)

    @pl.when(jnp.max(taken) > N_SELECT)
    def _():
        def tie_body(i, cnt):
            row = val_ref[pl.ds(i, 1), :]
            ge = jnp.where(row >= val, 1, 0)
            gt = jnp.where(row > val, 1, 0)
            return cnt + jnp.where(jblk > i, ge, gt)

        cnt_ref[...] = lax.fori_loop(0, n_live, tie_body, jnp.zeros((NSB, Q), jnp.int32))

    bias = jnp.where(cnt_ref[...] < N_SELECT, 0.0, NEG)
    sel_ref[...] = jnp.concatenate([bias] * R, axis=1)

    bps = SEL_SLOTS
    for buf in range(qa_ref.shape[0]):
        qa_ref[buf, :HEAD_DIM, :] = qT
        qa_ref[buf, HEAD_DIM:, :] = jnp.zeros((qa_ref.shape[1] - HEAD_DIM, RQ), bf16)
    m_ref[...] = jnp.full(m_ref.shape, NEG, f32)
    acc_ref[...] = jnp.zeros(acc_ref.shape, f32)
    krow = lax.broadcasted_iota(jnp.int32, (SEL_KEYS, 1), 0)

    def scores(kc, slot):
        k0 = pl.multiple_of(kc * SEL_KEYS, SEL_KEYS)
        grp = pl.multiple_of((kc * (SEL_KEYS // SEL_BLOCK)) // bps * bps, bps)
        rows = sel_ref[pl.ds(grp, bps), :]
        qa_ref[slot, HEAD_DIM:HEAD_DIM + 2 * bps, :] = (
            jnp.concatenate([rows, jnp.zeros_like(rows)], axis=0).astype(bf16))
        s = mm(ks_ref[0, 0, pl.ds(k0, SEL_KEYS), :], qa_ref[slot])
        s_ref[slot] = s
        cm_ref[slot] = jnp.max(s, axis=0, keepdims=True)

    def softmax(kc, slot, diagonal):
        s, cm = s_ref[slot], cm_ref[slot]
        if diagonal:
            s = jnp.where(pl.multiple_of(kc * SEL_KEYS, SEL_KEYS) + krow <= tpos, s, NEG)
            cm = jnp.max(s, axis=0, keepdims=True)
        m = m_ref[...]
        m_new = jnp.maximum(m, cm)
        m_ref[...] = m_new
        return jnp.exp2(s - m_new).astype(bf16), jnp.exp2(m - m_new)

    def values(kc, p):
        return mm(vsT_ref[0, 0, :, pl.ds(pl.multiple_of(kc * SEL_KEYS, SEL_KEYS), SEL_KEYS)], p)

    def fold(alpha, pv):
        acc_ref[...] = alpha * acc_ref[...] + pv

    def accumulate(kc, slot, diagonal):
        p, alpha = softmax(kc, slot, diagonal)
        fold(alpha, values(kc, p))

    last = (t0 + Q - 1) // SEL_KEYS
    scores(0, 0)

    def sel_body(j, carry):
        scores(2 * j + 1, 1)
        accumulate(2 * j, 0, False)
        scores(2 * j + 2, 0)
        accumulate(2 * j + 1, 1, False)
        return carry

    lax.fori_loop(0, last // 2, sel_body, 0)

    @pl.when(last % 2 == 1)
    def _():
        scores(last, 1)
        accumulate(last - 1, 0, False)
        accumulate(last, 1, True)

    @pl.when(last % 2 == 0)
    def _():
        accumulate(last, 0, True)

    o_s = acc_ref[:HEAD_DIM, :] / acc_ref[HEAD_DIM:HEAD_DIM + 1, :]
    oT = part_ref[...] + g[1:2, :] * o_s
    pairs = [jnp.concatenate([oT[:, (2 * i) * Q:(2 * i + 1) * Q], oT[:, (2 * i + 1) * Q:(2 * i + 2) * Q]],
                             axis=0).T for i in range(R // 2)]
    o_ref[...] = jnp.concatenate(pairs, axis=1)


def _nsa_attention(qT, kc, vcT, ks, vsT, kw, vwT, gT):
    B, G, NQB, _, RQ = qT.shape
    S = ks.shape[2]
    NC = kc.shape[2]
    NSB = S // SEL_BLOCK
    c = np.arange(NC)[None, :] * CMP_STRIDE
    j = np.arange(NSB)[:, None] * SEL_BLOCK
    ovT = jnp.asarray((c <= j + SEL_BLOCK - 1) & (c + CMP_BLOCK - 1 >= j), jnp.bfloat16)
    per_q = lambda rows: pl.BlockSpec((1, 1, 1, rows, RQ), lambda b, g, q: (b, g, q, 0, 0))
    per_g = lambda d0, d1: pl.BlockSpec((1, 1, d0, d1), lambda b, g, q: (b, g, 0, 0))
    return pl.pallas_call(
        _nsa_attn_kernel,
        grid=(B, G, NQB),
        in_specs=[per_q(HEAD_DIM), per_g(NC, HEAD_DIM), per_g(HEAD_DIM, NC), _full((NSB, NC)),
                  per_g(S, V7X_LANES), per_g(V_ROWS, S), per_g(S, HEAD_DIM), per_g(V_ROWS, S),
                  per_q(V7X_SUBLANES)],
        out_specs=pl.BlockSpec((Q_BLOCK, NSA_R * HEAD_DIM), lambda b, g, q: (b * NQB + q, g)),
        out_shape=jax.ShapeDtypeStruct((B * S, NSA_DIM), jnp.float32),
        scratch_shapes=[pltpu.VMEM((NSB, Q_BLOCK), jnp.float32), pltpu.VMEM((NSB, Q_BLOCK), jnp.int32),
                        pltpu.VMEM((NSB, RQ), jnp.float32),
                        pltpu.VMEM((2, V7X_LANES, RQ), jnp.bfloat16),
                        pltpu.VMEM((2, SEL_KEYS, RQ), jnp.float32), pltpu.VMEM((2, 1, RQ), jnp.float32),
                        pltpu.VMEM((1, RQ), jnp.float32),
                        pltpu.VMEM((V_ROWS, RQ), jnp.float32), pltpu.VMEM((HEAD_DIM, RQ), jnp.float32),
                        pltpu.VMEM((NSB, Q_BLOCK), jnp.float32)],
        compiler_params=_params("parallel", "parallel", "arbitrary"),
        name="nsa_attention",
    )(qT, kc, vcT, ovT, ks, vsT, kw, vwT, gT)


def _nsa_branch(x2d, w_nsa, pe_k, pe_v, ck_w1, ck_w2, cv_w1, cv_w2, batch, seq_len):
    qT, kc_in, vc_in, ks, vsT, kw, vwT, gT = _nsa_prep(x2d, w_nsa, batch, seq_len)
    kc, vcT = _nsa_compress(kc_in, vc_in, pe_k, pe_v, ck_w1, ck_w2, cv_w1, cv_w2)
    return _nsa_attention(qT, kc, vcT, ks, vsT, kw, vwT, gT)


def _merge_kernel(x_ref, ya_ref, yb_ref, wg_ref, pa_ref, pb_ref, wo_ref, g_ref, b_ref, o_ref):
    bf16 = jnp.bfloat16
    mm = lambda a, w: jnp.dot(a.astype(bf16), w, preferred_element_type=jnp.float32)
    half = x_ref.shape[0] // 2
    parts = [slice(0, half), slice(half, 2 * half)]
    xs = [x_ref[rows, :] for rows in parts]
    logits = [mm(x, wg_ref[...]) for x in xs]
    pa = [mm(ya_ref[rows, :], pa_ref[...]) for rows in parts]
    pb = [mm(yb_ref[rows, :], pb_ref[...]) for rows in parts]
    mixed = []
    for lg, a, b in zip(logits, pa, pb):
        gates = jax.nn.sigmoid(lg)
        mixed.append(mm(gates[:, :D_MODEL] * a + gates[:, D_MODEL:] * b, wo_ref[...]))
    for rows, x, mx in zip(parts, xs, mixed):
        o_ref[rows, :] = _layer_norm(ALPHA * x + mx, g_ref[...], b_ref[...])


def _merge(x2d, y_a, y_b, w_gate, p_a, p_b, w_o, ln_g, ln_b, tm=512):
    M = x2d.shape[0]
    bf16 = jnp.bfloat16
    rows = lambda w: pl.BlockSpec((tm, w), lambda i: (i, 0))
    return pl.pallas_call(
        _merge_kernel,
        grid=(M // tm,),
        in_specs=[rows(D_MODEL), rows(RWKV_DIM), rows(NSA_DIM), _full((D_MODEL, 2 * D_MODEL)),
                  _full((RWKV_DIM, D_MODEL)), _full((NSA_DIM, D_MODEL)), _full((D_MODEL, D_MODEL)),
                  _full((1, D_MODEL)), _full((1, D_MODEL))],
        out_specs=rows(D_MODEL),
        out_shape=jax.ShapeDtypeStruct((M, D_MODEL), jnp.float32),
        compiler_params=_params("parallel"),
        name="merge",
    )(x2d, y_a, y_b, w_gate, p_a.astype(bf16), p_b.astype(bf16), w_o.astype(bf16),
      ln_g.reshape(1, -1), ln_b.reshape(1, -1))


def _mem_kv_kernel(mem_ref, wk_ref, wv_ref, k_out, v_out):
    m = mem_ref[...].astype(jnp.bfloat16)
    k_out[...] = jnp.dot(m, wk_ref[...], preferred_element_type=jnp.float32).astype(k_out.dtype)
    v_out[...] = jnp.dot(m, wv_ref[...], preferred_element_type=jnp.float32).astype(v_out.dtype)


def _mem_kv(mem2d, wk, wv):
    M = mem2d.shape[0]
    bf16 = jnp.bfloat16
    out = jax.ShapeDtypeStruct((M, D_MODEL), bf16)
    return pl.pallas_call(
        _mem_kv_kernel,
        grid=(1,),
        in_specs=[_full((M, D_MODEL)), _full((D_MODEL, D_MODEL)), _full((D_MODEL, D_MODEL))],
        out_specs=[_full((M, D_MODEL))] * 2,
        out_shape=[out, out],
        compiler_params=_params("arbitrary"),
        name="mem_kv",
    )(mem2d, wk.astype(bf16), wv.astype(bf16))


def _xattn_kernel(x_ref, k_ref, v_ref, wq_ref, wo_ref, g_ref, b_ref, o_ref):
    bf16, f32 = jnp.bfloat16, jnp.float32
    cols = [slice(h * X_HEAD_DIM, (h + 1) * X_HEAD_DIM) for h in range(X_HEADS)]
    half = x_ref.shape[0] // 2
    parts = [slice(0, half), slice(half, 2 * half)]
    xs = [x_ref[rows, :] for rows in parts]
    qs = [jnp.dot(x.astype(bf16), wq_ref[...], preferred_element_type=f32).astype(bf16) for x in xs]
    scores = [[lax.dot_general(q[:, sl], k_ref[:, sl], (((1,), (1,)), ((), ())),
                               preferred_element_type=f32) * (X_HEAD_DIM ** -0.5) for sl in cols]
              for q in qs]
    outs = []
    for per_head in scores:
        heads = []
        for s, sl in zip(per_head, cols):
            p = jnp.exp(s - jnp.max(s, axis=-1, keepdims=True))
            p = p / jnp.sum(p, axis=-1, keepdims=True)
            heads.append(jnp.dot(p.astype(bf16), v_ref[:, sl], preferred_element_type=f32))
        o = jnp.concatenate(heads, axis=1).astype(bf16)
        outs.append(jnp.dot(o, wo_ref[...], preferred_element_type=f32))
    for rows, x, xa in zip(parts, xs, outs):
        o_ref[rows, :] = _layer_norm(ALPHA * x + xa, g_ref[...], b_ref[...])


def _xattn(x2d, k_mem, v_mem, wq, wo, ln_g, ln_b, seq_len, mem_len, tm=512):
    M = x2d.shape[0]
    bf16 = jnp.bfloat16
    seq_tiles = seq_len // tm
    rows = pl.BlockSpec((tm, D_MODEL), lambda i: (i, 0))
    mem_spec = pl.BlockSpec((mem_len, D_MODEL), lambda i: (i // seq_tiles, 0))
    return pl.pallas_call(
        _xattn_kernel,
        grid=(M // tm,),
        in_specs=[rows, mem_spec, mem_spec, _full((D_MODEL, D_MODEL)), _full((D_MODEL, D_MODEL)),
                  _full((1, D_MODEL)), _full((1, D_MODEL))],
        out_specs=rows,
        out_shape=jax.ShapeDtypeStruct((M, D_MODEL), jnp.float32),
        compiler_params=_params("parallel"),
        name="xattn",
    )(x2d, k_mem, v_mem, wq.astype(bf16), wo.astype(bf16), ln_g.reshape(1, -1), ln_b.reshape(1, -1))


FFN_CHUNK = 1408


def _ffn_kernel(seq_tiles, x_ref, xp_ref, wup_ref, cw_ref, cb_ref, wdn_ref, g_ref, b_ref, o_ref):
    bf16, f32 = jnp.bfloat16, jnp.float32
    i = pl.program_id(0)
    tm = x_ref.shape[0]
    H = V7X_SUBLANES
    x = x_ref[...]
    xprev = jnp.where(i % seq_tiles == 0, 0.0, xp_ref[...])
    xe = jnp.concatenate([xprev, x], axis=0).astype(bf16)

    def conv(cols):
        h = jnp.dot(xe, wup_ref[:, cols], preferred_element_type=f32)
        w = cw_ref[:, cols]
        return (h[H - 2:H - 2 + tm] * w[0:1] + h[H - 1:H - 1 + tm] * w[1:2] + h[H:] * w[2:3]
                + cb_ref[:, cols])

    acc = jnp.zeros((tm, D_MODEL), f32)
    for c in range(D_FF // FFN_CHUNK):
        gate = conv(slice(c * FFN_CHUNK, (c + 1) * FFN_CHUNK))
        val = conv(slice(D_FF + c * FFN_CHUNK, D_FF + (c + 1) * FFN_CHUNK))
        act = (gate * jax.nn.sigmoid(gate) * val).astype(bf16)
        acc = acc + jnp.dot(act, wdn_ref[c * FFN_CHUNK:(c + 1) * FFN_CHUNK, :], preferred_element_type=f32)
    o_ref[...] = _layer_norm(ALPHA * x + acc, g_ref[...], b_ref[...])


def _ffn(x2d, w_up, conv_w, conv_b, w_down, ln_g, ln_b, seq_len, tm=512):
    M = x2d.shape[0]
    bf16 = jnp.bfloat16
    seq_tiles = seq_len // tm
    blocks_per_tile = tm // V7X_SUBLANES
    rows = pl.BlockSpec((tm, D_MODEL), lambda i: (i, 0))
    once = lambda shape: pl.BlockSpec(shape, lambda i: (0,) * len(shape), pipeline_mode=pl.Buffered(1))
    return pl.pallas_call(
        functools.partial(_ffn_kernel, seq_tiles),
        grid=(M // tm,),
        in_specs=[rows,
                  pl.BlockSpec((V7X_SUBLANES, D_MODEL), lambda i: (jnp.maximum(i * blocks_per_tile - 1, 0), 0)),
                  once((D_MODEL, 2 * D_FF)), _full((3, 2 * D_FF)), _full((1, 2 * D_FF)),
                  once((D_FF, D_MODEL)), _full((1, D_MODEL)), _full((1, D_MODEL))],
        out_specs=rows,
        out_shape=jax.ShapeDtypeStruct((M, D_MODEL), jnp.float32),
        compiler_params=_params("parallel"),
        name="ffn",
    )(x2d, x2d, w_up.astype(bf16), conv_w, conv_b.reshape(1, -1), w_down.astype(bf16),
      ln_g.reshape(1, -1), ln_b.reshape(1, -1))


def kernel(x, mem, w_in, rwkv_mu, rwkv_w0, rwkv_w2, rwkv_a0, rwkv_a2, rwkv_k_k, rwkv_k_a, rwkv_r_k, rwkv_gn_g, rwkv_gn_b, nsa_pe_k, nsa_pe_v, nsa_ck_w1, nsa_ck_w2, nsa_cv_w1, nsa_cv_w2, merge_p_a, merge_p_b, mix_w_o, ln1_g, ln1_b, xa_wq, xa_wk, xa_wv, xa_wo, ln2_g, ln2_b, ffn_w_up, ffn_conv_w, ffn_conv_b, ffn_w_down, ln3_g, ln3_b):
    B, S, _ = x.shape
    mem_len = mem.shape[1]
    bf16 = jnp.bfloat16
    x2d = x.reshape(B * S, D_MODEL)
    for l in range(DEPTH):
        w = w_in[l]
        w_rwkv = w[:, :RWKV_COLS].astype(bf16)
        w_nsa = _nsa_weight(w[:, RWKV_COLS:RWKV_COLS + NSA_COLS])
        w_gate = w[:, RWKV_COLS + NSA_COLS:].astype(bf16)
        y_a = _rwkv_time_mix(x2d, w_rwkv, rwkv_mu[l], rwkv_w0[l], rwkv_w2[l], rwkv_a0[l], rwkv_a2[l],
                             rwkv_k_k[l], rwkv_k_a[l], rwkv_r_k[l], rwkv_gn_g[l], rwkv_gn_b[l], B, S)
        y_b = _nsa_branch(x2d, w_nsa, nsa_pe_k[l], nsa_pe_v[l], nsa_ck_w1[l], nsa_ck_w2[l],
                          nsa_cv_w1[l], nsa_cv_w2[l], B, S)
        x2d = _merge(x2d, y_a, y_b, w_gate, merge_p_a[l], merge_p_b[l], mix_w_o[l], ln1_g[l], ln1_b[l])
        k_mem, v_mem = _mem_kv(mem.reshape(B * mem_len, D_MODEL), xa_wk[l], xa_wv[l])
        x2d = _xattn(x2d, k_mem, v_mem, xa_wq[l], xa_wo[l], ln2_g[l], ln2_b[l], S, mem_len)
        x2d = _ffn(x2d, ffn_w_up[l], ffn_conv_w[l], ffn_conv_b[l], ffn_w_down[l], ln3_g[l], ln3_b[l], S)
    return x2d.reshape(B, S, D_MODEL)
```

```python
import functools
import math

import jax
import jax.numpy as jnp
import numpy as np
from jax import lax
from jax.experimental import pallas as pl
from jax.experimental.pallas import tpu as pltpu

D_MODEL = 1024
HEAD_DIM = 64
RWKV_DIM = 512
RWKV_HEADS = 8
LORA = 64
RWKV_COLS = 3 * RWKV_DIM + 2 * LORA
GN_EPS = 64e-5
NSA_DIM = 512
NSA_Q_HEADS = 8
NSA_GROUPS = 2
NSA_R = NSA_Q_HEADS // NSA_GROUPS
NSA_KV_DIM = NSA_GROUPS * HEAD_DIM
NSA_COLS = NSA_DIM + 6 * NSA_KV_DIM + 3 * NSA_Q_HEADS
NSA_COLS_PAD = 1408
CMP_BLOCK = 32
CMP_STRIDE = 16
CMP_HIDDEN = 256
SEL_BLOCK = 64
N_SELECT = 16
WINDOW = 512
Q_BLOCK = 128
ROPE_THETA = 500000.0
ROPE_DIM = 16
X_HEADS = 4
X_HEAD_DIM = 256
D_FF = 2816
LN_EPS = 1e-5
DEPTH = 1
ALPHA = (2 * DEPTH) ** 0.25
NEG = -1e30
BIG = 1e30
LOG2E = math.log2(math.e)

V7X_LANES = 128
V7X_SUBLANES = 8
V7X_VMEM_LIMIT_BYTES = 56 * 1024 * 1024

HI = lax.Precision.HIGHEST
RWKV_CHUNK = 64


def _params(*sem):
    return pltpu.CompilerParams(dimension_semantics=sem, vmem_limit_bytes=V7X_VMEM_LIMIT_BYTES)


def _full(shape):
    n = len(shape)
    return pl.BlockSpec(shape, lambda *_: (0,) * n)


def _head_ones(width):
    r = lax.broadcasted_iota(jnp.int32, (width, width), 0) // HEAD_DIM
    c = lax.broadcasted_iota(jnp.int32, (width, width), 1) // HEAD_DIM
    return (r == c).astype(jnp.float32)


def _layer_norm(y, g, b):
    mu = jnp.mean(y, axis=-1, keepdims=True)
    d = y - mu
    var = jnp.mean(d * d, axis=-1, keepdims=True)
    return d * lax.rsqrt(var + LN_EPS) * g + b


def _rwkv_prep_kernel(seq_tiles, x_ref, xp_ref, w_ref, mu_ref, w0_ref, w2_ref, a0_ref, a2_ref,
                      kk_ref, ka_ref, rk_ref,
                      r_out, k_out, v_out, lw_out, a_out, b_out, bonus_out):
    i = pl.program_id(0)
    tm = x_ref.shape[0]
    C = RWKV_DIM
    w = w_ref[...]
    p = jnp.dot(x_ref[...].astype(jnp.bfloat16), w, preferred_element_type=jnp.float32)
    xprev = xp_ref[...].astype(jnp.bfloat16)
    pprev = jnp.dot(xprev, w, preferred_element_type=jnp.float32)[V7X_SUBLANES - 1:V7X_SUBLANES, :]
    pprev = jnp.where(i % seq_tiles == 0, 0.0, pprev)
    row = lax.broadcasted_iota(jnp.int32, (tm, 1), 0)
    shifted = jnp.where(row == 0, pprev, pltpu.roll(p, 1, 0))
    p = p + (shifted - p) * mu_ref[...]
    r, k, v = p[:, :C], p[:, C:2 * C], p[:, 2 * C:3 * C]
    wl = p[:, 3 * C:3 * C + LORA]
    al = p[:, 3 * C + LORA:]
    z = -(w0_ref[...] + jnp.dot(jnp.tanh(wl).astype(jnp.bfloat16), w2_ref[...],
                                preferred_element_type=jnp.float32))
    softplus = jnp.maximum(z, 0.0) + jnp.log(1.0 + jnp.exp(-jnp.abs(z)))
    w_log = -softplus - 0.5
    lw_out[...] = -jnp.exp(w_log)
    a = jax.nn.sigmoid(a0_ref[...] + jnp.dot(al.astype(jnp.bfloat16), a2_ref[...],
                                             preferred_element_type=jnp.float32))
    ones = _head_ones(C).astype(jnp.bfloat16)
    ones2 = jnp.concatenate([ones, ones], axis=0)

    def head_sum(t):
        return jnp.dot(jnp.concatenate(_split(t), axis=1), ones2, preferred_element_type=jnp.float32)

    kk = k * kk_ref[...]
    kk = kk / jnp.maximum(jnp.sqrt(head_sum(kk * kk)), 1e-12)
    kmod = k * (1.0 + (a - 1.0) * ka_ref[...])
    bonus = head_sum(r * kmod * rk_ref[...]) * v
    r_out[...] = r
    k_out[...] = kmod
    v_out[...] = v
    a_out[...] = -kk
    b_out[...] = kk * a
    bonus_out[...] = bonus


def _rwkv_prep(x2d, w_rwkv, mu, w0, w2, a0, a2, k_k, k_a, r_k, seq_len, tm=512):
    M = x2d.shape[0]
    C = RWKV_DIM
    seq_tiles = seq_len // tm
    row = lambda a: a.reshape(1, -1)
    out = jax.ShapeDtypeStruct((M, C), jnp.float32)
    tile = pl.BlockSpec((tm, C), lambda i: (i, 0))
    blocks_per_tile = tm // V7X_SUBLANES
    return pl.pallas_call(
        functools.partial(_rwkv_prep_kernel, seq_tiles),
        grid=(M // tm,),
        in_specs=[
            pl.BlockSpec((tm, D_MODEL), lambda i: (i, 0)),
            pl.BlockSpec((V7X_SUBLANES, D_MODEL), lambda i: (jnp.maximum(i * blocks_per_tile - 1, 0), 0)),
            _full((D_MODEL, RWKV_COLS)), _full((1, RWKV_COLS)), _full((1, C)), _full((LORA, C)),
            _full((1, C)), _full((LORA, C)), _full((1, C)), _full((1, C)), _full((1, C)),
        ],
        out_specs=[tile] * 7,
        out_shape=[out] * 7,
        compiler_params=_params("parallel"),
        name="rwkv_prep",
    )(x2d, x2d, w_rwkv, row(mu), row(w0), w2.astype(jnp.bfloat16), row(a0), a2.astype(jnp.bfloat16),
      row(k_k), row(k_a), row(r_k))


RWKV_GROUP = 4


def _split(x):
    hi = x.astype(jnp.bfloat16)
    return hi, (x - hi.astype(jnp.float32)).astype(jnp.bfloat16)


def _mm(a, b):
    return jnp.dot(a.astype(jnp.bfloat16), b.astype(jnp.bfloat16), preferred_element_type=jnp.float32)


def _rwkv_mix_kernel(r_ref, k_ref, v_ref, lw_ref, a_ref, b_ref, bonus_ref, gng_ref, gnb_ref,
                     y_out, state):
    T, N, GH = RWKV_CHUNK, HEAD_DIM, RWKV_GROUP
    W = GH * N
    f32 = jnp.float32

    @pl.when(pl.program_id(1) == 0)
    def _():
        state[...] = jnp.zeros_like(state)

    ri = lax.broadcasted_iota(jnp.int32, (W, W), 0)
    ci = lax.broadcasted_iota(jnp.int32, (W, W), 1)
    same_head = ri // N == ci // N
    strict, incl = ri > ci, ri >= ci
    eye = (ri == ci).astype(f32)
    tri = (lax.broadcasted_iota(jnp.int32, (T, T), 0)
           >= lax.broadcasted_iota(jnp.int32, (T, T), 1)).astype(jnp.bfloat16)
    mean_w = (_head_ones(RWKV_DIM) * (1.0 / N)).astype(jnp.bfloat16)

    def expand(x):
        return jnp.where(same_head, jnp.concatenate([x] * GH, axis=0), jnp.zeros((), x.dtype))

    def collapse(x):
        out = x[:T]
        for h in range(1, GH):
            out = out + x[h * T:(h + 1) * T]
        return out

    bf = lambda t: t.astype(jnp.bfloat16)
    n_chunks = r_ref.shape[0] // T
    n_groups = RWKV_HEADS // GH
    chains = [(ch, g) for ch in range(n_chunks) for g in range(n_groups)]
    each = lambda fn, *lists: [fn(*args) for args in zip(*lists)]

    def cumsum(x):
        hi, lo = _split(x)
        lo2 = (x - hi.astype(f32) - lo.astype(f32)).astype(jnp.bfloat16)
        return jnp.dot(jnp.concatenate([tri] * 3, axis=1), jnp.concatenate([hi, lo, lo2], axis=0),
                       preferred_element_type=f32)

    c_all = [cumsum(lw_ref[ch * T:(ch + 1) * T, :]) for ch in range(n_chunks)]

    def operands(ch, g):
        rows, cols = slice(ch * T, (ch + 1) * T), slice(g * W, (g + 1) * W)
        lw, c = lw_ref[rows, cols], c_all[ch][:, cols]
        r, k, v = r_ref[rows, cols], k_ref[rows, cols], v_ref[rows, cols]
        a, b = a_ref[rows, cols], b_ref[rows, cols]
        c_last = c[T - 1:T, :]
        e_neg, e_end = jnp.exp(-c), jnp.exp(c_last - c)
        r_t = r * jnp.exp(c)
        return dict(a_x=expand(bf(a * jnp.exp(c - lw))), r_x=expand(bf(r_t)), b_x=expand(bf(b * e_neg)),
                    k_x=expand(bf(k * e_neg)), v_x=expand(bf(v)), r_t=r_t, b_end=b * e_end, k_end=k * e_end,
                    w_end=jnp.exp(c_last))

    ops = [operands(ch, g) for ch, g in chains]
    P = [lax.dot_general(jnp.concatenate([o["a_x"], o["r_x"]], axis=0),
                         jnp.concatenate([o["b_x"], o["k_x"]], axis=0),
                         (((1,), (1,)), ((), ())), preferred_element_type=f32) for o in ops]
    L_ab = [jnp.where(strict, p[:W, :W], 0.0) for p in P]
    L_ak = [bf(jnp.where(strict, p[:W, W:], 0.0)) for p in P]
    M_rb = [bf(jnp.where(incl, p[W:, :W], 0.0)) for p in P]
    M_rk = [bf(jnp.where(incl, p[W:, W:], 0.0)) for p in P]
    v_x = [o["v_x"] for o in ops]

    base = V7X_SUBLANES
    D = [bf(jnp.where(ri // base == ci // base, l, 0.0)) for l in L_ab]
    D2 = each(_mm, D, D)
    D4 = each(_mm, D2, D2)
    X = each(_mm, each(lambda d, d2: _mm(eye + d.astype(f32), eye + d2), D, D2), [eye + d4 for d4 in D4])
    def second_rows(x, blk):
        return jnp.concatenate([x[s:s + blk] for s in range(blk, W, 2 * blk)], axis=0)

    def merge_rows(x, u, blk):
        pieces = []
        for i, s in enumerate(range(0, W, 2 * blk)):
            pieces += [x[s:s + blk], x[s + blk:s + 2 * blk] + u[i * blk:(i + 1) * blk]]
        return jnp.concatenate(pieces, axis=0)

    blk = base
    while blk < T:
        pair = (ri // (2 * blk) == ci // (2 * blk)) & (ri // blk != ci // blk)
        Xb = [bf(x) for x in X]
        lower = [bf(second_rows(x, blk)) for x in X]
        step = each(_mm, each(_mm, lower, [jnp.where(pair, l, 0.0) for l in L_ab]), Xb)
        X = [merge_rows(x, u, blk) for x, u in zip(X, step)]
        blk *= 2

    lakv = each(_mm, L_ak, v_x)
    AV = [bf(_mm(x, jnp.concatenate([o["a_x"], bf(t)], axis=1))) for x, o, t in zip(X, ops, lakv)]
    ry = each(_mm, M_rb, AV)
    mrkv = each(_mm, M_rk, v_x)
    gh = [_mm(expand(o["b_end"]).T, av) for o, av in zip(ops, AV)]
    kv = [_mm(expand(o["k_end"]).T, vx) for o, vx in zip(ops, v_x)]
    rp = [bf(expand(o["r_t"]) + t[:, :W]) for o, t in zip(ops, ry)]
    yp = [t[:, W:] + u for t, u in zip(ry, mrkv)]
    G = [bf(eye * o["w_end"] + t[:, :W]) for o, t in zip(ops, gh)]
    H = [t[:, W:] + u for t, u in zip(gh, kv)]

    y_rows = []
    for ch in range(n_chunks):
        ys = []
        for g in range(n_groups):
            i = chains.index((ch, g))
            s0 = bf(state[g])
            ys.append(collapse(_mm(rp[i], s0) + yp[i]))
            state[g] = _mm(G[i], s0) + H[i]
        y_rows.append(jnp.concatenate(ys, axis=1))
    y = jnp.concatenate(y_rows, axis=0)
    y_hi, y_lo = _split(y)
    ym = jnp.dot(jnp.concatenate([y_hi, y_lo], axis=1), jnp.concatenate([mean_w, mean_w], axis=0),
                 preferred_element_type=f32)
    d = y - ym
    yv = _mm(d * d, mean_w)
    y_out[...] = d * lax.rsqrt(yv + GN_EPS) * gng_ref[...] + gnb_ref[...] + bonus_ref[...]


def _rwkv_mix(r, k, v, lw, a, b, bonus, gn_g, gn_b, batch, chunks_per_step=4):
    M, C = r.shape
    rows = RWKV_CHUNK * chunks_per_step
    steps = M // rows // batch
    W = RWKV_GROUP * HEAD_DIM
    tile = pl.BlockSpec((rows, C), lambda bi, ci: (bi * steps + ci, 0))
    return pl.pallas_call(
        _rwkv_mix_kernel,
        grid=(batch, steps),
        in_specs=[tile] * 7 + [_full((1, C)), _full((1, C))],
        out_specs=tile,
        out_shape=jax.ShapeDtypeStruct((M, C), jnp.float32),
        scratch_shapes=[pltpu.VMEM((RWKV_HEADS // RWKV_GROUP, W, W), jnp.float32)],
        compiler_params=_params("parallel", "arbitrary"),
        name="rwkv_mix",
    )(r, k, v, lw, a, b, bonus, gn_g.reshape(1, C), gn_b.reshape(1, C))


def _rwkv_time_mix(x2d, w_rwkv, mu, w0, w2, a0, a2, k_k, k_a, r_k, gn_g, gn_b, batch, seq_len):
    r, k, v, lw, a, b, bonus = _rwkv_prep(x2d, w_rwkv, mu, w0, w2, a0, a2, k_k, k_a, r_k, seq_len)
    return _rwkv_mix(r, k, v, lw, a, b, bonus, gn_g, gn_b, batch)


def _rope_tables(seq_len):
    half = ROPE_DIM // 2
    inv = ROPE_THETA ** (-np.arange(half, dtype=np.float64) * 2.0 / ROPE_DIM)
    ang = np.arange(seq_len, dtype=np.float64)[:, None] * inv[None, :]
    cos, sin = np.cos(ang).astype(np.float32), np.sin(ang).astype(np.float32)
    pad = np.zeros((seq_len, HEAD_DIM - ROPE_DIM), np.float32)
    zero = np.zeros_like(sin)
    c = np.concatenate([cos, cos, pad + 1.0], axis=1)
    s_lo = np.concatenate([-sin, zero, pad], axis=1)
    s_hi = np.concatenate([zero, sin, pad], axis=1)
    two = lambda t: jnp.asarray(np.concatenate([t, t], axis=1))
    return two(c), two(s_lo), two(s_hi)


def _rope_pair(x, c, s_lo, s_hi):
    return x * c + pltpu.roll(x, V7X_LANES - ROPE_DIM // 2, 1) * s_lo + pltpu.roll(x, ROPE_DIM // 2, 1) * s_hi


GATE_SLOTS = V7X_SUBLANES
V_ROWS = HEAD_DIM + 2 * V7X_SUBLANES


def _nsa_prep_kernel(x_ref, w_ref, c_ref, slo_ref, shi_ref,
                     qT_out, kc_out, vc_out, ks_out, vsT_out, kw_out, vwT_out, gT_out):
    f32 = jnp.float32
    L, N, G, R, Q = V7X_LANES, HEAD_DIM, NSA_GROUPS, NSA_R, Q_BLOCK
    tm = x_ref.shape[0]
    p = jnp.dot(x_ref[...].astype(jnp.bfloat16), w_ref[...], preferred_element_type=f32)
    c, s_lo, s_hi = c_ref[...], slo_ref[...], shi_ref[...]
    rope = lambda t: _rope_pair(t, c, s_lo, s_hi)
    kv = lambda i: p[:, NSA_DIM + i * L:NSA_DIM + (i + 1) * L]

    qT = [(rope(p[:, j * L:(j + 1) * L]) * (N ** -0.5 * LOG2E)).T for j in range(NSA_DIM // L)]
    gT = jax.nn.sigmoid(kv(6)).T
    for g in range(G):
        for qb in range(tm // Q):
            blk = slice(qb * Q, (qb + 1) * Q)
            heads = [qT[(g * R + r) // 2][((g * R + r) % 2) * N:((g * R + r) % 2 + 1) * N, blk]
                     for r in range(R)]
            qT_out[0, g, qb] = jnp.concatenate(heads, axis=1).astype(qT_out.dtype)
            gates = [gT[(g * R + r) * GATE_SLOTS:(g * R + r + 1) * GATE_SLOTS, blk] for r in range(R)]
            gT_out[0, g, qb] = jnp.concatenate(gates, axis=1)

    k_c, k_s, k_w = rope(kv(0)), rope(kv(2)), rope(kv(4))
    v_c = kv(1)
    v_sT, v_wT = kv(3).T, kv(5).T
    slot = (lax.broadcasted_iota(jnp.int32, (tm, L - N), 0) // SEL_BLOCK) % SEL_SLOTS
    onehot = (slot == lax.broadcasted_iota(jnp.int32, (tm, L - N), 1)).astype(f32)
    for g in range(G):
        cols = slice(g * N, (g + 1) * N)
        kc_out[0, g] = k_c[:, cols]
        vc_out[0, g] = v_c[:, cols]
        ks_out[0, g] = jnp.concatenate([k_s[:, cols], onehot], axis=1).astype(ks_out.dtype)
        kw_out[0, g] = k_w[:, cols].astype(kw_out.dtype)
        extra = (lax.broadcasted_iota(jnp.int32, (V_ROWS - N, tm), 0) == 0).astype(f32)
        vsT_out[0, g] = jnp.concatenate([v_sT[cols, :], extra], axis=0).astype(vsT_out.dtype)
        vwT_out[0, g] = jnp.concatenate([v_wT[cols, :], extra], axis=0).astype(vwT_out.dtype)


def _nsa_weight(w):
    n_qkv = NSA_DIM + 6 * NSA_KV_DIM
    gates = w[:, n_qkv:].reshape(-1, NSA_Q_HEADS, 3)
    gates = jnp.pad(gates, ((0, 0), (0, 0), (0, GATE_SLOTS - 3))).reshape(-1, NSA_Q_HEADS * GATE_SLOTS)
    gates = jnp.pad(gates, ((0, 0), (0, NSA_COLS_PAD - n_qkv - NSA_Q_HEADS * GATE_SLOTS)))
    return jnp.concatenate([w[:, :n_qkv], gates], axis=1).astype(jnp.bfloat16)


def _nsa_prep(x2d, w_nsa, batch, seq_len, tm=512):
    B, S, G, N, L, Q = batch, seq_len, NSA_GROUPS, HEAD_DIM, V7X_LANES, Q_BLOCK
    RQ = NSA_R * Q
    tiles = S // tm
    tabs = _rope_tables(S)
    tab_spec = pl.BlockSpec((tm, L), lambda i: (i % tiles, 0))
    f32, bf16 = jnp.float32, jnp.bfloat16
    per_q = lambda rows: pl.BlockSpec((1, G, tm // Q, rows, RQ), lambda i: (i // tiles, 0, i % tiles, 0, 0))
    by_row = lambda w: pl.BlockSpec((1, G, tm, w), lambda i: (i // tiles, 0, i % tiles, 0))
    by_col = pl.BlockSpec((1, G, V_ROWS, tm), lambda i: (i // tiles, 0, 0, i % tiles))
    sds = jax.ShapeDtypeStruct
    return pl.pallas_call(
        _nsa_prep_kernel,
        grid=(B * tiles,),
        in_specs=[pl.BlockSpec((tm, D_MODEL), lambda i: (i, 0)), _full((D_MODEL, NSA_COLS_PAD)),
                  tab_spec, tab_spec, tab_spec],
        out_specs=[per_q(N), by_row(N), by_row(N), by_row(L), by_col, by_row(N), by_col, per_q(GATE_SLOTS)],
        out_shape=[sds((B, G, S // Q, N, RQ), bf16), sds((B, G, S, N), f32), sds((B, G, S, N), f32),
                   sds((B, G, S, L), bf16), sds((B, G, V_ROWS, S), bf16), sds((B, G, S, N), bf16),
                   sds((B, G, V_ROWS, S), bf16), sds((B, G, S // Q, GATE_SLOTS, RQ), f32)],
        compiler_params=_params("parallel"),
        name="nsa_prep",
    )(x2d, w_nsa, *tabs)


def _gelu_tanh(x):
    return 0.5 * x * (1.0 + jnp.tanh(math.sqrt(2.0 / math.pi) * (x + 0.044715 * x * x * x)))


def _nsa_compress_kernel(xk_ref, xv_ref, pek_ref, pev_ref, kw1_ref, kw2_ref, vw1_ref, vw2_ref,
                         kc_out, vcT_out):
    N, bf16, f32 = HEAD_DIM, jnp.bfloat16, jnp.float32
    n_blocks = kc_out.shape[2]

    def mlp(x_ref, pe_ref, w1_ref, w2_ref):
        lo = jnp.zeros((n_blocks, CMP_HIDDEN), f32)
        hi = jnp.zeros((n_blocks, CMP_HIDDEN), f32)
        for l in range(CMP_STRIDE):
            rows = x_ref[0, 0, pl.ds(l, n_blocks, stride=CMP_STRIDE), :]
            lo = lo + jnp.dot((rows + pe_ref[l:l + 1, :]).astype(bf16), w1_ref[l * N:(l + 1) * N, :],
                              preferred_element_type=f32)
            m = CMP_STRIDE + l
            hi = hi + jnp.dot((rows + pe_ref[m:m + 1, :]).astype(bf16), w1_ref[m * N:(m + 1) * N, :],
                              preferred_element_type=f32)
        pre = lo + pltpu.roll(hi, n_blocks - 1, 0)
        return jnp.dot(_gelu_tanh(pre).astype(bf16), w2_ref[...], preferred_element_type=f32)

    kc_out[0, 0] = mlp(xk_ref, pek_ref, kw1_ref, kw2_ref).astype(kc_out.dtype)
    vc = mlp(xv_ref, pev_ref, vw1_ref, vw2_ref)
    vcT_out[0, 0] = jnp.concatenate([vc, jnp.zeros_like(vc)], axis=1).T[:N].astype(vcT_out.dtype)


def _nsa_compress(xk, xv, pe_k, pe_v, ck_w1, ck_w2, cv_w1, cv_w2):
    B, G, S, N = xk.shape
    NC = S // CMP_STRIDE
    bf16 = jnp.bfloat16
    xin = pl.BlockSpec((1, 1, S, N), lambda b, g: (b, g, 0, 0))
    return pl.pallas_call(
        _nsa_compress_kernel,
        grid=(B, G),
        in_specs=[xin, xin, _full((CMP_BLOCK, N)), _full((CMP_BLOCK, N)), _full((CMP_BLOCK * N, CMP_HIDDEN)),
                  _full((CMP_HIDDEN, N)), _full((CMP_BLOCK * N, CMP_HIDDEN)), _full((CMP_HIDDEN, N))],
        out_specs=[pl.BlockSpec((1, 1, NC, N), lambda b, g: (b, g, 0, 0)),
                   pl.BlockSpec((1, 1, N, NC), lambda b, g: (b, g, 0, 0))],
        out_shape=[jax.ShapeDtypeStruct((B, G, NC, N), bf16), jax.ShapeDtypeStruct((B, G, N, NC), bf16)],
        compiler_params=_params("parallel", "parallel"),
        name="nsa_compress",
    )(xk, xv, pe_k, pe_v, ck_w1.astype(bf16), ck_w2.astype(bf16), cv_w1.astype(bf16), cv_w2.astype(bf16))


SEL_KEYS = 512
SEL_SLOTS = V7X_SUBLANES
CMP_TILE = 128
RANK_ROWS = 32
RANK_UNROLL = 8


def _nsa_attn_kernel(qT_ref, kc_ref, vcT_ref, ovT_ref, ks_ref, vsT_ref, kw_ref, vwT_ref, gT_ref,
                     o_ref, val_ref, cnt_ref, sel_ref, qa_ref, s_ref, cm_ref, m_ref, acc_ref, part_ref,
                     imp_ref):
    qb = pl.program_id(2)
    t0 = qb * Q_BLOCK
    R, Q = NSA_R, Q_BLOCK
    RQ = R * Q
    f32, bf16 = jnp.float32, jnp.bfloat16
    qT = qT_ref[0, 0, 0]
    lane = lax.broadcasted_iota(jnp.int32, (1, RQ), 1)
    tpos = t0 + lane % Q
    mm = lambda a, b: jnp.dot(a, b, preferred_element_type=f32)

    g = gT_ref[0, 0, 0]
    NC = kc_ref.shape[2]
    NSB = ovT_ref.shape[0]

    span = WINDOW + Q
    sub = lax.broadcasted_iota(jnp.int32, (Q, 1), 0)
    qpos = lane % Q

    def local_branches(rows, aligned_window):
        w0 = pl.multiple_of(t0 - WINDOW, Q) if aligned_window else 0
        s_cmp = mm(kc_ref[0, 0, :rows, :], qT)
        s_win = mm(kw_ref[0, 0, pl.ds(w0, span), :], qT)

        cend = lax.broadcasted_iota(jnp.int32, (rows, 1), 0) * CMP_STRIDE + (CMP_BLOCK - 1)
        mask = cend <= tpos
        sm = jnp.where(mask, s_cmp, NEG)
        e = jnp.where(mask, jnp.exp2(sm - jnp.max(sm, axis=0, keepdims=True)), 0.0)
        den = jnp.sum(e, axis=0, keepdims=True)
        p = e / jnp.where(den > 0.0, den, 1.0)
        o_c = mm(vcT_ref[0, 0, :, :rows], p.astype(bf16))
        psum = p[:, :Q]
        for r in range(1, R):
            psum = psum + p[:, r * Q:(r + 1) * Q]
        ps_hi, ps_lo = _split(psum)
        ov = ovT_ref[:, :rows]
        imp_ref[...] = mm(jnp.concatenate([ov, ov], axis=1), jnp.concatenate([ps_hi, ps_lo], axis=0))

        if aligned_window:
            head = jnp.where(sub > qpos, s_win[:Q], NEG)
            tail = jnp.where(sub <= qpos, s_win[WINDOW:], NEG)
            sm = jnp.concatenate([head, s_win[Q:WINDOW], tail], axis=0)
        else:
            diff = tpos - lax.broadcasted_iota(jnp.int32, (span, 1), 0)
            sm = jnp.where((diff >= 0) & (diff < WINDOW), s_win, NEG)
        p = jnp.exp2(sm - jnp.max(sm, axis=0, keepdims=True))
        pv = mm(vwT_ref[0, 0, :, pl.ds(w0, span)], p.astype(bf16))
        o_w = pv[:HEAD_DIM] / pv[HEAD_DIM:HEAD_DIM + 1]
        part_ref[...] = g[0:1, :] * o_c + g[2:3, :] * o_w

    tile = min(CMP_TILE, NC)
    per_tile = tile * CMP_STRIDE // Q
    n_tiles = NC // tile
    first_aligned = WINDOW // Q
    for i in range(n_tiles):
        lo, hi = i * per_tile, (i + 1) * per_tile
        cuts = [lo, hi] if not lo < first_aligned < hi else [lo, first_aligned, hi]
        for a, b in zip(cuts[:-1], cuts[1:]):
            in_range = (qb >= a) if (i == n_tiles - 1 and b == hi) else ((qb >= a) & (qb < b))
            pl.when(in_range)(functools.partial(local_branches, (i + 1) * tile, a >= first_aligned))
    imp = imp_ref[...]

    tq = t0 + lax.broadcasted_iota(jnp.int32, (1, Q), 1)
    jblk = lax.broadcasted_iota(jnp.int32, (NSB, 1), 0)
    cur = tq // SEL_BLOCK
    forced = (jblk == 0) | (jblk == cur) | (jblk == cur - 1)
    valid = jblk * SEL_BLOCK <= tq
    val = jnp.where(valid, jnp.where(forced, BIG, imp), NEG)
    val_ref[...] = val
    n_live = (t0 + Q - 1) // SEL_BLOCK + 1

    chunk = min(RANK_ROWS, NSB)
    n_rounds = (n_live + RANK_UNROLL - 1) // RANK_UNROLL
    for k in range(NSB // chunk):
        rows = slice(k * chunk, (k + 1) * chunk)

        @pl.when(k * chunk < n_live)
        def _():
            target = val_ref[rows, :]

            def strict_body(u, cnt):
                base = u * RANK_UNROLL
                for i in range(RANK_UNROLL):
                    cnt = cnt + jnp.where(val_ref[pl.ds(base + i, 1), :] > target, 1, 0)
                return cnt

            cnt_ref[rows, :] = lax.fori_loop(0, n_rounds, strict_body, jnp.zeros((chunk, Q), jnp.int32))

        @pl.when(k * chunk >= n_live)
        def _():
            cnt_ref[rows, :] = jnp.full((chunk, Q), N_SELECT, jnp.int32)

    taken =jnp.sum(jnp.where(valid & (cnt_ref[...] < N_SELECT), 1, 0), axis=0, keepdims=True)

    @pl.when(jnp.max(taken) > N_SELECT)
    def _():
        def tie_body(i, cnt):
            row = val_ref[pl.ds(i, 1), :]
            ge = jnp.where(row >= val, 1, 0)
            gt = jnp.where(row > val, 1, 0)
            return cnt + jnp.where(jblk > i, ge, gt)

        cnt_ref[...] = lax.fori_loop(0, n_live, tie_body, jnp.zeros((NSB, Q), jnp.int32))

    bias = jnp.where(cnt_ref[...] < N_SELECT, 0.0, NEG)
    sel_ref[...] = jnp.concatenate([bias] * R, axis=1)

    bps = SEL_SLOTS
    for buf in range(qa_ref.shape[0]):
        qa_ref[buf, :HEAD_DIM, :] = qT
        qa_ref[buf, HEAD_DIM:, :] = jnp.zeros((qa_ref.shape[1] - HEAD_DIM, RQ), bf16)
    m_ref[...] = jnp.full(m_ref.shape, NEG, f32)
    acc_ref[...] = jnp.zeros(acc_ref.shape, f32)

    def scores(kc, slot):
        k0 = pl.multiple_of(kc * SEL_KEYS, SEL_KEYS)
        grp = pl.multiple_of((kc * (SEL_KEYS // SEL_BLOCK)) // bps * bps, bps)
        rows = sel_ref[pl.ds(grp, bps), :]
        qa_ref[slot, HEAD_DIM:HEAD_DIM + 2 * bps, :] = (
            jnp.concatenate([rows, jnp.zeros_like(rows)], axis=0).astype(bf16))
        s = mm(ks_ref[0, 0, pl.ds(k0, SEL_KEYS), :], qa_ref[slot])
        s_ref[slot] = s
        cm_ref[slot] = jnp.max(s, axis=0, keepdims=True)

    def accumulate(kc, s, cm):
        k0 = pl.multiple_of(kc * SEL_KEYS, SEL_KEYS)
        m = m_ref[...]
        m_new = jnp.maximum(m, cm)
        m_ref[...] = m_new
        p = jnp.exp2(s - m_new).astype(bf16)
        pv = mm(vsT_ref[0, 0, :, pl.ds(k0, s.shape[0])], p)
        acc_ref[...] = jnp.exp2(m - m_new) * acc_ref[...] + pv

    last = (t0 + Q - 1) // SEL_KEYS
    scores(0, 0)

    def sel_body(j, carry):
        scores(2 * j + 1, 1)
        accumulate(2 * j, s_ref[0], cm_ref[0])
        scores(2 * j + 2, 0)
        accumulate(2 * j + 1, s_ref[1], cm_ref[1])
        return carry

    lax.fori_loop(0, last // 2, sel_body, 0)

    @pl.when(last % 2 == 1)
    def _():
        scores(last, 1)
        accumulate(last - 1, s_ref[0], cm_ref[0])

    def diagonal_step(live):
        rows = live * Q
        s = s_ref[last % 2, :rows, :]
        edge = jnp.where(sub <= qpos, s[rows - Q:], NEG)
        s = edge if live == 1 else jnp.concatenate([s[:rows - Q], edge], axis=0)
        accumulate(last, s, jnp.max(s, axis=0, keepdims=True))

    per_step = SEL_KEYS // Q
    for v in range(per_step):
        pl.when(qb % per_step == v)(functools.partial(diagonal_step, v + 1))

    o_s = acc_ref[:HEAD_DIM, :] / acc_ref[HEAD_DIM:HEAD_DIM + 1, :]
    oT = part_ref[...] + g[1:2, :] * o_s
    pairs = [jnp.concatenate([oT[:, (2 * i) * Q:(2 * i + 1) * Q], oT[:, (2 * i + 1) * Q:(2 * i + 2) * Q]],
                             axis=0).T for i in range(R // 2)]
    o_ref[...] = jnp.concatenate(pairs, axis=1)


def _nsa_attention(qT, kc, vcT, ks, vsT, kw, vwT, gT):
    B, G, NQB, _, RQ = qT.shape
    S = ks.shape[2]
    NC = kc.shape[2]
    NSB = S // SEL_BLOCK
    c = np.arange(NC)[None, :] * CMP_STRIDE
    j = np.arange(NSB)[:, None] * SEL_BLOCK
    ovT = jnp.asarray((c <= j + SEL_BLOCK - 1) & (c + CMP_BLOCK - 1 >= j), jnp.bfloat16)
    per_q = lambda rows: pl.BlockSpec((1, 1, 1, rows, RQ), lambda b, g, q: (b, g, q, 0, 0))
    per_g = lambda d0, d1: pl.BlockSpec((1, 1, d0, d1), lambda b, g, q: (b, g, 0, 0))
    return pl.pallas_call(
        _nsa_attn_kernel,
        grid=(B, G, NQB),
        in_specs=[per_q(HEAD_DIM), per_g(NC, HEAD_DIM), per_g(HEAD_DIM, NC), _full((NSB, NC)),
                  per_g(S, V7X_LANES), per_g(V_ROWS, S), per_g(S, HEAD_DIM), per_g(V_ROWS, S),
                  per_q(V7X_SUBLANES)],
        out_specs=pl.BlockSpec((Q_BLOCK, NSA_R * HEAD_DIM), lambda b, g, q: (b * NQB + q, g)),
        out_shape=jax.ShapeDtypeStruct((B * S, NSA_DIM), jnp.float32),
        scratch_shapes=[pltpu.VMEM((NSB, Q_BLOCK), jnp.float32), pltpu.VMEM((NSB, Q_BLOCK), jnp.int32),
                        pltpu.VMEM((NSB, RQ), jnp.float32),
                        pltpu.VMEM((2, V7X_LANES, RQ), jnp.bfloat16),
                        pltpu.VMEM((2, SEL_KEYS, RQ), jnp.float32), pltpu.VMEM((2, 1, RQ), jnp.float32),
                        pltpu.VMEM((1, RQ), jnp.float32),
                        pltpu.VMEM((V_ROWS, RQ), jnp.float32), pltpu.VMEM((HEAD_DIM, RQ), jnp.float32),
                        pltpu.VMEM((NSB, Q_BLOCK), jnp.float32)],
        compiler_params=_params("parallel", "parallel", "arbitrary"),
        name="nsa_attention",
    )(qT, kc, vcT, ovT, ks, vsT, kw, vwT, gT)


def _nsa_branch(x2d, w_nsa, pe_k, pe_v, ck_w1, ck_w2, cv_w1, cv_w2, batch, seq_len):
    qT, kc_in, vc_in, ks, vsT, kw, vwT, gT = _nsa_prep(x2d, w_nsa, batch, seq_len)
    kc, vcT = _nsa_compress(kc_in, vc_in, pe_k, pe_v, ck_w1, ck_w2, cv_w1, cv_w2)
    return _nsa_attention(qT, kc, vcT, ks, vsT, kw, vwT, gT)


def _merge_kernel(x_ref, ya_ref, yb_ref, wg_ref, pa_ref, pb_ref, wo_ref, g_ref, b_ref, o_ref):
    bf16 = jnp.bfloat16
    mm = lambda a, w: jnp.dot(a.astype(bf16), w, preferred_element_type=jnp.float32)
    half = x_ref.shape[0] // 2
    parts = [slice(0, half), slice(half, 2 * half)]
    xs = [x_ref[rows, :] for rows in parts]
    logits = [mm(x, wg_ref[...]) for x in xs]
    pa = [mm(ya_ref[rows, :], pa_ref[...]) for rows in parts]
    pb = [mm(yb_ref[rows, :], pb_ref[...]) for rows in parts]
    mixed = []
    for lg, a, b in zip(logits, pa, pb):
        gates = jax.nn.sigmoid(lg)
        mixed.append(mm(gates[:, :D_MODEL] * a + gates[:, D_MODEL:] * b, wo_ref[...]))
    for rows, x, mx in zip(parts, xs, mixed):
        o_ref[rows, :] = _layer_norm(ALPHA * x + mx, g_ref[...], b_ref[...])


def _merge(x2d, y_a, y_b, w_gate, p_a, p_b, w_o, ln_g, ln_b, tm=512):
    M = x2d.shape[0]
    bf16 = jnp.bfloat16
    rows = lambda w: pl.BlockSpec((tm, w), lambda i: (i, 0))
    return pl.pallas_call(
        _merge_kernel,
        grid=(M // tm,),
        in_specs=[rows(D_MODEL), rows(RWKV_DIM), rows(NSA_DIM), _full((D_MODEL, 2 * D_MODEL)),
                  _full((RWKV_DIM, D_MODEL)), _full((NSA_DIM, D_MODEL)), _full((D_MODEL, D_MODEL)),
                  _full((1, D_MODEL)), _full((1, D_MODEL))],
        out_specs=rows(D_MODEL),
        out_shape=jax.ShapeDtypeStruct((M, D_MODEL), jnp.float32),
        compiler_params=_params("parallel"),
        name="merge",
    )(x2d, y_a, y_b, w_gate, p_a.astype(bf16), p_b.astype(bf16), w_o.astype(bf16),
      ln_g.reshape(1, -1), ln_b.reshape(1, -1))


def _mem_kv_kernel(mem_ref, wk_ref, wv_ref, k_out, v_out):
    m = mem_ref[...].astype(jnp.bfloat16)
    k_out[...] = jnp.dot(m, wk_ref[...], preferred_element_type=jnp.float32).astype(k_out.dtype)
    v_out[...] = jnp.dot(m, wv_ref[...], preferred_element_type=jnp.float32).astype(v_out.dtype)


def _mem_kv(mem2d, wk, wv):
    M = mem2d.shape[0]
    bf16 = jnp.bfloat16
    out = jax.ShapeDtypeStruct((M, D_MODEL), bf16)
    return pl.pallas_call(
        _mem_kv_kernel,
        grid=(1,),
        in_specs=[_full((M, D_MODEL)), _full((D_MODEL, D_MODEL)), _full((D_MODEL, D_MODEL))],
        out_specs=[_full((M, D_MODEL))] * 2,
        out_shape=[out, out],
        compiler_params=_params("arbitrary"),
        name="mem_kv",
    )(mem2d, wk.astype(bf16), wv.astype(bf16))


def _xattn_kernel(x_ref, k_ref, v_ref, wq_ref, wo_ref, g_ref, b_ref, o_ref):
    bf16, f32 = jnp.bfloat16, jnp.float32
    cols = [slice(h * X_HEAD_DIM, (h + 1) * X_HEAD_DIM) for h in range(X_HEADS)]
    half = x_ref.shape[0] // 2
    parts = [slice(0, half), slice(half, 2 * half)]
    xs = [x_ref[rows, :] for rows in parts]
    qs = [jnp.dot(x.astype(bf16), wq_ref[...], preferred_element_type=f32).astype(bf16) for x in xs]
    scores = [[lax.dot_general(q[:, sl], k_ref[:, sl], (((1,), (1,)), ((), ())),
                               preferred_element_type=f32) * (X_HEAD_DIM ** -0.5) for sl in cols]
              for q in qs]
    outs = []
    for per_head in scores:
        heads = []
        for s, sl in zip(per_head, cols):
            p = jnp.exp(s - jnp.max(s, axis=-1, keepdims=True))
            p = p / jnp.sum(p, axis=-1, keepdims=True)
            heads.append(jnp.dot(p.astype(bf16), v_ref[:, sl], preferred_element_type=f32))
        o = jnp.concatenate(heads, axis=1).astype(bf16)
        outs.append(jnp.dot(o, wo_ref[...], preferred_element_type=f32))
    for rows, x, xa in zip(parts, xs, outs):
        o_ref[rows, :] = _layer_norm(ALPHA * x + xa, g_ref[...], b_ref[...])


def _xattn(x2d, k_mem, v_mem, wq, wo, ln_g, ln_b, seq_len, mem_len, tm=512):
    M = x2d.shape[0]
    bf16 = jnp.bfloat16
    seq_tiles = seq_len // tm
    rows = pl.BlockSpec((tm, D_MODEL), lambda i: (i, 0))
    mem_spec = pl.BlockSpec((mem_len, D_MODEL), lambda i: (i // seq_tiles, 0))
    return pl.pallas_call(
        _xattn_kernel,
        grid=(M // tm,),
        in_specs=[rows, mem_spec, mem_spec, _full((D_MODEL, D_MODEL)), _full((D_MODEL, D_MODEL)),
                  _full((1, D_MODEL)), _full((1, D_MODEL))],
        out_specs=rows,
        out_shape=jax.ShapeDtypeStruct((M, D_MODEL), jnp.float32),
        compiler_params=_params("parallel"),
        name="xattn",
    )(x2d, k_mem, v_mem, wq.astype(bf16), wo.astype(bf16), ln_g.reshape(1, -1), ln_b.reshape(1, -1))


FFN_CHUNK = 1408


def _ffn_kernel(seq_tiles, x_ref, xp_ref, wup_ref, cw_ref, cb_ref, wdn_ref, g_ref, b_ref, o_ref):
    bf16, f32 = jnp.bfloat16, jnp.float32
    i = pl.program_id(0)
    tm = x_ref.shape[0]
    H = V7X_SUBLANES
    x = x_ref[...]
    xprev = jnp.where(i % seq_tiles == 0, 0.0, xp_ref[...])
    xe = jnp.concatenate([xprev, x], axis=0).astype(bf16)

    def conv(cols):
        h = jnp.dot(xe, wup_ref[:, cols], preferred_element_type=f32)
        w = cw_ref[:, cols]
        return (h[H - 2:H - 2 + tm] * w[0:1] + h[H - 1:H - 1 + tm] * w[1:2] + h[H:] * w[2:3]
                + cb_ref[:, cols])

    acc = jnp.zeros((tm, D_MODEL), f32)
    for c in range(D_FF // FFN_CHUNK):
        gate = conv(slice(c * FFN_CHUNK, (c + 1) * FFN_CHUNK))
        val = conv(slice(D_FF + c * FFN_CHUNK, D_FF + (c + 1) * FFN_CHUNK))
        act = (gate * jax.nn.sigmoid(gate) * val).astype(bf16)
        acc = acc + jnp.dot(act, wdn_ref[c * FFN_CHUNK:(c + 1) * FFN_CHUNK, :], preferred_element_type=f32)
    o_ref[...] = _layer_norm(ALPHA * x + acc, g_ref[...], b_ref[...])


def _ffn(x2d, w_up, conv_w, conv_b, w_down, ln_g, ln_b, seq_len, tm=512):
    M = x2d.shape[0]
    bf16 = jnp.bfloat16
    seq_tiles = seq_len // tm
    blocks_per_tile = tm // V7X_SUBLANES
    rows = pl.BlockSpec((tm, D_MODEL), lambda i: (i, 0))
    once = lambda shape: pl.BlockSpec(shape, lambda i: (0,) * len(shape), pipeline_mode=pl.Buffered(1))
    return pl.pallas_call(
        functools.partial(_ffn_kernel, seq_tiles),
        grid=(M // tm,),
        in_specs=[rows,
                  pl.BlockSpec((V7X_SUBLANES, D_MODEL), lambda i: (jnp.maximum(i * blocks_per_tile - 1, 0), 0)),
                  once((D_MODEL, 2 * D_FF)), _full((3, 2 * D_FF)), _full((1, 2 * D_FF)),
                  once((D_FF, D_MODEL)), _full((1, D_MODEL)), _full((1, D_MODEL))],
        out_specs=rows,
        out_shape=jax.ShapeDtypeStruct((M, D_MODEL), jnp.float32),
        compiler_params=_params("parallel"),
        name="ffn",
    )(x2d, x2d, w_up.astype(bf16), conv_w, conv_b.reshape(1, -1), w_down.astype(bf16),
      ln_g.reshape(1, -1), ln_b.reshape(1, -1))


def kernel(x, mem, w_in, rwkv_mu, rwkv_w0, rwkv_w2, rwkv_a0, rwkv_a2, rwkv_k_k, rwkv_k_a, rwkv_r_k, rwkv_gn_g, rwkv_gn_b, nsa_pe_k, nsa_pe_v, nsa_ck_w1, nsa_ck_w2, nsa_cv_w1, nsa_cv_w2, merge_p_a, merge_p_b, mix_w_o, ln1_g, ln1_b, xa_wq, xa_wk, xa_wv, xa_wo, ln2_g, ln2_b, ffn_w_up, ffn_conv_w, ffn_conv_b, ffn_w_down, ln3_g, ln3_b):
    B, S, _ = x.shape
    mem_len = mem.shape[1]
    bf16 = jnp.bfloat16
    x2d = x.reshape(B * S, D_MODEL)
    for l in range(DEPTH):
        w = w_in[l]
        w_rwkv = w[:, :RWKV_COLS].astype(bf16)
        w_nsa = _nsa_weight(w[:, RWKV_COLS:RWKV_COLS + NSA_COLS])
        w_gate = w[:, RWKV_COLS + NSA_COLS:].astype(bf16)
        y_a = _rwkv_time_mix(x2d, w_rwkv, rwkv_mu[l], rwkv_w0[l], rwkv_w2[l], rwkv_a0[l], rwkv_a2[l],
                             rwkv_k_k[l], rwkv_k_a[l], rwkv_r_k[l], rwkv_gn_g[l], rwkv_gn_b[l], B, S)
        y_b = _nsa_branch(x2d, w_nsa, nsa_pe_k[l], nsa_pe_v[l], nsa_ck_w1[l], nsa_ck_w2[l],
                          nsa_cv_w1[l], nsa_cv_w2[l], B, S)
        x2d = _merge(x2d, y_a, y_b, w_gate, merge_p_a[l], merge_p_b[l], mix_w_o[l], ln1_g[l], ln1_b[l])
        k_mem, v_mem = _mem_kv(mem.reshape(B * mem_len, D_MODEL), xa_wk[l], xa_wv[l])
        x2d = _xattn(x2d, k_mem, v_mem, xa_wq[l], xa_wo[l], ln2_g[l], ln2_b[l], S, mem_len)
        x2d = _ffn(x2d, ffn_w_up[l], ffn_conv_w[l], ffn_conv_b[l], ffn_w_down[l], ln3_g[l], ln3_b[l], S)
    return x2d.reshape(B, S, D_MODEL)
```

```python
import functools
import math

import jax
import jax.numpy as jnp
import numpy as np
from jax import lax
from jax.experimental import pallas as pl
from jax.experimental.pallas import tpu as pltpu

D_MODEL = 1024
HEAD_DIM = 64
RWKV_DIM = 512
RWKV_HEADS = 8
LORA = 64
RWKV_COLS = 3 * RWKV_DIM + 2 * LORA
GN_EPS = 64e-5
NSA_DIM = 512
NSA_Q_HEADS = 8
NSA_GROUPS = 2
NSA_R = NSA_Q_HEADS // NSA_GROUPS
NSA_KV_DIM = NSA_GROUPS * HEAD_DIM
NSA_COLS = NSA_DIM + 6 * NSA_KV_DIM + 3 * NSA_Q_HEADS
CMP_BLOCK = 32
CMP_STRIDE = 16
CMP_HIDDEN = 256
SEL_BLOCK = 64
N_SELECT = 16
WINDOW = 512
Q_BLOCK = 128
ROPE_THETA = 500000.0
ROPE_DIM = 16
X_HEADS = 4
X_HEAD_DIM = 256
D_FF = 2816
LN_EPS = 1e-5
DEPTH = 1
ALPHA = (2 * DEPTH) ** 0.25
NEG = -1e30
BIG = 1e30
LOG2E = math.log2(math.e)

V7X_LANES = 128
V7X_SUBLANES = 8
V7X_VMEM_LIMIT_BYTES = 56 * 1024 * 1024

NSA_COLS_PAD = NSA_DIM + 6 * NSA_KV_DIM + V7X_LANES

RWKV_CHUNK = 64


def _params(*sem):
    return pltpu.CompilerParams(dimension_semantics=sem, vmem_limit_bytes=V7X_VMEM_LIMIT_BYTES)


def _full(shape):
    n = len(shape)
    return pl.BlockSpec(shape, lambda *_: (0,) * n)


def _head_ones(width):
    r = lax.broadcasted_iota(jnp.int32, (width, width), 0) // HEAD_DIM
    c = lax.broadcasted_iota(jnp.int32, (width, width), 1) // HEAD_DIM
    return (r == c).astype(jnp.float32)


def _layer_norm(y, g, b):
    mu = jnp.mean(y, axis=-1, keepdims=True)
    d = y - mu
    var = jnp.mean(d * d, axis=-1, keepdims=True)
    return d * lax.rsqrt(var + LN_EPS) * g + b


def _rwkv_prep_kernel(seq_tiles, x_ref, xp_ref, w_ref, mu_ref, w0_ref, w2_ref, a0_ref, a2_ref,
                      kk_ref, ka_ref, rk_ref,
                      r_out, k_out, v_out, lw_out, a_out, b_out, bonus_out):
    i = pl.program_id(0)
    tm = x_ref.shape[0]
    C = RWKV_DIM
    w = w_ref[...]
    p = jnp.dot(x_ref[...].astype(jnp.bfloat16), w, preferred_element_type=jnp.float32)
    xprev = xp_ref[...].astype(jnp.bfloat16)
    pprev = jnp.dot(xprev, w, preferred_element_type=jnp.float32)[V7X_SUBLANES - 1:V7X_SUBLANES, :]
    pprev = jnp.where(i % seq_tiles == 0, 0.0, pprev)
    row = lax.broadcasted_iota(jnp.int32, (tm, 1), 0)
    shifted = jnp.where(row == 0, pprev, pltpu.roll(p, 1, 0))
    p = p + (shifted - p) * mu_ref[...]
    r, k, v = p[:, :C], p[:, C:2 * C], p[:, 2 * C:3 * C]
    wl = p[:, 3 * C:3 * C + LORA]
    al = p[:, 3 * C + LORA:]
    z = -(w0_ref[...] + jnp.dot(jnp.tanh(wl).astype(jnp.bfloat16), w2_ref[...],
                                preferred_element_type=jnp.float32))
    softplus = jnp.maximum(z, 0.0) + jnp.log(1.0 + jnp.exp(-jnp.abs(z)))
    w_log = -softplus - 0.5
    lw_out[...] = -jnp.exp(w_log)
    a = jax.nn.sigmoid(a0_ref[...] + jnp.dot(al.astype(jnp.bfloat16), a2_ref[...],
                                             preferred_element_type=jnp.float32))
    ones = _head_ones(C).astype(jnp.bfloat16)
    ones2 = jnp.concatenate([ones, ones], axis=0)

    def head_sum(t):
        return jnp.dot(jnp.concatenate(_split(t), axis=1), ones2, preferred_element_type=jnp.float32)

    kk = k * kk_ref[...]
    kk = kk / jnp.maximum(jnp.sqrt(head_sum(kk * kk)), 1e-12)
    kmod = k * (1.0 + (a - 1.0) * ka_ref[...])
    bonus = head_sum(r * kmod * rk_ref[...]) * v
    r_out[...] = r
    k_out[...] = kmod
    v_out[...] = v
    a_out[...] = -kk
    b_out[...] = kk * a
    bonus_out[...] = bonus


def _rwkv_prep(x2d, w_rwkv, mu, w0, w2, a0, a2, k_k, k_a, r_k, seq_len, tm=512):
    M = x2d.shape[0]
    C = RWKV_DIM
    seq_tiles = seq_len // tm
    row = lambda a: a.reshape(1, -1)
    out = jax.ShapeDtypeStruct((M, C), jnp.float32)
    tile = pl.BlockSpec((tm, C), lambda i: (i, 0))
    blocks_per_tile = tm // V7X_SUBLANES
    return pl.pallas_call(
        functools.partial(_rwkv_prep_kernel, seq_tiles),
        grid=(M // tm,),
        in_specs=[
            pl.BlockSpec((tm, D_MODEL), lambda i: (i, 0)),
            pl.BlockSpec((V7X_SUBLANES, D_MODEL), lambda i: (jnp.maximum(i * blocks_per_tile - 1, 0), 0)),
            _full((D_MODEL, RWKV_COLS)), _full((1, RWKV_COLS)), _full((1, C)), _full((LORA, C)),
            _full((1, C)), _full((LORA, C)), _full((1, C)), _full((1, C)), _full((1, C)),
        ],
        out_specs=[tile] * 7,
        out_shape=[out] * 7,
        compiler_params=_params("parallel"),
        name="rwkv_prep",
    )(x2d, x2d, w_rwkv, row(mu), row(w0), w2.astype(jnp.bfloat16), row(a0), a2.astype(jnp.bfloat16),
      row(k_k), row(k_a), row(r_k))


RWKV_GROUP = 4


def _split(x):
    hi = x.astype(jnp.bfloat16)
    return hi, (x - hi.astype(jnp.float32)).astype(jnp.bfloat16)


def _mm(a, b):
    return jnp.dot(a.astype(jnp.bfloat16), b.astype(jnp.bfloat16), preferred_element_type=jnp.float32)


def _rwkv_mix_kernel(r_ref, k_ref, v_ref, lw_ref, a_ref, b_ref, bonus_ref, gng_ref, gnb_ref,
                     y_out, state):
    T, N, GH = RWKV_CHUNK, HEAD_DIM, RWKV_GROUP
    W = GH * N
    f32 = jnp.float32

    @pl.when(pl.program_id(1) == 0)
    def _():
        state[...] = jnp.zeros_like(state)

    ri = lax.broadcasted_iota(jnp.int32, (W, W), 0)
    ci = lax.broadcasted_iota(jnp.int32, (W, W), 1)
    same_head = ri // N == ci // N
    strict, incl = ri > ci, ri >= ci
    eye = (ri == ci).astype(f32)
    tri = (lax.broadcasted_iota(jnp.int32, (T, T), 0)
           >= lax.broadcasted_iota(jnp.int32, (T, T), 1)).astype(jnp.bfloat16)
    mean_w = (_head_ones(RWKV_DIM) * (1.0 / N)).astype(jnp.bfloat16)

    def expand(x):
        return jnp.where(same_head, jnp.concatenate([x] * GH, axis=0), jnp.zeros((), x.dtype))

    def collapse(x):
        out = x[:T]
        for h in range(1, GH):
            out = out + x[h * T:(h + 1) * T]
        return out

    bf = lambda t: t.astype(jnp.bfloat16)
    n_chunks = r_ref.shape[0] // T
    n_groups = RWKV_HEADS // GH
    chains = [(ch, g) for ch in range(n_chunks) for g in range(n_groups)]
    each = lambda fn, *lists: [fn(*args) for args in zip(*lists)]

    def cumsum(x):
        hi, lo = _split(x)
        lo2 = (x - hi.astype(f32) - lo.astype(f32)).astype(jnp.bfloat16)
        return jnp.dot(jnp.concatenate([tri] * 3, axis=1), jnp.concatenate([hi, lo, lo2], axis=0),
                       preferred_element_type=f32)

    c_all = [cumsum(lw_ref[ch * T:(ch + 1) * T, :]) for ch in range(n_chunks)]

    def operands(ch, g):
        rows, cols = slice(ch * T, (ch + 1) * T), slice(g * W, (g + 1) * W)
        lw, c = lw_ref[rows, cols], c_all[ch][:, cols]
        r, k, v = r_ref[rows, cols], k_ref[rows, cols], v_ref[rows, cols]
        a, b = a_ref[rows, cols], b_ref[rows, cols]
        c_last = c[T - 1:T, :]
        e_neg, e_end = jnp.exp(-c), jnp.exp(c_last - c)
        r_t = r * jnp.exp(c)
        return dict(a_x=expand(bf(a * jnp.exp(c - lw))), r_x=expand(bf(r_t)), b_x=expand(bf(b * e_neg)),
                    k_x=expand(bf(k * e_neg)), v_x=expand(bf(v)), r_t=r_t, b_end=b * e_end, k_end=k * e_end,
                    w_end=jnp.exp(c_last))

    ops = [operands(ch, g) for ch, g in chains]
    P = [lax.dot_general(jnp.concatenate([o["a_x"], o["r_x"]], axis=0),
                         jnp.concatenate([o["b_x"], o["k_x"]], axis=0),
                         (((1,), (1,)), ((), ())), preferred_element_type=f32) for o in ops]
    L_ab = [jnp.where(strict, p[:W, :W], 0.0) for p in P]
    L_ak = [bf(jnp.where(strict, p[:W, W:], 0.0)) for p in P]
    M_rb = [bf(jnp.where(incl, p[W:, :W], 0.0)) for p in P]
    M_rk = [bf(jnp.where(incl, p[W:, W:], 0.0)) for p in P]
    v_x = [o["v_x"] for o in ops]

    base = V7X_SUBLANES
    D = [bf(jnp.where(ri // base == ci // base, l, 0.0)) for l in L_ab]
    D2 = each(_mm, D, D)
    D4 = each(_mm, D2, D2)
    X = each(_mm, each(lambda d, d2: _mm(eye + d.astype(f32), eye + d2), D, D2), [eye + d4 for d4 in D4])
    def second_rows(x, blk):
        return jnp.concatenate([x[s:s + blk] for s in range(blk, W, 2 * blk)], axis=0)

    def merge_rows(x, u, blk):
        pieces = []
        for i, s in enumerate(range(0, W, 2 * blk)):
            pieces += [x[s:s + blk], x[s + blk:s + 2 * blk] + u[i * blk:(i + 1) * blk]]
        return jnp.concatenate(pieces, axis=0)

    blk = base
    while blk < T:
        pair = (ri // (2 * blk) == ci // (2 * blk)) & (ri // blk != ci // blk)
        Xb = [bf(x) for x in X]
        lower = [bf(second_rows(x, blk)) for x in X]
        step = each(_mm, each(_mm, lower, [jnp.where(pair, l, 0.0) for l in L_ab]), Xb)
        X = [merge_rows(x, u, blk) for x, u in zip(X, step)]
        blk *= 2

    lakv = each(_mm, L_ak, v_x)
    AV = [bf(_mm(x, jnp.concatenate([o["a_x"], bf(t)], axis=1))) for x, o, t in zip(X, ops, lakv)]
    ry = each(_mm, M_rb, AV)
    mrkv = each(_mm, M_rk, v_x)
    gh = [_mm(expand(o["b_end"]).T, av) for o, av in zip(ops, AV)]
    kv = [_mm(expand(o["k_end"]).T, vx) for o, vx in zip(ops, v_x)]
    rp = [bf(expand(o["r_t"]) + t[:, :W]) for o, t in zip(ops, ry)]
    yp = [t[:, W:] + u for t, u in zip(ry, mrkv)]
    G = [bf(eye * o["w_end"] + t[:, :W]) for o, t in zip(ops, gh)]
    H = [t[:, W:] + u for t, u in zip(gh, kv)]

    y_rows = []
    for ch in range(n_chunks):
        ys = []
        for g in range(n_groups):
            i = chains.index((ch, g))
            s0 = bf(state[g])
            ys.append(collapse(_mm(rp[i], s0) + yp[i]))
            state[g] = _mm(G[i], s0) + H[i]
        y_rows.append(jnp.concatenate(ys, axis=1))
    y = jnp.concatenate(y_rows, axis=0)
    y_hi, y_lo = _split(y)
    ym = jnp.dot(jnp.concatenate([y_hi, y_lo], axis=1), jnp.concatenate([mean_w, mean_w], axis=0),
                 preferred_element_type=f32)
    d = y - ym
    yv = _mm(d * d, mean_w)
    y_out[...] = d * lax.rsqrt(yv + GN_EPS) * gng_ref[...] + gnb_ref[...] + bonus_ref[...]


def _rwkv_mix(r, k, v, lw, a, b, bonus, gn_g, gn_b, batch, chunks_per_step=4):
    M, C = r.shape
    rows = RWKV_CHUNK * chunks_per_step
    steps = M // rows // batch
    W = RWKV_GROUP * HEAD_DIM
    tile = pl.BlockSpec((rows, C), lambda bi, ci: (bi * steps + ci, 0))
    return pl.pallas_call(
        _rwkv_mix_kernel,
        grid=(batch, steps),
        in_specs=[tile] * 7 + [_full((1, C)), _full((1, C))],
        out_specs=tile,
        out_shape=jax.ShapeDtypeStruct((M, C), jnp.float32),
        scratch_shapes=[pltpu.VMEM((RWKV_HEADS // RWKV_GROUP, W, W), jnp.float32)],
        compiler_params=_params("parallel", "arbitrary"),
        name="rwkv_mix",
    )(r, k, v, lw, a, b, bonus, gn_g.reshape(1, C), gn_b.reshape(1, C))


def _rwkv_time_mix(x2d, w_rwkv, mu, w0, w2, a0, a2, k_k, k_a, r_k, gn_g, gn_b, batch, seq_len):
    r, k, v, lw, a, b, bonus = _rwkv_prep(x2d, w_rwkv, mu, w0, w2, a0, a2, k_k, k_a, r_k, seq_len)
    return _rwkv_mix(r, k, v, lw, a, b, bonus, gn_g, gn_b, batch)


def _rope_tables(seq_len):
    half = ROPE_DIM // 2
    inv = ROPE_THETA ** (-np.arange(half, dtype=np.float64) * 2.0 / ROPE_DIM)
    ang = np.arange(seq_len, dtype=np.float64)[:, None] * inv[None, :]
    cos, sin = np.cos(ang).astype(np.float32), np.sin(ang).astype(np.float32)
    pad = np.zeros((seq_len, HEAD_DIM - ROPE_DIM), np.float32)
    zero = np.zeros_like(sin)
    c = np.concatenate([cos, cos, pad + 1.0], axis=1)
    s_lo = np.concatenate([-sin, zero, pad], axis=1)
    s_hi = np.concatenate([zero, sin, pad], axis=1)
    two = lambda t: jnp.asarray(np.concatenate([t, t], axis=1))
    return two(c), two(s_lo), two(s_hi)


def _rope_pair(x, c, s_lo, s_hi):
    return x * c + pltpu.roll(x, V7X_LANES - ROPE_DIM // 2, 1) * s_lo + pltpu.roll(x, ROPE_DIM // 2, 1) * s_hi


GATE_SLOTS = V7X_SUBLANES
V_ROWS = HEAD_DIM + 2 * V7X_SUBLANES


def _nsa_prep_kernel(x_ref, w_ref, c_ref, slo_ref, shi_ref,
                     qT_out, kc_out, vc_out, ks_out, vsT_out, kw_out, vwT_out, gT_out):
    f32 = jnp.float32
    L, N, G, R, Q = V7X_LANES, HEAD_DIM, NSA_GROUPS, NSA_R, Q_BLOCK
    tm = x_ref.shape[0]
    p = jnp.dot(x_ref[...].astype(jnp.bfloat16), w_ref[...], preferred_element_type=f32)
    c, s_lo, s_hi = c_ref[...], slo_ref[...], shi_ref[...]
    rope = lambda t: _rope_pair(t, c, s_lo, s_hi)
    kv = lambda i: p[:, NSA_DIM + i * L:NSA_DIM + (i + 1) * L]

    qT = [(rope(p[:, j * L:(j + 1) * L]) * (N ** -0.5 * LOG2E)).T for j in range(NSA_DIM // L)]
    gT = jax.nn.sigmoid(kv(6)).T
    for g in range(G):
        for qb in range(tm // Q):
            blk = slice(qb * Q, (qb + 1) * Q)
            heads = [qT[(g * R + r) // 2][((g * R + r) % 2) * N:((g * R + r) % 2 + 1) * N, blk]
                     for r in range(R)]
            qT_out[0, g, qb] = jnp.concatenate(heads, axis=1).astype(qT_out.dtype)
            gates = [gT[(g * R + r) * GATE_SLOTS:(g * R + r + 1) * GATE_SLOTS, blk] for r in range(R)]
            gT_out[0, g, qb] = jnp.concatenate(gates, axis=1)

    k_c, k_s, k_w = rope(kv(0)), rope(kv(2)), rope(kv(4))
    v_c = kv(1)
    v_sT, v_wT = kv(3).T, kv(5).T
    slot = (lax.broadcasted_iota(jnp.int32, (tm, L - N), 0) // SEL_BLOCK) % SEL_SLOTS
    onehot = (slot == lax.broadcasted_iota(jnp.int32, (tm, L - N), 1)).astype(f32)
    for g in range(G):
        cols = slice(g * N, (g + 1) * N)
        kc_out[0, g] = k_c[:, cols]
        vc_out[0, g] = v_c[:, cols]
        ks_out[0, g] = jnp.concatenate([k_s[:, cols], onehot], axis=1).astype(ks_out.dtype)
        kw_out[0, g] = k_w[:, cols].astype(kw_out.dtype)
        extra = (lax.broadcasted_iota(jnp.int32, (V_ROWS - N, tm), 0) == 0).astype(f32)
        vsT_out[0, g] = jnp.concatenate([v_sT[cols, :], extra], axis=0).astype(vsT_out.dtype)
        vwT_out[0, g] = jnp.concatenate([v_wT[cols, :], extra], axis=0).astype(vwT_out.dtype)


def _nsa_weight(w):
    n_qkv = NSA_DIM + 6 * NSA_KV_DIM
    gates = w[:, n_qkv:].reshape(-1, NSA_Q_HEADS, 3)
    gates = jnp.pad(gates, ((0, 0), (0, 0), (0, GATE_SLOTS - 3))).reshape(-1, NSA_Q_HEADS * GATE_SLOTS)
    gates = jnp.pad(gates, ((0, 0), (0, NSA_COLS_PAD - n_qkv - NSA_Q_HEADS * GATE_SLOTS)))
    return jnp.concatenate([w[:, :n_qkv], gates], axis=1).astype(jnp.bfloat16)


def _nsa_prep(x2d, w_nsa, batch, seq_len, tm=512):
    B, S, G, N, L, Q = batch, seq_len, NSA_GROUPS, HEAD_DIM, V7X_LANES, Q_BLOCK
    RQ = NSA_R * Q
    tiles = S // tm
    tabs = _rope_tables(S)
    tab_spec = pl.BlockSpec((tm, L), lambda i: (i % tiles, 0))
    f32, bf16 = jnp.float32, jnp.bfloat16
    per_q = lambda rows: pl.BlockSpec((1, G, tm // Q, rows, RQ), lambda i: (i // tiles, 0, i % tiles, 0, 0))
    by_row = lambda w: pl.BlockSpec((1, G, tm, w), lambda i: (i // tiles, 0, i % tiles, 0))
    by_col = pl.BlockSpec((1, G, V_ROWS, tm), lambda i: (i // tiles, 0, 0, i % tiles))
    sds = jax.ShapeDtypeStruct
    return pl.pallas_call(
        _nsa_prep_kernel,
        grid=(B * tiles,),
        in_specs=[pl.BlockSpec((tm, D_MODEL), lambda i: (i, 0)), _full((D_MODEL, NSA_COLS_PAD)),
                  tab_spec, tab_spec, tab_spec],
        out_specs=[per_q(N), by_row(N), by_row(N), by_row(L), by_col, by_row(N), by_col, per_q(GATE_SLOTS)],
        out_shape=[sds((B, G, S // Q, N, RQ), bf16), sds((B, G, S, N), f32), sds((B, G, S, N), f32),
                   sds((B, G, S, L), bf16), sds((B, G, V_ROWS, S), bf16), sds((B, G, S, N), bf16),
                   sds((B, G, V_ROWS, S), bf16), sds((B, G, S // Q, GATE_SLOTS, RQ), f32)],
        compiler_params=_params("parallel"),
        name="nsa_prep",
    )(x2d, w_nsa, *tabs)


def _gelu_tanh(x):
    return 0.5 * x * (1.0 + jnp.tanh(math.sqrt(2.0 / math.pi) * (x + 0.044715 * x * x * x)))


def _nsa_compress_kernel(xk_ref, xv_ref, pek_ref, pev_ref, kw1_ref, kw2_ref, vw1_ref, vw2_ref,
                         kc_out, vcT_out):
    N, bf16, f32 = HEAD_DIM, jnp.bfloat16, jnp.float32
    n_blocks = kc_out.shape[2]

    def mlp(x_ref, pe_ref, w1_ref, w2_ref):
        lo = jnp.zeros((n_blocks, CMP_HIDDEN), f32)
        hi = jnp.zeros((n_blocks, CMP_HIDDEN), f32)
        for l in range(CMP_STRIDE):
            rows = x_ref[0, 0, pl.ds(l, n_blocks, stride=CMP_STRIDE), :]
            lo = lo + jnp.dot((rows + pe_ref[l:l + 1, :]).astype(bf16), w1_ref[l * N:(l + 1) * N, :],
                              preferred_element_type=f32)
            m = CMP_STRIDE + l
            hi = hi + jnp.dot((rows + pe_ref[m:m + 1, :]).astype(bf16), w1_ref[m * N:(m + 1) * N, :],
                              preferred_element_type=f32)
        pre = lo + pltpu.roll(hi, n_blocks - 1, 0)
        return jnp.dot(_gelu_tanh(pre).astype(bf16), w2_ref[...], preferred_element_type=f32)

    kc_out[0, 0] = mlp(xk_ref, pek_ref, kw1_ref, kw2_ref).astype(kc_out.dtype)
    vc = mlp(xv_ref, pev_ref, vw1_ref, vw2_ref)
    vcT_out[0, 0] = jnp.concatenate([vc, jnp.zeros_like(vc)], axis=1).T[:N].astype(vcT_out.dtype)


def _nsa_compress(xk, xv, pe_k, pe_v, ck_w1, ck_w2, cv_w1, cv_w2):
    B, G, S, N = xk.shape
    NC = S // CMP_STRIDE
    bf16 = jnp.bfloat16
    xin = pl.BlockSpec((1, 1, S, N), lambda b, g: (b, g, 0, 0))
    return pl.pallas_call(
        _nsa_compress_kernel,
        grid=(B, G),
        in_specs=[xin, xin, _full((CMP_BLOCK, N)), _full((CMP_BLOCK, N)), _full((CMP_BLOCK * N, CMP_HIDDEN)),
                  _full((CMP_HIDDEN, N)), _full((CMP_BLOCK * N, CMP_HIDDEN)), _full((CMP_HIDDEN, N))],
        out_specs=[pl.BlockSpec((1, 1, NC, N), lambda b, g: (b, g, 0, 0)),
                   pl.BlockSpec((1, 1, N, NC), lambda b, g: (b, g, 0, 0))],
        out_shape=[jax.ShapeDtypeStruct((B, G, NC, N), bf16), jax.ShapeDtypeStruct((B, G, N, NC), bf16)],
        compiler_params=_params("parallel", "parallel"),
        name="nsa_compress",
    )(xk, xv, pe_k, pe_v, ck_w1.astype(bf16), ck_w2.astype(bf16), cv_w1.astype(bf16), cv_w2.astype(bf16))


SEL_KEYS = 512
SEL_SLOTS = V7X_SUBLANES
CMP_TILE = 128
RANK_ROWS = 32
RANK_UNROLL = 8


def _nsa_attn_kernel(qT_ref, kc_ref, vcT_ref, ovT_ref, ks_ref, vsT_ref, kw_ref, vwT_ref, gT_ref,
                     o_ref, val_ref, cnt_ref, sel_ref, qa_ref, s_ref, cm_ref, m_ref, acc_ref, part_ref,
                     imp_ref):
    qb = pl.program_id(2)
    t0 = qb * Q_BLOCK
    R, Q = NSA_R, Q_BLOCK
    RQ = R * Q
    f32, bf16 = jnp.float32, jnp.bfloat16
    qT = qT_ref[0, 0, 0]
    lane = lax.broadcasted_iota(jnp.int32, (1, RQ), 1)
    tpos = t0 + lane % Q
    mm = lambda a, b: jnp.dot(a, b, preferred_element_type=f32)

    g = gT_ref[0, 0, 0]
    NC = kc_ref.shape[2]
    NSB = ovT_ref.shape[0]

    span = WINDOW + Q
    sub = lax.broadcasted_iota(jnp.int32, (Q, 1), 0)
    qpos = lane % Q

    def local_branches(rows, aligned_window):
        w0 = pl.multiple_of(t0 - WINDOW, Q) if aligned_window else 0
        s_cmp = mm(kc_ref[0, 0, :rows, :], qT)
        s_win = mm(kw_ref[0, 0, pl.ds(w0, span), :], qT)

        cend = lax.broadcasted_iota(jnp.int32, (rows, 1), 0) * CMP_STRIDE + (CMP_BLOCK - 1)
        mask = cend <= tpos
        sm = jnp.where(mask, s_cmp, NEG)
        e = jnp.where(mask, jnp.exp2(sm - jnp.max(sm, axis=0, keepdims=True)), 0.0)
        den = jnp.sum(e, axis=0, keepdims=True)
        p = e / jnp.where(den > 0.0, den, 1.0)
        o_c = mm(vcT_ref[0, 0, :, :rows], p.astype(bf16))
        psum = p[:, :Q]
        for r in range(1, R):
            psum = psum + p[:, r * Q:(r + 1) * Q]
        ps_hi, ps_lo = _split(psum)
        ov = ovT_ref[:, :rows]
        imp_ref[...] = mm(jnp.concatenate([ov, ov], axis=1), jnp.concatenate([ps_hi, ps_lo], axis=0))

        if aligned_window:
            head = jnp.where(sub > qpos, s_win[:Q], NEG)
            tail = jnp.where(sub <= qpos, s_win[WINDOW:], NEG)
            sm = jnp.concatenate([head, s_win[Q:WINDOW], tail], axis=0)
        else:
            diff = tpos - lax.broadcasted_iota(jnp.int32, (span, 1), 0)
            sm = jnp.where((diff >= 0) & (diff < WINDOW), s_win, NEG)
        p = jnp.exp2(sm - jnp.max(sm, axis=0, keepdims=True))
        pv = mm(vwT_ref[0, 0, :, pl.ds(w0, span)], p.astype(bf16))
        o_w = pv[:HEAD_DIM] / pv[HEAD_DIM:HEAD_DIM + 1]
        part_ref[...] = g[0:1, :] * o_c + g[2:3, :] * o_w

    tile = min(CMP_TILE, NC)
    per_tile = tile * CMP_STRIDE // Q
    n_tiles = NC // tile
    first_aligned = WINDOW // Q
    for i in range(n_tiles):
        lo, hi = i * per_tile, (i + 1) * per_tile
        cuts = [lo, hi] if not lo < first_aligned < hi else [lo, first_aligned, hi]
        for a, b in zip(cuts[:-1], cuts[1:]):
            in_range = (qb >= a) if (i == n_tiles - 1 and b == hi) else ((qb >= a) & (qb < b))
            pl.when(in_range)(functools.partial(local_branches, (i + 1) * tile, a >= first_aligned))
    imp = imp_ref[...]

    tq = t0 + lax.broadcasted_iota(jnp.int32, (1, Q), 1)
    jblk = lax.broadcasted_iota(jnp.int32, (NSB, 1), 0)
    cur = tq // SEL_BLOCK
    forced = (jblk == 0) | (jblk == cur) | (jblk == cur - 1)
    valid = jblk * SEL_BLOCK <= tq
    val = jnp.where(valid, jnp.where(forced, BIG, imp), NEG)
    val_ref[...] = val
    n_live = (t0 + Q - 1) // SEL_BLOCK + 1

    chunk = min(RANK_ROWS, NSB)
    n_rounds = (n_live + RANK_UNROLL - 1) // RANK_UNROLL
    for k in range(NSB // chunk):
        rows = slice(k * chunk, (k + 1) * chunk)

        @pl.when(k * chunk < n_live)
        def _():
            target = val_ref[rows, :]

            def strict_body(u, cnt):
                base = u * RANK_UNROLL
                for i in range(RANK_UNROLL):
                    cnt = cnt + jnp.where(val_ref[pl.ds(base + i, 1), :] > target, 1, 0)
                return cnt

            cnt_ref[rows, :] = lax.fori_loop(0, n_rounds, strict_body, jnp.zeros((chunk, Q), jnp.int32))

        @pl.when(k * chunk >= n_live)
        def _():
            cnt_ref[rows, :] = jnp.full((chunk, Q), N_SELECT, jnp.int32)

    taken =jnp.sum(jnp.where(valid & (cnt_ref[...] < N_SELECT), 1, 0), axis=0, keepdims=True)

    @pl.when(jnp.max(taken) > N_SELECT)
    def _():
        def tie_body(i, cnt):
            row = val_ref[pl.ds(i, 1), :]
            ge = jnp.where(row >= val, 1, 0)
            gt = jnp.where(row > val, 1, 0)
            return cnt + jnp.where(jblk > i, ge, gt)

        cnt_ref[...] = lax.fori_loop(0, n_live, tie_body, jnp.zeros((NSB, Q), jnp.int32))

    bias = jnp.where(cnt_ref[...] < N_SELECT, 0.0, NEG)
    sel_ref[...] = jnp.concatenate([bias] * R, axis=1)

    bps = SEL_SLOTS
    for buf in range(qa_ref.shape[0]):
        qa_ref[buf, :HEAD_DIM, :] = qT
        qa_ref[buf, HEAD_DIM:, :] = jnp.zeros((qa_ref.shape[1] - HEAD_DIM, RQ), bf16)
    m_ref[...] = jnp.full(m_ref.shape, NEG, f32)
    acc_ref[...] = jnp.zeros(acc_ref.shape, f32)

    def scores(kc, slot):
        k0 = pl.multiple_of(kc * SEL_KEYS, SEL_KEYS)
        grp = pl.multiple_of((kc * (SEL_KEYS // SEL_BLOCK)) // bps * bps, bps)
        rows = sel_ref[pl.ds(grp, bps), :]
        qa_ref[slot, HEAD_DIM:HEAD_DIM + 2 * bps, :] = (
            jnp.concatenate([rows, jnp.zeros_like(rows)], axis=0).astype(bf16))
        s = mm(ks_ref[0, 0, pl.ds(k0, SEL_KEYS), :], qa_ref[slot])
        s_ref[slot] = s
        cm_ref[slot] = jnp.max(s, axis=0, keepdims=True)

    def accumulate(kc, s, cm):
        k0 = pl.multiple_of(kc * SEL_KEYS, SEL_KEYS)
        m = m_ref[...]
        m_new = jnp.maximum(m, cm)
        m_ref[...] = m_new
        p = jnp.exp2(s - m_new).astype(bf16)
        pv = mm(vsT_ref[0, 0, :, pl.ds(k0, s.shape[0])], p)
        acc_ref[...] = jnp.exp2(m - m_new) * acc_ref[...] + pv

    last = (t0 + Q - 1) // SEL_KEYS
    scores(0, 0)

    def sel_body(j, carry):
        scores(2 * j + 1, 1)
        accumulate(2 * j, s_ref[0], cm_ref[0])
        scores(2 * j + 2, 0)
        accumulate(2 * j + 1, s_ref[1], cm_ref[1])
        return carry

    lax.fori_loop(0, last // 2, sel_body, 0)

    @pl.when(last % 2 == 1)
    def _():
        scores(last, 1)
        accumulate(last - 1, s_ref[0], cm_ref[0])

    def diagonal_step(live):
        rows = live * Q
        s = s_ref[last % 2, :rows, :]
        edge = jnp.where(sub <= qpos, s[rows - Q:], NEG)
        s = edge if live == 1 else jnp.concatenate([s[:rows - Q], edge], axis=0)
        accumulate(last, s, jnp.max(s, axis=0, keepdims=True))

    per_step = SEL_KEYS // Q
    for v in range(per_step):
        pl.when(qb % per_step == v)(functools.partial(diagonal_step, v + 1))

    o_s = acc_ref[:HEAD_DIM, :] / acc_ref[HEAD_DIM:HEAD_DIM + 1, :]
    oT = part_ref[...] + g[1:2, :] * o_s
    pairs = [jnp.concatenate([oT[:, (2 * i) * Q:(2 * i + 1) * Q], oT[:, (2 * i + 1) * Q:(2 * i + 2) * Q]],
                             axis=0).T for i in range(R // 2)]
    o_ref[...] = jnp.concatenate(pairs, axis=1)


def _nsa_attention(qT, kc, vcT, ks, vsT, kw, vwT, gT):
    B, G, NQB, _, RQ = qT.shape
    S = ks.shape[2]
    NC = kc.shape[2]
    NSB = S // SEL_BLOCK
    c = np.arange(NC)[None, :] * CMP_STRIDE
    j = np.arange(NSB)[:, None] * SEL_BLOCK
    ovT = jnp.asarray((c <= j + SEL_BLOCK - 1) & (c + CMP_BLOCK - 1 >= j), jnp.bfloat16)
    per_q = lambda rows: pl.BlockSpec((1, 1, 1, rows, RQ), lambda b, g, q: (b, g, q, 0, 0))
    per_g = lambda d0, d1: pl.BlockSpec((1, 1, d0, d1), lambda b, g, q: (b, g, 0, 0))
    return pl.pallas_call(
        _nsa_attn_kernel,
        grid=(B, G, NQB),
        in_specs=[per_q(HEAD_DIM), per_g(NC, HEAD_DIM), per_g(HEAD_DIM, NC), _full((NSB, NC)),
                  per_g(S, V7X_LANES), per_g(V_ROWS, S), per_g(S, HEAD_DIM), per_g(V_ROWS, S),
                  per_q(V7X_SUBLANES)],
        out_specs=pl.BlockSpec((Q_BLOCK, NSA_R * HEAD_DIM), lambda b, g, q: (b * NQB + q, g)),
        out_shape=jax.ShapeDtypeStruct((B * S, NSA_DIM), jnp.float32),
        scratch_shapes=[pltpu.VMEM((NSB, Q_BLOCK), jnp.float32), pltpu.VMEM((NSB, Q_BLOCK), jnp.int32),
                        pltpu.VMEM((NSB, RQ), jnp.float32),
                        pltpu.VMEM((2, V7X_LANES, RQ), jnp.bfloat16),
                        pltpu.VMEM((2, SEL_KEYS, RQ), jnp.float32), pltpu.VMEM((2, 1, RQ), jnp.float32),
                        pltpu.VMEM((1, RQ), jnp.float32),
                        pltpu.VMEM((V_ROWS, RQ), jnp.float32), pltpu.VMEM((HEAD_DIM, RQ), jnp.float32),
                        pltpu.VMEM((NSB, Q_BLOCK), jnp.float32)],
        compiler_params=_params("parallel", "parallel", "arbitrary"),
        name="nsa_attention",
    )(qT, kc, vcT, ovT, ks, vsT, kw, vwT, gT)


def _nsa_branch(x2d, w_nsa, pe_k, pe_v, ck_w1, ck_w2, cv_w1, cv_w2, batch, seq_len):
    qT, kc_in, vc_in, ks, vsT, kw, vwT, gT = _nsa_prep(x2d, w_nsa, batch, seq_len)
    kc, vcT = _nsa_compress(kc_in, vc_in, pe_k, pe_v, ck_w1, ck_w2, cv_w1, cv_w2)
    return _nsa_attention(qT, kc, vcT, ks, vsT, kw, vwT, gT)


def _merge_kernel(x_ref, ya_ref, yb_ref, wg_ref, pa_ref, pb_ref, wo_ref, g_ref, b_ref, o_ref):
    bf16 = jnp.bfloat16
    mm = lambda a, w: jnp.dot(a.astype(bf16), w, preferred_element_type=jnp.float32)
    half = x_ref.shape[0] // 2
    parts = [slice(0, half), slice(half, 2 * half)]
    xs = [x_ref[rows, :] for rows in parts]
    logits = [mm(x, wg_ref[...]) for x in xs]
    pa = [mm(ya_ref[rows, :], pa_ref[...]) for rows in parts]
    pb = [mm(yb_ref[rows, :], pb_ref[...]) for rows in parts]
    mixed = []
    for lg, a, b in zip(logits, pa, pb):
        gates = jax.nn.sigmoid(lg)
        mixed.append(mm(gates[:, :D_MODEL] * a + gates[:, D_MODEL:] * b, wo_ref[...]))
    for rows, x, mx in zip(parts, xs, mixed):
        o_ref[rows, :] = _layer_norm(ALPHA * x + mx, g_ref[...], b_ref[...])


def _merge(x2d, y_a, y_b, w_gate, p_a, p_b, w_o, ln_g, ln_b, tm=512):
    M = x2d.shape[0]
    bf16 = jnp.bfloat16
    rows = lambda w: pl.BlockSpec((tm, w), lambda i: (i, 0))
    return pl.pallas_call(
        _merge_kernel,
        grid=(M // tm,),
        in_specs=[rows(D_MODEL), rows(RWKV_DIM), rows(NSA_DIM), _full((D_MODEL, 2 * D_MODEL)),
                  _full((RWKV_DIM, D_MODEL)), _full((NSA_DIM, D_MODEL)), _full((D_MODEL, D_MODEL)),
                  _full((1, D_MODEL)), _full((1, D_MODEL))],
        out_specs=rows(D_MODEL),
        out_shape=jax.ShapeDtypeStruct((M, D_MODEL), jnp.float32),
        compiler_params=_params("parallel"),
        name="merge",
    )(x2d, y_a, y_b, w_gate, p_a.astype(bf16), p_b.astype(bf16), w_o.astype(bf16),
      ln_g.reshape(1, -1), ln_b.reshape(1, -1))


def _mem_kv_kernel(mem_ref, wk_ref, wv_ref, k_out, v_out):
    m = mem_ref[...].astype(jnp.bfloat16)
    k_out[...] = jnp.dot(m, wk_ref[...], preferred_element_type=jnp.float32).astype(k_out.dtype)
    v_out[...] = jnp.dot(m, wv_ref[...], preferred_element_type=jnp.float32).astype(v_out.dtype)


def _mem_kv(mem2d, wk, wv):
    M = mem2d.shape[0]
    bf16 = jnp.bfloat16
    out = jax.ShapeDtypeStruct((M, D_MODEL), bf16)
    return pl.pallas_call(
        _mem_kv_kernel,
        grid=(1,),
        in_specs=[_full((M, D_MODEL)), _full((D_MODEL, D_MODEL)), _full((D_MODEL, D_MODEL))],
        out_specs=[_full((M, D_MODEL))] * 2,
        out_shape=[out, out],
        compiler_params=_params("arbitrary"),
        name="mem_kv",
    )(mem2d, wk.astype(bf16), wv.astype(bf16))


def _xattn_kernel(x_ref, k_ref, v_ref, wq_ref, wo_ref, g_ref, b_ref, o_ref):
    bf16, f32 = jnp.bfloat16, jnp.float32
    cols = [slice(h * X_HEAD_DIM, (h + 1) * X_HEAD_DIM) for h in range(X_HEADS)]
    half = x_ref.shape[0] // 2
    parts = [slice(0, half), slice(half, 2 * half)]
    xs = [x_ref[rows, :] for rows in parts]
    qs = [jnp.dot(x.astype(bf16), wq_ref[...], preferred_element_type=f32).astype(bf16) for x in xs]
    scores = [[lax.dot_general(q[:, sl], k_ref[:, sl], (((1,), (1,)), ((), ())),
                               preferred_element_type=f32) * (X_HEAD_DIM ** -0.5) for sl in cols]
              for q in qs]
    outs = []
    for per_head in scores:
        heads = []
        for s, sl in zip(per_head, cols):
            p = jnp.exp(s - jnp.max(s, axis=-1, keepdims=True))
            p = p / jnp.sum(p, axis=-1, keepdims=True)
            heads.append(jnp.dot(p.astype(bf16), v_ref[:, sl], preferred_element_type=f32))
        o = jnp.concatenate(heads, axis=1).astype(bf16)
        outs.append(jnp.dot(o, wo_ref[...], preferred_element_type=f32))
    for rows, x, xa in zip(parts, xs, outs):
        o_ref[rows, :] = _layer_norm(ALPHA * x + xa, g_ref[...], b_ref[...])


def _xattn(x2d, k_mem, v_mem, wq, wo, ln_g, ln_b, seq_len, mem_len, tm=512):
    M = x2d.shape[0]
    bf16 = jnp.bfloat16
    seq_tiles = seq_len // tm
    rows = pl.BlockSpec((tm, D_MODEL), lambda i: (i, 0))
    mem_spec = pl.BlockSpec((mem_len, D_MODEL), lambda i: (i // seq_tiles, 0))
    return pl.pallas_call(
        _xattn_kernel,
        grid=(M // tm,),
        in_specs=[rows, mem_spec, mem_spec, _full((D_MODEL, D_MODEL)), _full((D_MODEL, D_MODEL)),
                  _full((1, D_MODEL)), _full((1, D_MODEL))],
        out_specs=rows,
        out_shape=jax.ShapeDtypeStruct((M, D_MODEL), jnp.float32),
        compiler_params=_params("parallel"),
        name="xattn",
    )(x2d, k_mem, v_mem, wq.astype(bf16), wo.astype(bf16), ln_g.reshape(1, -1), ln_b.reshape(1, -1))


FFN_CHUNK = 1408


def _ffn_kernel(seq_tiles, x_ref, xp_ref, wup_ref, cw_ref, cb_ref, wdn_ref, g_ref, b_ref, o_ref):
    bf16, f32 = jnp.bfloat16, jnp.float32
    i = pl.program_id(0)
    tm = x_ref.shape[0]
    H = V7X_SUBLANES
    x = x_ref[...]
    xprev = jnp.where(i % seq_tiles == 0, 0.0, xp_ref[...])
    xe = jnp.concatenate([xprev, x], axis=0).astype(bf16)

    def conv(cols):
        h = jnp.dot(xe, wup_ref[:, cols], preferred_element_type=f32)
        w = cw_ref[:, cols]
        return (h[H - 2:H - 2 + tm] * w[0:1] + h[H - 1:H - 1 + tm] * w[1:2] + h[H:] * w[2:3]
                + cb_ref[:, cols])

    acc = jnp.zeros((tm, D_MODEL), f32)
    for c in range(D_FF // FFN_CHUNK):
        gate = conv(slice(c * FFN_CHUNK, (c + 1) * FFN_CHUNK))
        val = conv(slice(D_FF + c * FFN_CHUNK, D_FF + (c + 1) * FFN_CHUNK))
        act = (gate * jax.nn.sigmoid(gate) * val).astype(bf16)
        acc = acc + jnp.dot(act, wdn_ref[c * FFN_CHUNK:(c + 1) * FFN_CHUNK, :], preferred_element_type=f32)
    o_ref[...] = _layer_norm(ALPHA * x + acc, g_ref[...], b_ref[...])


def _ffn(x2d, w_up, conv_w, conv_b, w_down, ln_g, ln_b, seq_len, tm=512):
    M = x2d.shape[0]
    bf16 = jnp.bfloat16
    seq_tiles = seq_len // tm
    blocks_per_tile = tm // V7X_SUBLANES
    rows = pl.BlockSpec((tm, D_MODEL), lambda i: (i, 0))
    once = lambda shape: pl.BlockSpec(shape, lambda i: (0,) * len(shape), pipeline_mode=pl.Buffered(1))
    return pl.pallas_call(
        functools.partial(_ffn_kernel, seq_tiles),
        grid=(M // tm,),
        in_specs=[rows,
                  pl.BlockSpec((V7X_SUBLANES, D_MODEL), lambda i: (jnp.maximum(i * blocks_per_tile - 1, 0), 0)),
                  once((D_MODEL, 2 * D_FF)), _full((3, 2 * D_FF)), _full((1, 2 * D_FF)),
                  once((D_FF, D_MODEL)), _full((1, D_MODEL)), _full((1, D_MODEL))],
        out_specs=rows,
        out_shape=jax.ShapeDtypeStruct((M, D_MODEL), jnp.float32),
        compiler_params=_params("parallel"),
        name="ffn",
    )(x2d, x2d, w_up.astype(bf16), conv_w, conv_b.reshape(1, -1), w_down.astype(bf16),
      ln_g.reshape(1, -1), ln_b.reshape(1, -1))


def kernel(x, mem, w_in, rwkv_mu, rwkv_w0, rwkv_w2, rwkv_a0, rwkv_a2, rwkv_k_k, rwkv_k_a, rwkv_r_k, rwkv_gn_g, rwkv_gn_b, nsa_pe_k, nsa_pe_v, nsa_ck_w1, nsa_ck_w2, nsa_cv_w1, nsa_cv_w2, merge_p_a, merge_p_b, mix_w_o, ln1_g, ln1_b, xa_wq, xa_wk, xa_wv, xa_wo, ln2_g, ln2_b, ffn_w_up, ffn_conv_w, ffn_conv_b, ffn_w_down, ln3_g, ln3_b):
    B, S, _ = x.shape
    mem_len = mem.shape[1]
    bf16 = jnp.bfloat16
    x2d = x.reshape(B * S, D_MODEL)
    for l in range(DEPTH):
        w = w_in[l]
        w_rwkv = w[:, :RWKV_COLS].astype(bf16)
        w_nsa = _nsa_weight(w[:, RWKV_COLS:RWKV_COLS + NSA_COLS])
        w_gate = w[:, RWKV_COLS + NSA_COLS:].astype(bf16)
        y_a = _rwkv_time_mix(x2d, w_rwkv, rwkv_mu[l], rwkv_w0[l], rwkv_w2[l], rwkv_a0[l], rwkv_a2[l],
                             rwkv_k_k[l], rwkv_k_a[l], rwkv_r_k[l], rwkv_gn_g[l], rwkv_gn_b[l], B, S)
        y_b = _nsa_branch(x2d, w_nsa, nsa_pe_k[l], nsa_pe_v[l], nsa_ck_w1[l], nsa_ck_w2[l],
                          nsa_cv_w1[l], nsa_cv_w2[l], B, S)
        x2d = _merge(x2d, y_a, y_b, w_gate, merge_p_a[l], merge_p_b[l], mix_w_o[l], ln1_g[l], ln1_b[l])
        k_mem, v_mem = _mem_kv(mem.reshape(B * mem_len, D_MODEL), xa_wk[l], xa_wv[l])
        x2d = _xattn(x2d, k_mem, v_mem, xa_wq[l], xa_wo[l], ln2_g[l], ln2_b[l], S, mem_len)
        x2d = _ffn(x2d, ffn_w_up[l], ffn_conv_w[l], ffn_conv_b[l], ffn_w_down[l], ln3_g[l], ln3_b[l], S)
    return x2d.reshape(B, S, D_MODEL)
```

```python
import functools
import math

import jax
import jax.numpy as jnp
import numpy as np
from jax import lax
from jax.experimental import pallas as pl
from jax.experimental.pallas import tpu as pltpu

D_MODEL = 1024
HEAD_DIM = 64
RWKV_DIM = 512
RWKV_HEADS = 8
LORA = 64
RWKV_COLS = 3 * RWKV_DIM + 2 * LORA
GN_EPS = 64e-5
NSA_DIM = 512
NSA_Q_HEADS = 8
NSA_GROUPS = 2
NSA_R = NSA_Q_HEADS // NSA_GROUPS
NSA_KV_DIM = NSA_GROUPS * HEAD_DIM
NSA_COLS = NSA_DIM + 6 * NSA_KV_DIM + 3 * NSA_Q_HEADS
CMP_BLOCK = 32
CMP_STRIDE = 16
CMP_HIDDEN = 256
SEL_BLOCK = 64
N_SELECT = 16
WINDOW = 512
Q_BLOCK = 128
ROPE_THETA = 500000.0
ROPE_DIM = 16
X_HEADS = 4
X_HEAD_DIM = 256
D_FF = 2816
LN_EPS = 1e-5
DEPTH = 1
ALPHA = (2 * DEPTH) ** 0.25
NEG = -1e30
BIG = 1e30
LOG2E = math.log2(math.e)

V7X_LANES = 128
V7X_SUBLANES = 8
V7X_VMEM_LIMIT_BYTES = 56 * 1024 * 1024

NSA_COLS_PAD = NSA_DIM + 6 * NSA_KV_DIM + V7X_LANES

RWKV_CHUNK = 64
ROW_TILE = 512
PREP_TILE = 2 * ROW_TILE


def _params(*sem):
    return pltpu.CompilerParams(dimension_semantics=sem, vmem_limit_bytes=V7X_VMEM_LIMIT_BYTES)


def _full(shape):
    n = len(shape)
    return pl.BlockSpec(shape, lambda *_: (0,) * n)


def _head_ones(width):
    r = lax.broadcasted_iota(jnp.int32, (width, width), 0) // HEAD_DIM
    c = lax.broadcasted_iota(jnp.int32, (width, width), 1) // HEAD_DIM
    return (r == c).astype(jnp.float32)


def _layer_norm(y, g, b):
    mu = jnp.mean(y, axis=-1, keepdims=True)
    d = y - mu
    var = jnp.mean(d * d, axis=-1, keepdims=True)
    return d * lax.rsqrt(var + LN_EPS) * g + b


def _rwkv_prep_kernel(seq_tiles, x_ref, xp_ref, w_ref, mu_ref, w0_ref, w2_ref, a0_ref, a2_ref,
                      kk_ref, ka_ref, rk_ref,
                      r_out, k_out, v_out, lw_out, a_out, b_out, bonus_out):
    i = pl.program_id(0)
    tm = x_ref.shape[0]
    C = RWKV_DIM
    w = w_ref[...]
    p = jnp.dot(x_ref[...].astype(jnp.bfloat16), w, preferred_element_type=jnp.float32)
    xprev = xp_ref[...].astype(jnp.bfloat16)
    pprev = jnp.dot(xprev, w, preferred_element_type=jnp.float32)[V7X_SUBLANES - 1:V7X_SUBLANES, :]
    pprev = jnp.where(i % seq_tiles == 0, 0.0, pprev)
    row = lax.broadcasted_iota(jnp.int32, (tm, 1), 0)
    shifted = jnp.where(row == 0, pprev, pltpu.roll(p, 1, 0))
    p = p + (shifted - p) * mu_ref[...]
    r, k, v = p[:, :C], p[:, C:2 * C], p[:, 2 * C:3 * C]
    wl = p[:, 3 * C:3 * C + LORA]
    al = p[:, 3 * C + LORA:]
    z = -(w0_ref[...] + jnp.dot(jnp.tanh(wl).astype(jnp.bfloat16), w2_ref[...],
                                preferred_element_type=jnp.float32))
    softplus = jnp.maximum(z, 0.0) + jnp.log(1.0 + jnp.exp(-jnp.abs(z)))
    w_log = -softplus - 0.5
    lw_out[...] = -jnp.exp(w_log)
    a = jax.nn.sigmoid(a0_ref[...] + jnp.dot(al.astype(jnp.bfloat16), a2_ref[...],
                                             preferred_element_type=jnp.float32))
    ones = _head_ones(C).astype(jnp.bfloat16)
    ones2 = jnp.concatenate([ones, ones], axis=0)

    def head_sum(t):
        return jnp.dot(jnp.concatenate(_split(t), axis=1), ones2, preferred_element_type=jnp.float32)

    kk = k * kk_ref[...]
    kk = kk / jnp.maximum(jnp.sqrt(head_sum(kk * kk)), 1e-12)
    kmod = k * (1.0 + (a - 1.0) * ka_ref[...])
    bonus = head_sum(r * kmod * rk_ref[...]) * v
    r_out[...] = r
    k_out[...] = kmod
    v_out[...] = v
    a_out[...] = -kk
    b_out[...] = kk * a
    bonus_out[...] = bonus


def _rwkv_prep(x2d, w_rwkv, mu, w0, w2, a0, a2, k_k, k_a, r_k, seq_len, tm=PREP_TILE):
    M = x2d.shape[0]
    C = RWKV_DIM
    seq_tiles = seq_len // tm
    row = lambda a: a.reshape(1, -1)
    out = jax.ShapeDtypeStruct((M, C), jnp.float32)
    tile = pl.BlockSpec((tm, C), lambda i: (i, 0))
    blocks_per_tile = tm // V7X_SUBLANES
    return pl.pallas_call(
        functools.partial(_rwkv_prep_kernel, seq_tiles),
        grid=(M // tm,),
        in_specs=[
            pl.BlockSpec((tm, D_MODEL), lambda i: (i, 0)),
            pl.BlockSpec((V7X_SUBLANES, D_MODEL), lambda i: (jnp.maximum(i * blocks_per_tile - 1, 0), 0)),
            _full((D_MODEL, RWKV_COLS)), _full((1, RWKV_COLS)), _full((1, C)), _full((LORA, C)),
            _full((1, C)), _full((LORA, C)), _full((1, C)), _full((1, C)), _full((1, C)),
        ],
        out_specs=[tile] * 7,
        out_shape=[out] * 7,
        compiler_params=_params("parallel"),
        name="rwkv_prep",
    )(x2d, x2d, w_rwkv, row(mu), row(w0), w2.astype(jnp.bfloat16), row(a0), a2.astype(jnp.bfloat16),
      row(k_k), row(k_a), row(r_k))


RWKV_GROUP = 4


def _split(x):
    hi = x.astype(jnp.bfloat16)
    return hi, (x - hi.astype(jnp.float32)).astype(jnp.bfloat16)


def _mm(a, b):
    return jnp.dot(a.astype(jnp.bfloat16), b.astype(jnp.bfloat16), preferred_element_type=jnp.float32)


def _rwkv_mix_kernel(r_ref, k_ref, v_ref, lw_ref, a_ref, b_ref, bonus_ref, gng_ref, gnb_ref,
                     y_out, state):
    T, N, GH = RWKV_CHUNK, HEAD_DIM, RWKV_GROUP
    W = GH * N
    f32 = jnp.float32

    @pl.when(pl.program_id(1) == 0)
    def _():
        state[...] = jnp.zeros_like(state)

    ri = lax.broadcasted_iota(jnp.int32, (W, W), 0)
    ci = lax.broadcasted_iota(jnp.int32, (W, W), 1)
    same_head = ri // N == ci // N
    strict, incl = ri > ci, ri >= ci
    eye = (ri == ci).astype(f32)
    tri = (lax.broadcasted_iota(jnp.int32, (T, T), 0)
           >= lax.broadcasted_iota(jnp.int32, (T, T), 1)).astype(jnp.bfloat16)
    mean_w = (_head_ones(RWKV_DIM) * (1.0 / N)).astype(jnp.bfloat16)

    def expand(x):
        return jnp.where(same_head, jnp.concatenate([x] * GH, axis=0), jnp.zeros((), x.dtype))

    def collapse(x):
        out = x[:T]
        for h in range(1, GH):
            out = out + x[h * T:(h + 1) * T]
        return out

    bf = lambda t: t.astype(jnp.bfloat16)
    n_chunks = r_ref.shape[0] // T
    n_groups = RWKV_HEADS // GH
    chains = [(ch, g) for ch in range(n_chunks) for g in range(n_groups)]
    each = lambda fn, *lists: [fn(*args) for args in zip(*lists)]

    def cumsum(x):
        hi, lo = _split(x)
        lo2 = (x - hi.astype(f32) - lo.astype(f32)).astype(jnp.bfloat16)
        return jnp.dot(jnp.concatenate([tri] * 3, axis=1), jnp.concatenate([hi, lo, lo2], axis=0),
                       preferred_element_type=f32)

    c_all = [cumsum(lw_ref[ch * T:(ch + 1) * T, :]) for ch in range(n_chunks)]

    def operands(ch, g):
        rows, cols = slice(ch * T, (ch + 1) * T), slice(g * W, (g + 1) * W)
        lw, c = lw_ref[rows, cols], c_all[ch][:, cols]
        r, k, v = r_ref[rows, cols], k_ref[rows, cols], v_ref[rows, cols]
        a, b = a_ref[rows, cols], b_ref[rows, cols]
        c_last = c[T - 1:T, :]
        e_neg, e_end = jnp.exp(-c), jnp.exp(c_last - c)
        r_t = r * jnp.exp(c)
        return dict(a_x=expand(bf(a * jnp.exp(c - lw))), r_x=expand(bf(r_t)), b_x=expand(bf(b * e_neg)),
                    k_x=expand(bf(k * e_neg)), v_x=expand(bf(v)), r_t=r_t, b_end=b * e_end, k_end=k * e_end,
                    w_end=jnp.exp(c_last))

    ops = [operands(ch, g) for ch, g in chains]
    P = [lax.dot_general(jnp.concatenate([o["a_x"], o["r_x"]], axis=0),
                         jnp.concatenate([o["b_x"], o["k_x"]], axis=0),
                         (((1,), (1,)), ((), ())), preferred_element_type=f32) for o in ops]
    L_ab = [jnp.where(strict, p[:W, :W], 0.0) for p in P]
    L_ak = [bf(jnp.where(strict, p[:W, W:], 0.0)) for p in P]
    M_rb = [bf(jnp.where(incl, p[W:, :W], 0.0)) for p in P]
    M_rk = [bf(jnp.where(incl, p[W:, W:], 0.0)) for p in P]
    v_x = [o["v_x"] for o in ops]

    base = V7X_SUBLANES
    D = [bf(jnp.where(ri // base == ci // base, l, 0.0)) for l in L_ab]
    D2 = each(_mm, D, D)
    D4 = each(_mm, D2, D2)
    X = each(_mm, each(lambda d, d2: _mm(eye + d.astype(f32), eye + d2), D, D2), [eye + d4 for d4 in D4])
    def second_rows(x, blk):
        return jnp.concatenate([x[s:s + blk] for s in range(blk, W, 2 * blk)], axis=0)

    def merge_rows(x, u, blk):
        pieces = []
        for i, s in enumerate(range(0, W, 2 * blk)):
            pieces += [x[s:s + blk], x[s + blk:s + 2 * blk] + u[i * blk:(i + 1) * blk]]
        return jnp.concatenate(pieces, axis=0)

    blk = base
    while blk < T:
        pair = (ri // (2 * blk) == ci // (2 * blk)) & (ri // blk != ci // blk)
        Xb = [bf(x) for x in X]
        lower = [bf(second_rows(x, blk)) for x in X]
        step = each(_mm, each(_mm, lower, [jnp.where(pair, l, 0.0) for l in L_ab]), Xb)
        X = [merge_rows(x, u, blk) for x, u in zip(X, step)]
        blk *= 2

    lakv = each(_mm, L_ak, v_x)
    AV = [bf(_mm(x, jnp.concatenate([o["a_x"], bf(t)], axis=1))) for x, o, t in zip(X, ops, lakv)]
    ry = each(_mm, M_rb, AV)
    mrkv = each(_mm, M_rk, v_x)
    gh = [_mm(expand(o["b_end"]).T, av) for o, av in zip(ops, AV)]
    kv = [_mm(expand(o["k_end"]).T, vx) for o, vx in zip(ops, v_x)]
    rp = [bf(expand(o["r_t"]) + t[:, :W]) for o, t in zip(ops, ry)]
    yp = [t[:, W:] + u for t, u in zip(ry, mrkv)]
    G = [bf(eye * o["w_end"] + t[:, :W]) for o, t in zip(ops, gh)]
    H = [t[:, W:] + u for t, u in zip(gh, kv)]

    y_rows = []
    for ch in range(n_chunks):
        ys = []
        for g in range(n_groups):
            i = chains.index((ch, g))
            s0 = bf(state[g])
            ys.append(collapse(_mm(rp[i], s0) + yp[i]))
            state[g] = _mm(G[i], s0) + H[i]
        y_rows.append(jnp.concatenate(ys, axis=1))
    y = jnp.concatenate(y_rows, axis=0)
    y_hi, y_lo = _split(y)
    ym = jnp.dot(jnp.concatenate([y_hi, y_lo], axis=1), jnp.concatenate([mean_w, mean_w], axis=0),
                 preferred_element_type=f32)
    d = y - ym
    yv = _mm(d * d, mean_w)
    y_out[...] = d * lax.rsqrt(yv + GN_EPS) * gng_ref[...] + gnb_ref[...] + bonus_ref[...]


def _rwkv_mix(r, k, v, lw, a, b, bonus, gn_g, gn_b, batch, chunks_per_step=4):
    M, C = r.shape
    rows = RWKV_CHUNK * chunks_per_step
    steps = M // rows // batch
    W = RWKV_GROUP * HEAD_DIM
    tile = pl.BlockSpec((rows, C), lambda bi, ci: (bi * steps + ci, 0))
    return pl.pallas_call(
        _rwkv_mix_kernel,
        grid=(batch, steps),
        in_specs=[tile] * 7 + [_full((1, C)), _full((1, C))],
        out_specs=tile,
        out_shape=jax.ShapeDtypeStruct((M, C), jnp.float32),
        scratch_shapes=[pltpu.VMEM((RWKV_HEADS // RWKV_GROUP, W, W), jnp.float32)],
        compiler_params=_params("parallel", "arbitrary"),
        name="rwkv_mix",
    )(r, k, v, lw, a, b, bonus, gn_g.reshape(1, C), gn_b.reshape(1, C))


def _rwkv_time_mix(x2d, w_rwkv, mu, w0, w2, a0, a2, k_k, k_a, r_k, gn_g, gn_b, batch, seq_len):
    r, k, v, lw, a, b, bonus = _rwkv_prep(x2d, w_rwkv, mu, w0, w2, a0, a2, k_k, k_a, r_k, seq_len)
    return _rwkv_mix(r, k, v, lw, a, b, bonus, gn_g, gn_b, batch)


def _rope_tables(seq_len):
    half = ROPE_DIM // 2
    inv = ROPE_THETA ** (-np.arange(half, dtype=np.float64) * 2.0 / ROPE_DIM)
    ang = np.arange(seq_len, dtype=np.float64)[:, None] * inv[None, :]
    cos, sin = np.cos(ang).astype(np.float32), np.sin(ang).astype(np.float32)
    pad = np.zeros((seq_len, HEAD_DIM - ROPE_DIM), np.float32)
    zero = np.zeros_like(sin)
    c = np.concatenate([cos, cos, pad + 1.0], axis=1)
    s_lo = np.concatenate([-sin, zero, pad], axis=1)
    s_hi = np.concatenate([zero, sin, pad], axis=1)
    two = lambda t: jnp.asarray(np.concatenate([t, t], axis=1))
    return two(c), two(s_lo), two(s_hi)


def _rope_pair(x, c, s_lo, s_hi):
    return x * c + pltpu.roll(x, V7X_LANES - ROPE_DIM // 2, 1) * s_lo + pltpu.roll(x, ROPE_DIM // 2, 1) * s_hi


GATE_SLOTS = V7X_SUBLANES
V_ROWS = HEAD_DIM + 2 * V7X_SUBLANES


def _nsa_prep_kernel(x_ref, w_ref, c_ref, slo_ref, shi_ref,
                     qT_out, kc_out, vc_out, ks_out, vsT_out, kw_out, vwT_out, gT_out):
    f32 = jnp.float32
    L, N, G, R, Q = V7X_LANES, HEAD_DIM, NSA_GROUPS, NSA_R, Q_BLOCK
    tm = x_ref.shape[0]
    p = jnp.dot(x_ref[...].astype(jnp.bfloat16), w_ref[...], preferred_element_type=f32)
    c, s_lo, s_hi = c_ref[...], slo_ref[...], shi_ref[...]
    rope = lambda t: _rope_pair(t, c, s_lo, s_hi)
    kv = lambda i: p[:, NSA_DIM + i * L:NSA_DIM + (i + 1) * L]

    qT = [(rope(p[:, j * L:(j + 1) * L]) * (N ** -0.5 * LOG2E)).T for j in range(NSA_DIM // L)]
    gT = jax.nn.sigmoid(kv(6)).T
    for g in range(G):
        for qb in range(tm // Q):
            blk = slice(qb * Q, (qb + 1) * Q)
            heads = [qT[(g * R + r) // 2][((g * R + r) % 2) * N:((g * R + r) % 2 + 1) * N, blk]
                     for r in range(R)]
            qT_out[0, g, qb] = jnp.concatenate(heads, axis=1).astype(qT_out.dtype)
            gates = [gT[(g * R + r) * GATE_SLOTS:(g * R + r + 1) * GATE_SLOTS, blk] for r in range(R)]
            gT_out[0, g, qb] = jnp.concatenate(gates, axis=1)

    k_c, k_s, k_w = rope(kv(0)), rope(kv(2)), rope(kv(4))
    v_c = kv(1)
    v_sT, v_wT = kv(3).T, kv(5).T
    slot = (lax.broadcasted_iota(jnp.int32, (tm, L - N), 0) // SEL_BLOCK) % SEL_SLOTS
    onehot = (slot == lax.broadcasted_iota(jnp.int32, (tm, L - N), 1)).astype(f32)
    for g in range(G):
        cols = slice(g * N, (g + 1) * N)
        kc_out[0, g] = k_c[:, cols]
        vc_out[0, g] = v_c[:, cols]
        ks_out[0, g] = jnp.concatenate([k_s[:, cols], onehot], axis=1).astype(ks_out.dtype)
        kw_out[0, g] = k_w[:, cols].astype(kw_out.dtype)
        extra = (lax.broadcasted_iota(jnp.int32, (V_ROWS - N, tm), 0) == 0).astype(f32)
        vsT_out[0, g] = jnp.concatenate([v_sT[cols, :], extra], axis=0).astype(vsT_out.dtype)
        vwT_out[0, g] = jnp.concatenate([v_wT[cols, :], extra], axis=0).astype(vwT_out.dtype)


def _nsa_weight(w):
    n_qkv = NSA_DIM + 6 * NSA_KV_DIM
    gates = w[:, n_qkv:].reshape(-1, NSA_Q_HEADS, 3)
    gates = jnp.pad(gates, ((0, 0), (0, 0), (0, GATE_SLOTS - 3))).reshape(-1, NSA_Q_HEADS * GATE_SLOTS)
    gates = jnp.pad(gates, ((0, 0), (0, NSA_COLS_PAD - n_qkv - NSA_Q_HEADS * GATE_SLOTS)))
    return jnp.concatenate([w[:, :n_qkv], gates], axis=1).astype(jnp.bfloat16)


def _nsa_prep(x2d, w_nsa, batch, seq_len, tm=PREP_TILE):
    B, S, G, N, L, Q = batch, seq_len, NSA_GROUPS, HEAD_DIM, V7X_LANES, Q_BLOCK
    RQ = NSA_R * Q
    tiles = S // tm
    tabs = _rope_tables(S)
    tab_spec = pl.BlockSpec((tm, L), lambda i: (i % tiles, 0))
    f32, bf16 = jnp.float32, jnp.bfloat16
    per_q = lambda rows: pl.BlockSpec((1, G, tm // Q, rows, RQ), lambda i: (i // tiles, 0, i % tiles, 0, 0))
    by_row = lambda w: pl.BlockSpec((1, G, tm, w), lambda i: (i // tiles, 0, i % tiles, 0))
    by_col = pl.BlockSpec((1, G, V_ROWS, tm), lambda i: (i // tiles, 0, 0, i % tiles))
    sds = jax.ShapeDtypeStruct
    return pl.pallas_call(
        _nsa_prep_kernel,
        grid=(B * tiles,),
        in_specs=[pl.BlockSpec((tm, D_MODEL), lambda i: (i, 0)), _full((D_MODEL, NSA_COLS_PAD)),
                  tab_spec, tab_spec, tab_spec],
        out_specs=[per_q(N), by_row(N), by_row(N), by_row(L), by_col, by_row(N), by_col, per_q(GATE_SLOTS)],
        out_shape=[sds((B, G, S // Q, N, RQ), bf16), sds((B, G, S, N), f32), sds((B, G, S, N), f32),
                   sds((B, G, S, L), bf16), sds((B, G, V_ROWS, S), bf16), sds((B, G, S, N), bf16),
                   sds((B, G, V_ROWS, S), bf16), sds((B, G, S // Q, GATE_SLOTS, RQ), f32)],
        compiler_params=_params("parallel"),
        name="nsa_prep",
    )(x2d, w_nsa, *tabs)


def _gelu_tanh(x):
    return 0.5 * x * (1.0 + jnp.tanh(math.sqrt(2.0 / math.pi) * (x + 0.044715 * x * x * x)))


def _nsa_compress_kernel(xk_ref, xv_ref, pek_ref, pev_ref, kw1_ref, kw2_ref, vw1_ref, vw2_ref,
                         kc_out, vcT_out):
    N, bf16, f32 = HEAD_DIM, jnp.bfloat16, jnp.float32
    n_blocks = kc_out.shape[2]

    def mlp(x_ref, pe_ref, w1_ref, w2_ref):
        lo = jnp.zeros((n_blocks, CMP_HIDDEN), f32)
        hi = jnp.zeros((n_blocks, CMP_HIDDEN), f32)
        for l in range(CMP_STRIDE):
            rows = x_ref[0, 0, pl.ds(l, n_blocks, stride=CMP_STRIDE), :]
            lo = lo + jnp.dot((rows + pe_ref[l:l + 1, :]).astype(bf16), w1_ref[l * N:(l + 1) * N, :],
                              preferred_element_type=f32)
            m = CMP_STRIDE + l
            hi = hi + jnp.dot((rows + pe_ref[m:m + 1, :]).astype(bf16), w1_ref[m * N:(m + 1) * N, :],
                              preferred_element_type=f32)
        pre = lo + pltpu.roll(hi, n_blocks - 1, 0)
        return jnp.dot(_gelu_tanh(pre).astype(bf16), w2_ref[...], preferred_element_type=f32)

    kc_out[0, 0] = mlp(xk_ref, pek_ref, kw1_ref, kw2_ref).astype(kc_out.dtype)
    vc = mlp(xv_ref, pev_ref, vw1_ref, vw2_ref)
    vcT_out[0, 0] = jnp.concatenate([vc, jnp.zeros_like(vc)], axis=1).T[:N].astype(vcT_out.dtype)


def _nsa_compress(xk, xv, pe_k, pe_v, ck_w1, ck_w2, cv_w1, cv_w2):
    B, G, S, N = xk.shape
    NC = S // CMP_STRIDE
    bf16 = jnp.bfloat16
    xin = pl.BlockSpec((1, 1, S, N), lambda b, g: (b, g, 0, 0))
    return pl.pallas_call(
        _nsa_compress_kernel,
        grid=(B, G),
        in_specs=[xin, xin, _full((CMP_BLOCK, N)), _full((CMP_BLOCK, N)), _full((CMP_BLOCK * N, CMP_HIDDEN)),
                  _full((CMP_HIDDEN, N)), _full((CMP_BLOCK * N, CMP_HIDDEN)), _full((CMP_HIDDEN, N))],
        out_specs=[pl.BlockSpec((1, 1, NC, N), lambda b, g: (b, g, 0, 0)),
                   pl.BlockSpec((1, 1, N, NC), lambda b, g: (b, g, 0, 0))],
        out_shape=[jax.ShapeDtypeStruct((B, G, NC, N), bf16), jax.ShapeDtypeStruct((B, G, N, NC), bf16)],
        compiler_params=_params("parallel", "parallel"),
        name="nsa_compress",
    )(xk, xv, pe_k, pe_v, ck_w1.astype(bf16), ck_w2.astype(bf16), cv_w1.astype(bf16), cv_w2.astype(bf16))


SEL_KEYS = 512
SEL_SLOTS = V7X_SUBLANES
CMP_TILE = 128
RANK_ROWS = 32
RANK_UNROLL = 8


def _nsa_attn_kernel(qT_ref, kc_ref, vcT_ref, ovT_ref, ks_ref, vsT_ref, kw_ref, vwT_ref, gT_ref,
                     o_ref, val_ref, cnt_ref, sel_ref, qa_ref, s_ref, cm_ref, m_ref, acc_ref, part_ref,
                     imp_ref):
    qb = pl.program_id(2)
    t0 = qb * Q_BLOCK
    R, Q = NSA_R, Q_BLOCK
    RQ = R * Q
    f32, bf16 = jnp.float32, jnp.bfloat16
    qT = qT_ref[0, 0, 0]
    lane = lax.broadcasted_iota(jnp.int32, (1, RQ), 1)
    tpos = t0 + lane % Q
    mm = lambda a, b: jnp.dot(a, b, preferred_element_type=f32)

    g = gT_ref[0, 0, 0]
    NC = kc_ref.shape[2]
    NSB = ovT_ref.shape[0]

    span = WINDOW + Q
    sub = lax.broadcasted_iota(jnp.int32, (Q, 1), 0)
    qpos = lane % Q

    def local_branches(rows, aligned_window):
        w0 = pl.multiple_of(t0 - WINDOW, Q) if aligned_window else 0
        s_cmp = mm(kc_ref[0, 0, :rows, :], qT)
        s_win = mm(kw_ref[0, 0, pl.ds(w0, span), :], qT)

        cend = lax.broadcasted_iota(jnp.int32, (rows, 1), 0) * CMP_STRIDE + (CMP_BLOCK - 1)
        mask = cend <= tpos
        sm = jnp.where(mask, s_cmp, NEG)
        e = jnp.where(mask, jnp.exp2(sm - jnp.max(sm, axis=0, keepdims=True)), 0.0)
        den = jnp.sum(e, axis=0, keepdims=True)
        p = e / jnp.where(den > 0.0, den, 1.0)
        o_c = mm(vcT_ref[0, 0, :, :rows], p.astype(bf16))
        psum = p[:, :Q]
        for r in range(1, R):
            psum = psum + p[:, r * Q:(r + 1) * Q]
        ps_hi, ps_lo = _split(psum)
        ov = ovT_ref[:, :rows]
        imp_ref[...] = mm(jnp.concatenate([ov, ov], axis=1), jnp.concatenate([ps_hi, ps_lo], axis=0))

        if aligned_window:
            head = jnp.where(sub > qpos, s_win[:Q], NEG)
            tail = jnp.where(sub <= qpos, s_win[WINDOW:], NEG)
            sm = jnp.concatenate([head, s_win[Q:WINDOW], tail], axis=0)
        else:
            diff = tpos - lax.broadcasted_iota(jnp.int32, (span, 1), 0)
            sm = jnp.where((diff >= 0) & (diff < WINDOW), s_win, NEG)
        p = jnp.exp2(sm - jnp.max(sm, axis=0, keepdims=True))
        pv = mm(vwT_ref[0, 0, :, pl.ds(w0, span)], p.astype(bf16))
        o_w = pv[:HEAD_DIM] / pv[HEAD_DIM:HEAD_DIM + 1]
        part_ref[...] = g[0:1, :] * o_c + g[2:3, :] * o_w

    tile = min(CMP_TILE, NC)
    per_tile = tile * CMP_STRIDE // Q
    n_tiles = NC // tile
    first_aligned = WINDOW // Q
    for i in range(n_tiles):
        lo, hi = i * per_tile, (i + 1) * per_tile
        cuts = [lo, hi] if not lo < first_aligned < hi else [lo, first_aligned, hi]
        for a, b in zip(cuts[:-1], cuts[1:]):
            in_range = (qb >= a) if (i == n_tiles - 1 and b == hi) else ((qb >= a) & (qb < b))
            pl.when(in_range)(functools.partial(local_branches, (i + 1) * tile, a >= first_aligned))
    imp = imp_ref[...]

    tq = t0 + lax.broadcasted_iota(jnp.int32, (1, Q), 1)
    jblk = lax.broadcasted_iota(jnp.int32, (NSB, 1), 0)
    cur = tq // SEL_BLOCK
    forced = (jblk == 0) | (jblk == cur) | (jblk == cur - 1)
    valid = jblk * SEL_BLOCK <= tq
    val = jnp.where(valid, jnp.where(forced, BIG, imp), NEG)
    val_ref[...] = val
    n_live = (t0 + Q - 1) // SEL_BLOCK + 1

    chunk = min(RANK_ROWS, NSB)
    n_rounds = (n_live + RANK_UNROLL - 1) // RANK_UNROLL
    for k in range(NSB // chunk):
        rows = slice(k * chunk, (k + 1) * chunk)

        @pl.when(k * chunk < n_live)
        def _():
            target = val_ref[rows, :]

            def strict_body(u, cnt):
                base = u * RANK_UNROLL
                for i in range(RANK_UNROLL):
                    cnt = cnt + jnp.where(val_ref[pl.ds(base + i, 1), :] > target, 1, 0)
                return cnt

            cnt_ref[rows, :] = lax.fori_loop(0, n_rounds, strict_body, jnp.zeros((chunk, Q), jnp.int32))

        @pl.when(k * chunk >= n_live)
        def _():
            cnt_ref[rows, :] = jnp.full((chunk, Q), N_SELECT, jnp.int32)

    taken =jnp.sum(jnp.where(valid & (cnt_ref[...] < N_SELECT), 1, 0), axis=0, keepdims=True)

    @pl.when(jnp.max(taken) > N_SELECT)
    def _():
        def tie_body(i, cnt):
            row = val_ref[pl.ds(i, 1), :]
            ge = jnp.where(row >= val, 1, 0)
            gt = jnp.where(row > val, 1, 0)
            return cnt + jnp.where(jblk > i, ge, gt)

        cnt_ref[...] = lax.fori_loop(0, n_live, tie_body, jnp.zeros((NSB, Q), jnp.int32))

    bias = jnp.where(cnt_ref[...] < N_SELECT, 0.0, NEG)
    sel_ref[...] = jnp.concatenate([bias] * R, axis=1)

    bps = SEL_SLOTS
    for buf in range(qa_ref.shape[0]):
        qa_ref[buf, :HEAD_DIM, :] = qT
        qa_ref[buf, HEAD_DIM:, :] = jnp.zeros((qa_ref.shape[1] - HEAD_DIM, RQ), bf16)
    m_ref[...] = jnp.full(m_ref.shape, NEG, f32)
    acc_ref[...] = jnp.zeros(acc_ref.shape, f32)

    def scores(kc, slot):
        k0 = pl.multiple_of(kc * SEL_KEYS, SEL_KEYS)
        grp = pl.multiple_of((kc * (SEL_KEYS // SEL_BLOCK)) // bps * bps, bps)
        rows = sel_ref[pl.ds(grp, bps), :]
        qa_ref[slot, HEAD_DIM:HEAD_DIM + 2 * bps, :] = (
            jnp.concatenate([rows, jnp.zeros_like(rows)], axis=0).astype(bf16))
        s = mm(ks_ref[0, 0, pl.ds(k0, SEL_KEYS), :], qa_ref[slot])
        s_ref[slot] = s
        cm_ref[slot] = jnp.max(s, axis=0, keepdims=True)

    def accumulate(kc, s, cm):
        k0 = pl.multiple_of(kc * SEL_KEYS, SEL_KEYS)
        m = m_ref[...]
        m_new = jnp.maximum(m, cm)
        m_ref[...] = m_new
        p = jnp.exp2(s - m_new).astype(bf16)
        pv = mm(vsT_ref[0, 0, :, pl.ds(k0, s.shape[0])], p)
        acc_ref[...] = jnp.exp2(m - m_new) * acc_ref[...] + pv

    last = (t0 + Q - 1) // SEL_KEYS
    scores(0, 0)

    def sel_body(j, carry):
        scores(2 * j + 1, 1)
        accumulate(2 * j, s_ref[0], cm_ref[0])
        scores(2 * j + 2, 0)
        accumulate(2 * j + 1, s_ref[1], cm_ref[1])
        return carry

    lax.fori_loop(0, last // 2, sel_body, 0)

    @pl.when(last % 2 == 1)
    def _():
        scores(last, 1)
        accumulate(last - 1, s_ref[0], cm_ref[0])

    def diagonal_step(live):
        rows = live * Q
        s = s_ref[last % 2, :rows, :]
        edge = jnp.where(sub <= qpos, s[rows - Q:], NEG)
        s = edge if live == 1 else jnp.concatenate([s[:rows - Q], edge], axis=0)
        accumulate(last, s, jnp.max(s, axis=0, keepdims=True))

    per_step = SEL_KEYS // Q
    for v in range(per_step):
        pl.when(qb % per_step == v)(functools.partial(diagonal_step, v + 1))

    o_s = acc_ref[:HEAD_DIM, :] / acc_ref[HEAD_DIM:HEAD_DIM + 1, :]
    oT = part_ref[...] + g[1:2, :] * o_s
    pairs = [jnp.concatenate([oT[:, (2 * i) * Q:(2 * i + 1) * Q], oT[:, (2 * i + 1) * Q:(2 * i + 2) * Q]],
                             axis=0).T for i in range(R // 2)]
    o_ref[...] = jnp.concatenate(pairs, axis=1)


def _nsa_attention(qT, kc, vcT, ks, vsT, kw, vwT, gT):
    B, G, NQB, _, RQ = qT.shape
    S = ks.shape[2]
    NC = kc.shape[2]
    NSB = S // SEL_BLOCK
    c = np.arange(NC)[None, :] * CMP_STRIDE
    j = np.arange(NSB)[:, None] * SEL_BLOCK
    ovT = jnp.asarray((c <= j + SEL_BLOCK - 1) & (c + CMP_BLOCK - 1 >= j), jnp.bfloat16)
    per_q = lambda rows: pl.BlockSpec((1, 1, 1, rows, RQ), lambda b, g, q: (b, g, q, 0, 0))
    per_g = lambda d0, d1: pl.BlockSpec((1, 1, d0, d1), lambda b, g, q: (b, g, 0, 0))
    return pl.pallas_call(
        _nsa_attn_kernel,
        grid=(B, G, NQB),
        in_specs=[per_q(HEAD_DIM), per_g(NC, HEAD_DIM), per_g(HEAD_DIM, NC), _full((NSB, NC)),
                  per_g(S, V7X_LANES), per_g(V_ROWS, S), per_g(S, HEAD_DIM), per_g(V_ROWS, S),
                  per_q(V7X_SUBLANES)],
        out_specs=pl.BlockSpec((Q_BLOCK, NSA_R * HEAD_DIM), lambda b, g, q: (b * NQB + q, g)),
        out_shape=jax.ShapeDtypeStruct((B * S, NSA_DIM), jnp.float32),
        scratch_shapes=[pltpu.VMEM((NSB, Q_BLOCK), jnp.float32), pltpu.VMEM((NSB, Q_BLOCK), jnp.int32),
                        pltpu.VMEM((NSB, RQ), jnp.float32),
                        pltpu.VMEM((2, V7X_LANES, RQ), jnp.bfloat16),
                        pltpu.VMEM((2, SEL_KEYS, RQ), jnp.float32), pltpu.VMEM((2, 1, RQ), jnp.float32),
                        pltpu.VMEM((1, RQ), jnp.float32),
                        pltpu.VMEM((V_ROWS, RQ), jnp.float32), pltpu.VMEM((HEAD_DIM, RQ), jnp.float32),
                        pltpu.VMEM((NSB, Q_BLOCK), jnp.float32)],
        compiler_params=_params("parallel", "parallel", "arbitrary"),
        name="nsa_attention",
    )(qT, kc, vcT, ovT, ks, vsT, kw, vwT, gT)


def _nsa_branch(x2d, w_nsa, pe_k, pe_v, ck_w1, ck_w2, cv_w1, cv_w2, batch, seq_len):
    qT, kc_in, vc_in, ks, vsT, kw, vwT, gT = _nsa_prep(x2d, w_nsa, batch, seq_len)
    kc, vcT = _nsa_compress(kc_in, vc_in, pe_k, pe_v, ck_w1, ck_w2, cv_w1, cv_w2)
    return _nsa_attention(qT, kc, vcT, ks, vsT, kw, vwT, gT)


def _merge_kernel(x_ref, ya_ref, yb_ref, wg_ref, pa_ref, pb_ref, wo_ref, g_ref, b_ref, o_ref):
    bf16 = jnp.bfloat16
    mm = lambda a, w: jnp.dot(a.astype(bf16), w, preferred_element_type=jnp.float32)
    half = x_ref.shape[0] // 2
    parts = [slice(0, half), slice(half, 2 * half)]
    xs = [x_ref[rows, :] for rows in parts]
    logits = [mm(x, wg_ref[...]) for x in xs]
    pa = [mm(ya_ref[rows, :], pa_ref[...]) for rows in parts]
    pb = [mm(yb_ref[rows, :], pb_ref[...]) for rows in parts]
    mixed = []
    for lg, a, b in zip(logits, pa, pb):
        gates = jax.nn.sigmoid(lg)
        mixed.append(mm(gates[:, :D_MODEL] * a + gates[:, D_MODEL:] * b, wo_ref[...]))
    for rows, x, mx in zip(parts, xs, mixed):
        o_ref[rows, :] = _layer_norm(ALPHA * x + mx, g_ref[...], b_ref[...])


def _merge(x2d, y_a, y_b, w_gate, p_a, p_b, w_o, ln_g, ln_b, tm=ROW_TILE):
    M = x2d.shape[0]
    bf16 = jnp.bfloat16
    rows = lambda w: pl.BlockSpec((tm, w), lambda i: (i, 0))
    return pl.pallas_call(
        _merge_kernel,
        grid=(M // tm,),
        in_specs=[rows(D_MODEL), rows(RWKV_DIM), rows(NSA_DIM), _full((D_MODEL, 2 * D_MODEL)),
                  _full((RWKV_DIM, D_MODEL)), _full((NSA_DIM, D_MODEL)), _full((D_MODEL, D_MODEL)),
                  _full((1, D_MODEL)), _full((1, D_MODEL))],
        out_specs=rows(D_MODEL),
        out_shape=jax.ShapeDtypeStruct((M, D_MODEL), jnp.float32),
        compiler_params=_params("parallel"),
        name="merge",
    )(x2d, y_a, y_b, w_gate, p_a.astype(bf16), p_b.astype(bf16), w_o.astype(bf16),
      ln_g.reshape(1, -1), ln_b.reshape(1, -1))


def _mem_kv_kernel(mem_ref, wk_ref, wv_ref, k_out, v_out):
    m = mem_ref[...].astype(jnp.bfloat16)
    k_out[...] = jnp.dot(m, wk_ref[...], preferred_element_type=jnp.float32).astype(k_out.dtype)
    v_out[...] = jnp.dot(m, wv_ref[...], preferred_element_type=jnp.float32).astype(v_out.dtype)


def _mem_kv(mem2d, wk, wv):
    M = mem2d.shape[0]
    bf16 = jnp.bfloat16
    out = jax.ShapeDtypeStruct((M, D_MODEL), bf16)
    return pl.pallas_call(
        _mem_kv_kernel,
        grid=(1,),
        in_specs=[_full((M, D_MODEL)), _full((D_MODEL, D_MODEL)), _full((D_MODEL, D_MODEL))],
        out_specs=[_full((M, D_MODEL))] * 2,
        out_shape=[out, out],
        compiler_params=_params("arbitrary"),
        name="mem_kv",
    )(mem2d, wk.astype(bf16), wv.astype(bf16))


def _xattn_kernel(x_ref, k_ref, v_ref, wq_ref, wo_ref, g_ref, b_ref, o_ref):
    bf16, f32 = jnp.bfloat16, jnp.float32
    cols = [slice(h * X_HEAD_DIM, (h + 1) * X_HEAD_DIM) for h in range(X_HEADS)]
    half = x_ref.shape[0] // 2
    parts = [slice(0, half), slice(half, 2 * half)]
    xs = [x_ref[rows, :] for rows in parts]
    qs = [jnp.dot(x.astype(bf16), wq_ref[...], preferred_element_type=f32).astype(bf16) for x in xs]
    scores = [[lax.dot_general(q[:, sl], k_ref[:, sl], (((1,), (1,)), ((), ())),
                               preferred_element_type=f32) * (X_HEAD_DIM ** -0.5) for sl in cols]
              for q in qs]
    outs = []
    for per_head in scores:
        heads = []
        for s, sl in zip(per_head, cols):
            p = jnp.exp(s - jnp.max(s, axis=-1, keepdims=True))
            p = p / jnp.sum(p, axis=-1, keepdims=True)
            heads.append(jnp.dot(p.astype(bf16), v_ref[:, sl], preferred_element_type=f32))
        o = jnp.concatenate(heads, axis=1).astype(bf16)
        outs.append(jnp.dot(o, wo_ref[...], preferred_element_type=f32))
    for rows, x, xa in zip(parts, xs, outs):
        o_ref[rows, :] = _layer_norm(ALPHA * x + xa, g_ref[...], b_ref[...])


def _xattn(x2d, k_mem, v_mem, wq, wo, ln_g, ln_b, seq_len, mem_len, tm=ROW_TILE):
    M = x2d.shape[0]
    bf16 = jnp.bfloat16
    seq_tiles = seq_len // tm
    rows = pl.BlockSpec((tm, D_MODEL), lambda i: (i, 0))
    mem_spec = pl.BlockSpec((mem_len, D_MODEL), lambda i: (i // seq_tiles, 0))
    return pl.pallas_call(
        _xattn_kernel,
        grid=(M // tm,),
        in_specs=[rows, mem_spec, mem_spec, _full((D_MODEL, D_MODEL)), _full((D_MODEL, D_MODEL)),
                  _full((1, D_MODEL)), _full((1, D_MODEL))],
        out_specs=rows,
        out_shape=jax.ShapeDtypeStruct((M, D_MODEL), jnp.float32),
        compiler_params=_params("parallel"),
        name="xattn",
    )(x2d, k_mem, v_mem, wq.astype(bf16), wo.astype(bf16), ln_g.reshape(1, -1), ln_b.reshape(1, -1))


FFN_CHUNK = 1408


def _ffn_kernel(seq_tiles, x_ref, xp_ref, wup_ref, cw_ref, cb_ref, wdn_ref, g_ref, b_ref, o_ref):
    bf16, f32 = jnp.bfloat16, jnp.float32
    i = pl.program_id(0)
    tm = x_ref.shape[0]
    H = V7X_SUBLANES
    x = x_ref[...]
    xprev = jnp.where(i % seq_tiles == 0, 0.0, xp_ref[...])
    xe = jnp.concatenate([xprev, x], axis=0).astype(bf16)

    def conv(cols):
        h = jnp.dot(xe, wup_ref[:, cols], preferred_element_type=f32)
        w = cw_ref[:, cols]
        return (h[H - 2:H - 2 + tm] * w[0:1] + h[H - 1:H - 1 + tm] * w[1:2] + h[H:] * w[2:3]
                + cb_ref[:, cols])

    acc = jnp.zeros((tm, D_MODEL), f32)
    for c in range(D_FF // FFN_CHUNK):
        gate = conv(slice(c * FFN_CHUNK, (c + 1) * FFN_CHUNK))
        val = conv(slice(D_FF + c * FFN_CHUNK, D_FF + (c + 1) * FFN_CHUNK))
        act = (gate * jax.nn.sigmoid(gate) * val).astype(bf16)
        acc = acc + jnp.dot(act, wdn_ref[c * FFN_CHUNK:(c + 1) * FFN_CHUNK, :], preferred_element_type=f32)
    o_ref[...] = _layer_norm(ALPHA * x + acc, g_ref[...], b_ref[...])


def _ffn(x2d, w_up, conv_w, conv_b, w_down, ln_g, ln_b, seq_len, tm=ROW_TILE):
    M = x2d.shape[0]
    bf16 = jnp.bfloat16
    seq_tiles = seq_len // tm
    blocks_per_tile = tm // V7X_SUBLANES
    rows = pl.BlockSpec((tm, D_MODEL), lambda i: (i, 0))
    once = lambda shape: pl.BlockSpec(shape, lambda i: (0,) * len(shape), pipeline_mode=pl.Buffered(1))
    return pl.pallas_call(
        functools.partial(_ffn_kernel, seq_tiles),
        grid=(M // tm,),
        in_specs=[rows,
                  pl.BlockSpec((V7X_SUBLANES, D_MODEL), lambda i: (jnp.maximum(i * blocks_per_tile - 1, 0), 0)),
                  once((D_MODEL, 2 * D_FF)), _full((3, 2 * D_FF)), _full((1, 2 * D_FF)),
                  once((D_FF, D_MODEL)), _full((1, D_MODEL)), _full((1, D_MODEL))],
        out_specs=rows,
        out_shape=jax.ShapeDtypeStruct((M, D_MODEL), jnp.float32),
        compiler_params=_params("parallel"),
        name="ffn",
    )(x2d, x2d, w_up.astype(bf16), conv_w, conv_b.reshape(1, -1), w_down.astype(bf16),
      ln_g.reshape(1, -1), ln_b.reshape(1, -1))


def kernel(x, mem, w_in, rwkv_mu, rwkv_w0, rwkv_w2, rwkv_a0, rwkv_a2, rwkv_k_k, rwkv_k_a, rwkv_r_k, rwkv_gn_g, rwkv_gn_b, nsa_pe_k, nsa_pe_v, nsa_ck_w1, nsa_ck_w2, nsa_cv_w1, nsa_cv_w2, merge_p_a, merge_p_b, mix_w_o, ln1_g, ln1_b, xa_wq, xa_wk, xa_wv, xa_wo, ln2_g, ln2_b, ffn_w_up, ffn_conv_w, ffn_conv_b, ffn_w_down, ln3_g, ln3_b):
    B, S, _ = x.shape
    mem_len = mem.shape[1]
    if x.shape[2] != D_MODEL or S % PREP_TILE or S % SEL_KEYS or S < WINDOW + Q_BLOCK:
        raise ValueError(f"unsupported input shape {x.shape}")
    bf16 = jnp.bfloat16
    x2d = x.reshape(B * S, D_MODEL)
    for l in range(DEPTH):
        w = w_in[l]
        w_rwkv = w[:, :RWKV_COLS].astype(bf16)
        w_nsa = _nsa_weight(w[:, RWKV_COLS:RWKV_COLS + NSA_COLS])
        w_gate = w[:, RWKV_COLS + NSA_COLS:].astype(bf16)
        y_a = _rwkv_time_mix(x2d, w_rwkv, rwkv_mu[l], rwkv_w0[l], rwkv_w2[l], rwkv_a0[l], rwkv_a2[l],
                             rwkv_k_k[l], rwkv_k_a[l], rwkv_r_k[l], rwkv_gn_g[l], rwkv_gn_b[l], B, S)
        y_b = _nsa_branch(x2d, w_nsa, nsa_pe_k[l], nsa_pe_v[l], nsa_ck_w1[l], nsa_ck_w2[l],
                          nsa_cv_w1[l], nsa_cv_w2[l], B, S)
        x2d = _merge(x2d, y_a, y_b, w_gate, merge_p_a[l], merge_p_b[l], mix_w_o[l], ln1_g[l], ln1_b[l])
        k_mem, v_mem = _mem_kv(mem.reshape(B * mem_len, D_MODEL), xa_wk[l], xa_wv[l])
        x2d = _xattn(x2d, k_mem, v_mem, xa_wq[l], xa_wo[l], ln2_g[l], ln2_b[l], S, mem_len)
        x2d = _ffn(x2d, ffn_w_up[l], ffn_conv_w[l], ffn_conv_b[l], ffn_w_down[l], ln3_g[l], ln3_b[l], S)
    return x2d.reshape(B, S, D_MODEL)
```

```python
import functools
import math

import jax
import jax.numpy as jnp
import numpy as np
from jax import lax
from jax.experimental import pallas as pl
from jax.experimental.pallas import tpu as pltpu

D_MODEL = 1024
HEAD_DIM = 64
RWKV_DIM = 512
RWKV_HEADS = 8
LORA = 64
RWKV_COLS = 3 * RWKV_DIM + 2 * LORA
GN_EPS = 64e-5
NSA_DIM = 512
NSA_Q_HEADS = 8
NSA_GROUPS = 2
NSA_R = NSA_Q_HEADS // NSA_GROUPS
NSA_KV_DIM = NSA_GROUPS * HEAD_DIM
NSA_COLS = NSA_DIM + 6 * NSA_KV_DIM + 3 * NSA_Q_HEADS
CMP_BLOCK = 32
CMP_STRIDE = 16
CMP_HIDDEN = 256
SEL_BLOCK = 64
N_SELECT = 16
WINDOW = 512
Q_BLOCK = 128
ROPE_THETA = 500000.0
ROPE_DIM = 16
X_HEADS = 4
X_HEAD_DIM = 256
D_FF = 2816
LN_EPS = 1e-5
DEPTH = 1
ALPHA = (2 * DEPTH) ** 0.25
NEG = -1e30
BIG = 1e30
LOG2E = math.log2(math.e)

V7X_LANES = 128
V7X_SUBLANES = 8
V7X_VMEM_LIMIT_BYTES = 56 * 1024 * 1024

NSA_COLS_PAD = NSA_DIM + 6 * NSA_KV_DIM + V7X_LANES

RWKV_CHUNK = 64
ROW_TILE = 512
PREP_TILE = 2 * ROW_TILE


def _params(*sem):
    return pltpu.CompilerParams(dimension_semantics=sem, vmem_limit_bytes=V7X_VMEM_LIMIT_BYTES)


def _full(shape):
    n = len(shape)
    return pl.BlockSpec(shape, lambda *_: (0,) * n)


def _head_ones(width):
    r = lax.broadcasted_iota(jnp.int32, (width, width), 0) // HEAD_DIM
    c = lax.broadcasted_iota(jnp.int32, (width, width), 1) // HEAD_DIM
    return (r == c).astype(jnp.float32)


def _layer_norm(y, g, b):
    mu = jnp.mean(y, axis=-1, keepdims=True)
    d = y - mu
    var = jnp.mean(d * d, axis=-1, keepdims=True)
    return d * lax.rsqrt(var + LN_EPS) * g + b


def _rwkv_prep_kernel(seq_tiles, x_ref, xp_ref, w_ref, mu_ref, w0_ref, w2_ref, a0_ref, a2_ref,
                      kk_ref, ka_ref, rk_ref,
                      r_out, k_out, v_out, lw_out, a_out, b_out, bonus_out):
    i = pl.program_id(0)
    tm = x_ref.shape[0]
    C = RWKV_DIM
    w = w_ref[...]
    p = jnp.dot(x_ref[...].astype(jnp.bfloat16), w, preferred_element_type=jnp.float32)
    xprev = xp_ref[...].astype(jnp.bfloat16)
    pprev = jnp.dot(xprev, w, preferred_element_type=jnp.float32)[V7X_SUBLANES - 1:V7X_SUBLANES, :]
    pprev = jnp.where(i % seq_tiles == 0, 0.0, pprev)
    row = lax.broadcasted_iota(jnp.int32, (tm, 1), 0)
    shifted = jnp.where(row == 0, pprev, pltpu.roll(p, 1, 0))
    p = p + (shifted - p) * mu_ref[...]
    r, k, v = p[:, :C], p[:, C:2 * C], p[:, 2 * C:3 * C]
    wl = p[:, 3 * C:3 * C + LORA]
    al = p[:, 3 * C + LORA:]
    z = -(w0_ref[...] + jnp.dot(jnp.tanh(wl).astype(jnp.bfloat16), w2_ref[...],
                                preferred_element_type=jnp.float32))
    softplus = jnp.maximum(z, 0.0) + jnp.log(1.0 + jnp.exp(-jnp.abs(z)))
    w_log = -softplus - 0.5
    lw_out[...] = -jnp.exp(w_log)
    a = jax.nn.sigmoid(a0_ref[...] + jnp.dot(al.astype(jnp.bfloat16), a2_ref[...],
                                             preferred_element_type=jnp.float32))
    ones = _head_ones(C).astype(jnp.bfloat16)
    ones2 = jnp.concatenate([ones, ones], axis=0)

    def head_sum(t):
        return jnp.dot(jnp.concatenate(_split(t), axis=1), ones2, preferred_element_type=jnp.float32)

    kk = k * kk_ref[...]
    kk = kk / jnp.maximum(jnp.sqrt(head_sum(kk * kk)), 1e-12)
    kmod = k * (1.0 + (a - 1.0) * ka_ref[...])
    bonus = head_sum(r * kmod * rk_ref[...]) * v
    r_out[...] = r
    k_out[...] = kmod
    v_out[...] = v
    a_out[...] = -kk
    b_out[...] = kk * a
    bonus_out[...] = bonus


def _rwkv_prep(x2d, w_rwkv, mu, w0, w2, a0, a2, k_k, k_a, r_k, seq_len, tm=PREP_TILE):
    M = x2d.shape[0]
    C = RWKV_DIM
    seq_tiles = seq_len // tm
    row = lambda a: a.reshape(1, -1)
    out = jax.ShapeDtypeStruct((M, C), jnp.float32)
    tile = pl.BlockSpec((tm, C), lambda i: (i, 0))
    blocks_per_tile = tm // V7X_SUBLANES
    return pl.pallas_call(
        functools.partial(_rwkv_prep_kernel, seq_tiles),
        grid=(M // tm,),
        in_specs=[
            pl.BlockSpec((tm, D_MODEL), lambda i: (i, 0)),
            pl.BlockSpec((V7X_SUBLANES, D_MODEL), lambda i: (jnp.maximum(i * blocks_per_tile - 1, 0), 0)),
            _full((D_MODEL, RWKV_COLS)), _full((1, RWKV_COLS)), _full((1, C)), _full((LORA, C)),
            _full((1, C)), _full((LORA, C)), _full((1, C)), _full((1, C)), _full((1, C)),
        ],
        out_specs=[tile] * 7,
        out_shape=[out] * 7,
        compiler_params=_params("parallel"),
        name="rwkv_prep",
    )(x2d, x2d, w_rwkv, row(mu), row(w0), w2.astype(jnp.bfloat16), row(a0), a2.astype(jnp.bfloat16),
      row(k_k), row(k_a), row(r_k))


RWKV_GROUP = 4


def _split(x):
    hi = x.astype(jnp.bfloat16)
    return hi, (x - hi.astype(jnp.float32)).astype(jnp.bfloat16)


def _mm(a, b):
    return jnp.dot(a.astype(jnp.bfloat16), b.astype(jnp.bfloat16), preferred_element_type=jnp.float32)


def _rwkv_mix_kernel(r_ref, k_ref, v_ref, lw_ref, a_ref, b_ref, bonus_ref, gng_ref, gnb_ref,
                     y_out, state):
    T, N, GH = RWKV_CHUNK, HEAD_DIM, RWKV_GROUP
    W = GH * N
    f32 = jnp.float32

    @pl.when(pl.program_id(1) == 0)
    def _():
        state[...] = jnp.zeros_like(state)

    ri = lax.broadcasted_iota(jnp.int32, (W, W), 0)
    ci = lax.broadcasted_iota(jnp.int32, (W, W), 1)
    same_head = ri // N == ci // N
    strict, incl = ri > ci, ri >= ci
    eye = (ri == ci).astype(f32)
    tri = (lax.broadcasted_iota(jnp.int32, (T, T), 0)
           >= lax.broadcasted_iota(jnp.int32, (T, T), 1)).astype(jnp.bfloat16)
    mean_w = (_head_ones(RWKV_DIM) * (1.0 / N)).astype(jnp.bfloat16)

    def expand(x):
        return jnp.where(same_head, jnp.concatenate([x] * GH, axis=0), jnp.zeros((), x.dtype))

    def collapse(x):
        out = x[:T]
        for h in range(1, GH):
            out = out + x[h * T:(h + 1) * T]
        return out

    bf = lambda t: t.astype(jnp.bfloat16)
    n_chunks = r_ref.shape[0] // T
    n_groups = RWKV_HEADS // GH
    chains = [(ch, g) for ch in range(n_chunks) for g in range(n_groups)]
    each = lambda fn, *lists: [fn(*args) for args in zip(*lists)]

    def cumsum(x):
        hi, lo = _split(x)
        lo2 = (x - hi.astype(f32) - lo.astype(f32)).astype(jnp.bfloat16)
        return jnp.dot(jnp.concatenate([tri] * 3, axis=1), jnp.concatenate([hi, lo, lo2], axis=0),
                       preferred_element_type=f32)

    c_all = [cumsum(lw_ref[ch * T:(ch + 1) * T, :]) for ch in range(n_chunks)]

    def operands(ch, g):
        rows, cols = slice(ch * T, (ch + 1) * T), slice(g * W, (g + 1) * W)
        lw, c = lw_ref[rows, cols], c_all[ch][:, cols]
        r, k, v = r_ref[rows, cols], k_ref[rows, cols], v_ref[rows, cols]
        a, b = a_ref[rows, cols], b_ref[rows, cols]
        c_last = c[T - 1:T, :]
        e_neg, e_end = jnp.exp(-c), jnp.exp(c_last - c)
        r_t = r * jnp.exp(c)
        return dict(a_x=expand(bf(a * jnp.exp(c - lw))), r_x=expand(bf(r_t)), b_x=expand(bf(b * e_neg)),
                    k_x=expand(bf(k * e_neg)), v_x=expand(bf(v)), r_t=r_t, b_end=b * e_end, k_end=k * e_end,
                    w_end=jnp.exp(c_last))

    ops = [operands(ch, g) for ch, g in chains]
    P = [lax.dot_general(jnp.concatenate([o["a_x"], o["r_x"]], axis=0),
                         jnp.concatenate([o["b_x"], o["k_x"]], axis=0),
                         (((1,), (1,)), ((), ())), preferred_element_type=f32) for o in ops]
    L_ab = [jnp.where(strict, p[:W, :W], 0.0) for p in P]
    L_ak = [bf(jnp.where(strict, p[:W, W:], 0.0)) for p in P]
    M_rb = [bf(jnp.where(incl, p[W:, :W], 0.0)) for p in P]
    M_rk = [bf(jnp.where(incl, p[W:, W:], 0.0)) for p in P]
    v_x = [o["v_x"] for o in ops]

    base = V7X_SUBLANES
    D = [bf(jnp.where(ri // base == ci // base, l, 0.0)) for l in L_ab]
    D2 = each(_mm, D, D)
    D4 = each(_mm, D2, D2)
    X = each(_mm, each(lambda d, d2: _mm(eye + d.astype(f32), eye + d2), D, D2), [eye + d4 for d4 in D4])
    def second_rows(x, blk):
        return jnp.concatenate([x[s:s + blk] for s in range(blk, W, 2 * blk)], axis=0)

    def merge_rows(x, u, blk):
        pieces = []
        for i, s in enumerate(range(0, W, 2 * blk)):
            pieces += [x[s:s + blk], x[s + blk:s + 2 * blk] + u[i * blk:(i + 1) * blk]]
        return jnp.concatenate(pieces, axis=0)

    blk = base
    while blk < T:
        pair = (ri // (2 * blk) == ci // (2 * blk)) & (ri // blk != ci // blk)
        Xb = [bf(x) for x in X]
        lower = [bf(second_rows(x, blk)) for x in X]
        step = each(_mm, each(_mm, lower, [jnp.where(pair, l, 0.0) for l in L_ab]), Xb)
        X = [merge_rows(x, u, blk) for x, u in zip(X, step)]
        blk *= 2

    lakv = each(_mm, L_ak, v_x)
    AV = [bf(_mm(x, jnp.concatenate([o["a_x"], bf(t)], axis=1))) for x, o, t in zip(X, ops, lakv)]
    ry = each(_mm, M_rb, AV)
    mrkv = each(_mm, M_rk, v_x)
    gh = [_mm(expand(o["b_end"]).T, av) for o, av in zip(ops, AV)]
    kv = [_mm(expand(o["k_end"]).T, vx) for o, vx in zip(ops, v_x)]
    rp = [bf(expand(o["r_t"]) + t[:, :W]) for o, t in zip(ops, ry)]
    yp = [t[:, W:] + u for t, u in zip(ry, mrkv)]
    G = [bf(eye * o["w_end"] + t[:, :W]) for o, t in zip(ops, gh)]
    H = [t[:, W:] + u for t, u in zip(gh, kv)]

    y_rows = []
    for ch in range(n_chunks):
        ys = []
        for g in range(n_groups):
            i = chains.index((ch, g))
            s0 = bf(state[g])
            ys.append(collapse(_mm(rp[i], s0) + yp[i]))
            state[g] = _mm(G[i], s0) + H[i]
        y_rows.append(jnp.concatenate(ys, axis=1))
    y = jnp.concatenate(y_rows, axis=0)
    y_hi, y_lo = _split(y)
    ym = jnp.dot(jnp.concatenate([y_hi, y_lo], axis=1), jnp.concatenate([mean_w, mean_w], axis=0),
                 preferred_element_type=f32)
    d = y - ym
    yv = _mm(d * d, mean_w)
    y_out[...] = d * lax.rsqrt(yv + GN_EPS) * gng_ref[...] + gnb_ref[...] + bonus_ref[...]


def _rwkv_mix(r, k, v, lw, a, b, bonus, gn_g, gn_b, batch, chunks_per_step=4):
    M, C = r.shape
    rows = RWKV_CHUNK * chunks_per_step
    steps = M // rows // batch
    W = RWKV_GROUP * HEAD_DIM
    tile = pl.BlockSpec((rows, C), lambda bi, ci: (bi * steps + ci, 0))
    return pl.pallas_call(
        _rwkv_mix_kernel,
        grid=(batch, steps),
        in_specs=[tile] * 7 + [_full((1, C)), _full((1, C))],
        out_specs=tile,
        out_shape=jax.ShapeDtypeStruct((M, C), jnp.float32),
        scratch_shapes=[pltpu.VMEM((RWKV_HEADS // RWKV_GROUP, W, W), jnp.float32)],
        compiler_params=_params("parallel", "arbitrary"),
        name="rwkv_mix",
    )(r, k, v, lw, a, b, bonus, gn_g.reshape(1, C), gn_b.reshape(1, C))


def _rwkv_time_mix(x2d, w_rwkv, mu, w0, w2, a0, a2, k_k, k_a, r_k, gn_g, gn_b, batch, seq_len):
    r, k, v, lw, a, b, bonus = _rwkv_prep(x2d, w_rwkv, mu, w0, w2, a0, a2, k_k, k_a, r_k, seq_len)
    return _rwkv_mix(r, k, v, lw, a, b, bonus, gn_g, gn_b, batch)


def _rope_tables(seq_len):
    half = ROPE_DIM // 2
    inv = ROPE_THETA ** (-np.arange(half, dtype=np.float64) * 2.0 / ROPE_DIM)
    ang = np.arange(seq_len, dtype=np.float64)[:, None] * inv[None, :]
    cos, sin = np.cos(ang).astype(np.float32), np.sin(ang).astype(np.float32)
    pad = np.zeros((seq_len, HEAD_DIM - ROPE_DIM), np.float32)
    zero = np.zeros_like(sin)
    c = np.concatenate([cos, cos, pad + 1.0], axis=1)
    s_lo = np.concatenate([-sin, zero, pad], axis=1)
    s_hi = np.concatenate([zero, sin, pad], axis=1)
    two = lambda t: jnp.asarray(np.concatenate([t, t], axis=1))
    return two(c), two(s_lo), two(s_hi)


def _rope_pair(x, c, s_lo, s_hi):
    return x * c + pltpu.roll(x, V7X_LANES - ROPE_DIM // 2, 1) * s_lo + pltpu.roll(x, ROPE_DIM // 2, 1) * s_hi


GATE_SLOTS = V7X_SUBLANES
V_ROWS = HEAD_DIM + 2 * V7X_SUBLANES


def _nsa_prep_kernel(x_ref, w_ref, c_ref, slo_ref, shi_ref,
                     qT_out, kc_out, vc_out, ks_out, vsT_out, kw_out, vwT_out, gT_out):
    f32 = jnp.float32
    L, N, G, R, Q = V7X_LANES, HEAD_DIM, NSA_GROUPS, NSA_R, Q_BLOCK
    tm = x_ref.shape[0]
    p = jnp.dot(x_ref[...].astype(jnp.bfloat16), w_ref[...], preferred_element_type=f32)
    c, s_lo, s_hi = c_ref[...], slo_ref[...], shi_ref[...]
    rope = lambda t: _rope_pair(t, c, s_lo, s_hi)
    kv = lambda i: p[:, NSA_DIM + i * L:NSA_DIM + (i + 1) * L]

    qT = [(rope(p[:, j * L:(j + 1) * L]) * (N ** -0.5 * LOG2E)).T for j in range(NSA_DIM // L)]
    gT = jax.nn.sigmoid(kv(6)).T
    for g in range(G):
        for qb in range(tm // Q):
            blk = slice(qb * Q, (qb + 1) * Q)
            heads = [qT[(g * R + r) // 2][((g * R + r) % 2) * N:((g * R + r) % 2 + 1) * N, blk]
                     for r in range(R)]
            qT_out[0, g, qb] = jnp.concatenate(heads, axis=1).astype(qT_out.dtype)
            gates = [gT[(g * R + r) * GATE_SLOTS:(g * R + r + 1) * GATE_SLOTS, blk] for r in range(R)]
            gT_out[0, g, qb] = jnp.concatenate(gates, axis=1)

    k_c, k_s, k_w = rope(kv(0)), rope(kv(2)), rope(kv(4))
    v_c = kv(1)
    v_sT, v_wT = kv(3).T, kv(5).T
    slot = (lax.broadcasted_iota(jnp.int32, (tm, L - N), 0) // SEL_BLOCK) % SEL_SLOTS
    onehot = (slot == lax.broadcasted_iota(jnp.int32, (tm, L - N), 1)).astype(f32)
    for g in range(G):
        cols = slice(g * N, (g + 1) * N)
        kc_out[0, g] = k_c[:, cols]
        vc_out[0, g] = v_c[:, cols]
        ks_out[0, g] = jnp.concatenate([k_s[:, cols], onehot], axis=1).astype(ks_out.dtype)
        kw_out[0, g] = k_w[:, cols].astype(kw_out.dtype)
        extra = (lax.broadcasted_iota(jnp.int32, (V_ROWS - N, tm), 0) == 0).astype(f32)
        vsT_out[0, g] = jnp.concatenate([v_sT[cols, :], extra], axis=0).astype(vsT_out.dtype)
        vwT_out[0, g] = jnp.concatenate([v_wT[cols, :], extra], axis=0).astype(vwT_out.dtype)


def _nsa_weight(w):
    n_qkv = NSA_DIM + 6 * NSA_KV_DIM
    gates = w[:, n_qkv:].reshape(-1, NSA_Q_HEADS, 3)
    gates = jnp.pad(gates, ((0, 0), (0, 0), (0, GATE_SLOTS - 3))).reshape(-1, NSA_Q_HEADS * GATE_SLOTS)
    gates = jnp.pad(gates, ((0, 0), (0, NSA_COLS_PAD - n_qkv - NSA_Q_HEADS * GATE_SLOTS)))
    return jnp.concatenate([w[:, :n_qkv], gates], axis=1).astype(jnp.bfloat16)


def _nsa_prep(x2d, w_nsa, batch, seq_len, tm=PREP_TILE):
    B, S, G, N, L, Q = batch, seq_len, NSA_GROUPS, HEAD_DIM, V7X_LANES, Q_BLOCK
    RQ = NSA_R * Q
    tiles = S // tm
    tabs = _rope_tables(S)
    tab_spec = pl.BlockSpec((tm, L), lambda i: (i % tiles, 0))
    f32, bf16 = jnp.float32, jnp.bfloat16
    per_q = lambda rows: pl.BlockSpec((1, G, tm // Q, rows, RQ), lambda i: (i // tiles, 0, i % tiles, 0, 0))
    by_row = lambda w: pl.BlockSpec((1, G, tm, w), lambda i: (i // tiles, 0, i % tiles, 0))
    by_col = pl.BlockSpec((1, G, V_ROWS, tm), lambda i: (i // tiles, 0, 0, i % tiles))
    sds = jax.ShapeDtypeStruct
    return pl.pallas_call(
        _nsa_prep_kernel,
        grid=(B * tiles,),
        in_specs=[pl.BlockSpec((tm, D_MODEL), lambda i: (i, 0)), _full((D_MODEL, NSA_COLS_PAD)),
                  tab_spec, tab_spec, tab_spec],
        out_specs=[per_q(N), by_row(N), by_row(N), by_row(L), by_col, by_row(N), by_col, per_q(GATE_SLOTS)],
        out_shape=[sds((B, G, S // Q, N, RQ), bf16), sds((B, G, S, N), f32), sds((B, G, S, N), f32),
                   sds((B, G, S, L), bf16), sds((B, G, V_ROWS, S), bf16), sds((B, G, S, N), bf16),
                   sds((B, G, V_ROWS, S), bf16), sds((B, G, S // Q, GATE_SLOTS, RQ), f32)],
        compiler_params=_params("parallel"),
        name="nsa_prep",
    )(x2d, w_nsa, *tabs)


def _gelu_tanh(x):
    return 0.5 * x * (1.0 + jnp.tanh(math.sqrt(2.0 / math.pi) * (x + 0.044715 * x * x * x)))


def _nsa_compress_kernel(xk_ref, xv_ref, pek_ref, pev_ref, kw1_ref, kw2_ref, vw1_ref, vw2_ref,
                         kc_out, vcT_out):
    N, bf16, f32 = HEAD_DIM, jnp.bfloat16, jnp.float32
    n_blocks = kc_out.shape[2]

    def mlp(x_ref, pe_ref, w1_ref, w2_ref):
        lo = jnp.zeros((n_blocks, CMP_HIDDEN), f32)
        hi = jnp.zeros((n_blocks, CMP_HIDDEN), f32)
        for l in range(CMP_STRIDE):
            rows = x_ref[0, 0, pl.ds(l, n_blocks, stride=CMP_STRIDE), :]
            lo = lo + jnp.dot((rows + pe_ref[l:l + 1, :]).astype(bf16), w1_ref[l * N:(l + 1) * N, :],
                              preferred_element_type=f32)
            m = CMP_STRIDE + l
            hi = hi + jnp.dot((rows + pe_ref[m:m + 1, :]).astype(bf16), w1_ref[m * N:(m + 1) * N, :],
                              preferred_element_type=f32)
        pre = lo + pltpu.roll(hi, n_blocks - 1, 0)
        return jnp.dot(_gelu_tanh(pre).astype(bf16), w2_ref[...], preferred_element_type=f32)

    kc_out[0, 0] = mlp(xk_ref, pek_ref, kw1_ref, kw2_ref).astype(kc_out.dtype)
    vc = mlp(xv_ref, pev_ref, vw1_ref, vw2_ref)
    vcT_out[0, 0] = jnp.concatenate([vc, jnp.zeros_like(vc)], axis=1).T[:N].astype(vcT_out.dtype)


def _nsa_compress(xk, xv, pe_k, pe_v, ck_w1, ck_w2, cv_w1, cv_w2):
    B, G, S, N = xk.shape
    NC = S // CMP_STRIDE
    bf16 = jnp.bfloat16
    xin = pl.BlockSpec((1, 1, S, N), lambda b, g: (b, g, 0, 0))
    return pl.pallas_call(
        _nsa_compress_kernel,
        grid=(B, G),
        in_specs=[xin, xin, _full((CMP_BLOCK, N)), _full((CMP_BLOCK, N)), _full((CMP_BLOCK * N, CMP_HIDDEN)),
                  _full((CMP_HIDDEN, N)), _full((CMP_BLOCK * N, CMP_HIDDEN)), _full((CMP_HIDDEN, N))],
        out_specs=[pl.BlockSpec((1, 1, NC, N), lambda b, g: (b, g, 0, 0)),
                   pl.BlockSpec((1, 1, N, NC), lambda b, g: (b, g, 0, 0))],
        out_shape=[jax.ShapeDtypeStruct((B, G, NC, N), bf16), jax.ShapeDtypeStruct((B, G, N, NC), bf16)],
        compiler_params=_params("parallel", "parallel"),
        name="nsa_compress",
    )(xk, xv, pe_k, pe_v, ck_w1.astype(bf16), ck_w2.astype(bf16), cv_w1.astype(bf16), cv_w2.astype(bf16))


SEL_KEYS = 512
SEL_SLOTS = V7X_SUBLANES
CMP_TILE = 128
RANK_ROWS = 32
RANK_UNROLL = 8


def _nsa_attn_kernel(qT_ref, kc_ref, vcT_ref, ovT_ref, ks_ref, vsT_ref, kw_ref, vwT_ref, gT_ref,
                     o_ref, val_ref, cnt_ref, sel_ref, qa_ref, s_ref, cm_ref, m_ref, acc_ref, part_ref,
                     imp_ref):
    qb = pl.program_id(2)
    t0 = qb * Q_BLOCK
    R, Q = NSA_R, Q_BLOCK
    RQ = R * Q
    f32, bf16 = jnp.float32, jnp.bfloat16
    qT = qT_ref[0, 0, 0]
    lane = lax.broadcasted_iota(jnp.int32, (1, RQ), 1)
    tpos = t0 + lane % Q
    mm = lambda a, b: jnp.dot(a, b, preferred_element_type=f32)

    g = gT_ref[0, 0, 0]
    NC = kc_ref.shape[2]
    NSB = ovT_ref.shape[0]

    span = WINDOW + Q
    sub = lax.broadcasted_iota(jnp.int32, (Q, 1), 0)
    qpos = lane % Q

    def local_branches(rows, aligned_window):
        w0 = pl.multiple_of(t0 - WINDOW, Q) if aligned_window else 0
        s_cmp = mm(kc_ref[0, 0, :rows, :], qT)
        s_win = mm(kw_ref[0, 0, pl.ds(w0, span), :], qT)

        cend = lax.broadcasted_iota(jnp.int32, (rows, 1), 0) * CMP_STRIDE + (CMP_BLOCK - 1)
        mask = cend <= tpos
        sm = jnp.where(mask, s_cmp, NEG)
        e = jnp.where(mask, jnp.exp2(sm - jnp.max(sm, axis=0, keepdims=True)), 0.0)
        den = jnp.sum(e, axis=0, keepdims=True)
        p = e / jnp.where(den > 0.0, den, 1.0)
        o_c = mm(vcT_ref[0, 0, :, :rows], p.astype(bf16))
        psum = p[:, :Q]
        for r in range(1, R):
            psum = psum + p[:, r * Q:(r + 1) * Q]
        ps_hi, ps_lo = _split(psum)
        ov = ovT_ref[:, :rows]
        imp_ref[...] = mm(jnp.concatenate([ov, ov], axis=1), jnp.concatenate([ps_hi, ps_lo], axis=0))

        if aligned_window:
            head = jnp.where(sub > qpos, s_win[:Q], NEG)
            tail = jnp.where(sub <= qpos, s_win[WINDOW:], NEG)
            sm = jnp.concatenate([head, s_win[Q:WINDOW], tail], axis=0)
        else:
            diff = tpos - lax.broadcasted_iota(jnp.int32, (span, 1), 0)
            sm = jnp.where((diff >= 0) & (diff < WINDOW), s_win, NEG)
        p = jnp.exp2(sm - jnp.max(sm, axis=0, keepdims=True))
        pv = mm(vwT_ref[0, 0, :, pl.ds(w0, span)], p.astype(bf16))
        o_w = pv[:HEAD_DIM] / pv[HEAD_DIM:HEAD_DIM + 1]
        part_ref[...] = g[0:1, :] * o_c + g[2:3, :] * o_w

    tile = min(CMP_TILE, NC)
    per_tile = tile * CMP_STRIDE // Q
    n_tiles = NC // tile
    first_aligned = WINDOW // Q
    for i in range(n_tiles):
        lo, hi = i * per_tile, (i + 1) * per_tile
        cuts = [lo, hi] if not lo < first_aligned < hi else [lo, first_aligned, hi]
        for a, b in zip(cuts[:-1], cuts[1:]):
            in_range = (qb >= a) if (i == n_tiles - 1 and b == hi) else ((qb >= a) & (qb < b))
            pl.when(in_range)(functools.partial(local_branches, (i + 1) * tile, a >= first_aligned))
    imp = imp_ref[...]

    tq = t0 + lax.broadcasted_iota(jnp.int32, (1, Q), 1)
    jblk = lax.broadcasted_iota(jnp.int32, (NSB, 1), 0)
    cur = tq // SEL_BLOCK
    forced = (jblk == 0) | (jblk == cur) | (jblk == cur - 1)
    valid = jblk * SEL_BLOCK <= tq
    val = jnp.where(valid, jnp.where(forced, BIG, imp), NEG)
    val_ref[...] = val
    n_live = (t0 + Q - 1) // SEL_BLOCK + 1

    chunk = min(RANK_ROWS, NSB)
    n_rounds = (n_live + RANK_UNROLL - 1) // RANK_UNROLL
    for k in range(NSB // chunk):
        rows = slice(k * chunk, (k + 1) * chunk)

        @pl.when(k * chunk < n_live)
        def _():
            target = val_ref[rows, :]

            def strict_body(u, cnt):
                base = u * RANK_UNROLL
                for i in range(RANK_UNROLL):
                    cnt = cnt + jnp.where(val_ref[pl.ds(base + i, 1), :] > target, 1, 0)
                return cnt

            cnt_ref[rows, :] = lax.fori_loop(0, n_rounds, strict_body, jnp.zeros((chunk, Q), jnp.int32))

        @pl.when(k * chunk >= n_live)
        def _():
            cnt_ref[rows, :] = jnp.full((chunk, Q), N_SELECT, jnp.int32)

    taken =jnp.sum(jnp.where(valid & (cnt_ref[...] < N_SELECT), 1, 0), axis=0, keepdims=True)

    @pl.when(jnp.max(taken) > N_SELECT)
    def _():
        def tie_body(i, cnt):
            row = val_ref[pl.ds(i, 1), :]
            ge = jnp.where(row >= val, 1, 0)
            gt = jnp.where(row > val, 1, 0)
            return cnt + jnp.where(jblk > i, ge, gt)

        cnt_ref[...] = lax.fori_loop(0, n_live, tie_body, jnp.zeros((NSB, Q), jnp.int32))

    bias = jnp.where(cnt_ref[...] < N_SELECT, 0.0, NEG)
    sel_ref[...] = jnp.concatenate([bias] * R, axis=1)

    bps = SEL_SLOTS
    for buf in range(qa_ref.shape[0]):
        qa_ref[buf, :HEAD_DIM, :] = qT
        qa_ref[buf, HEAD_DIM:, :] = jnp.zeros((qa_ref.shape[1] - HEAD_DIM, RQ), bf16)
    m_ref[...] = jnp.full(m_ref.shape, NEG, f32)
    acc_ref[...] = jnp.zeros(acc_ref.shape, f32)

    def scores(kc, slot):
        k0 = pl.multiple_of(kc * SEL_KEYS, SEL_KEYS)
        grp = pl.multiple_of((kc * (SEL_KEYS // SEL_BLOCK)) // bps * bps, bps)
        rows = sel_ref[pl.ds(grp, bps), :]
        qa_ref[slot, HEAD_DIM:HEAD_DIM + 2 * bps, :] = (
            jnp.concatenate([rows, jnp.zeros_like(rows)], axis=0).astype(bf16))
        s = mm(ks_ref[0, 0, pl.ds(k0, SEL_KEYS), :], qa_ref[slot])
        s_ref[slot] = s
        cm_ref[slot] = jnp.max(s, axis=0, keepdims=True)

    def accumulate(kc, s, cm):
        k0 = pl.multiple_of(kc * SEL_KEYS, SEL_KEYS)
        m = m_ref[...]
        m_new = jnp.maximum(m, cm)
        m_ref[...] = m_new
        p = jnp.exp2(s - m_new).astype(bf16)
        pv = mm(vsT_ref[0, 0, :, pl.ds(k0, s.shape[0])], p)
        acc_ref[...] = jnp.exp2(m - m_new) * acc_ref[...] + pv

    last = (t0 + Q - 1) // SEL_KEYS
    scores(0, 0)

    def sel_body(j, carry):
        scores(2 * j + 1, 1)
        accumulate(2 * j, s_ref[0], cm_ref[0])
        scores(2 * j + 2, 0)
        accumulate(2 * j + 1, s_ref[1], cm_ref[1])
        return carry

    lax.fori_loop(0, last // 2, sel_body, 0)

    @pl.when(last % 2 == 1)
    def _():
        scores(last, 1)
        accumulate(last - 1, s_ref[0], cm_ref[0])

    def diagonal_step(live):
        rows = live * Q
        s = s_ref[last % 2, :rows, :]
        edge = jnp.where(sub <= qpos, s[rows - Q:], NEG)
        s = edge if live == 1 else jnp.concatenate([s[:rows - Q], edge], axis=0)
        accumulate(last, s, jnp.max(s, axis=0, keepdims=True))

    per_step = SEL_KEYS // Q
    for v in range(per_step):
        pl.when(qb % per_step == v)(functools.partial(diagonal_step, v + 1))

    o_s = acc_ref[:HEAD_DIM, :] / acc_ref[HEAD_DIM:HEAD_DIM + 1, :]
    oT = part_ref[...] + g[1:2, :] * o_s
    pairs = [jnp.concatenate([oT[:, (2 * i) * Q:(2 * i + 1) * Q], oT[:, (2 * i + 1) * Q:(2 * i + 2) * Q]],
                             axis=0).T for i in range(R // 2)]
    o_ref[...] = jnp.concatenate(pairs, axis=1)


def _nsa_attention(qT, kc, vcT, ks, vsT, kw, vwT, gT):
    B, G, NQB, _, RQ = qT.shape
    S = ks.shape[2]
    NC = kc.shape[2]
    NSB = S // SEL_BLOCK
    c = np.arange(NC)[None, :] * CMP_STRIDE
    j = np.arange(NSB)[:, None] * SEL_BLOCK
    ovT = jnp.asarray((c <= j + SEL_BLOCK - 1) & (c + CMP_BLOCK - 1 >= j), jnp.bfloat16)
    per_q = lambda rows: pl.BlockSpec((1, 1, 1, rows, RQ), lambda b, g, q: (b, g, q, 0, 0))
    per_g = lambda d0, d1: pl.BlockSpec((1, 1, d0, d1), lambda b, g, q: (b, g, 0, 0))
    return pl.pallas_call(
        _nsa_attn_kernel,
        grid=(B, G, NQB),
        in_specs=[per_q(HEAD_DIM), per_g(NC, HEAD_DIM), per_g(HEAD_DIM, NC), _full((NSB, NC)),
                  per_g(S, V7X_LANES), per_g(V_ROWS, S), per_g(S, HEAD_DIM), per_g(V_ROWS, S),
                  per_q(V7X_SUBLANES)],
        out_specs=pl.BlockSpec((Q_BLOCK, NSA_R * HEAD_DIM), lambda b, g, q: (b * NQB + q, g)),
        out_shape=jax.ShapeDtypeStruct((B * S, NSA_DIM), jnp.float32),
        scratch_shapes=[pltpu.VMEM((NSB, Q_BLOCK), jnp.float32), pltpu.VMEM((NSB, Q_BLOCK), jnp.int32),
                        pltpu.VMEM((NSB, RQ), jnp.float32),
                        pltpu.VMEM((2, V7X_LANES, RQ), jnp.bfloat16),
                        pltpu.VMEM((2, SEL_KEYS, RQ), jnp.float32), pltpu.VMEM((2, 1, RQ), jnp.float32),
                        pltpu.VMEM((1, RQ), jnp.float32),
                        pltpu.VMEM((V_ROWS, RQ), jnp.float32), pltpu.VMEM((HEAD_DIM, RQ), jnp.float32),
                        pltpu.VMEM((NSB, Q_BLOCK), jnp.float32)],
        compiler_params=_params("parallel", "parallel", "arbitrary"),
        name="nsa_attention",
    )(qT, kc, vcT, ovT, ks, vsT, kw, vwT, gT)


def _nsa_branch(x2d, w_nsa, pe_k, pe_v, ck_w1, ck_w2, cv_w1, cv_w2, batch, seq_len):
    qT, kc_in, vc_in, ks, vsT, kw, vwT, gT = _nsa_prep(x2d, w_nsa, batch, seq_len)
    kc, vcT = _nsa_compress(kc_in, vc_in, pe_k, pe_v, ck_w1, ck_w2, cv_w1, cv_w2)
    return _nsa_attention(qT, kc, vcT, ks, vsT, kw, vwT, gT)


def _merge_kernel(x_ref, ya_ref, yb_ref, wg_ref, pa_ref, pb_ref, wo_ref, g_ref, b_ref, o_ref):
    bf16 = jnp.bfloat16
    mm = lambda a, w: jnp.dot(a.astype(bf16), w, preferred_element_type=jnp.float32)
    half = x_ref.shape[0] // 2
    parts = [slice(0, half), slice(half, 2 * half)]
    xs = [x_ref[rows, :] for rows in parts]
    logits = [mm(x, wg_ref[...]) for x in xs]
    pa = [mm(ya_ref[rows, :], pa_ref[...]) for rows in parts]
    pb = [mm(yb_ref[rows, :], pb_ref[...]) for rows in parts]
    mixed = []
    for lg, a, b in zip(logits, pa, pb):
        gates = jax.nn.sigmoid(lg)
        mixed.append(mm(gates[:, :D_MODEL] * a + gates[:, D_MODEL:] * b, wo_ref[...]))
    for rows, x, mx in zip(parts, xs, mixed):
        o_ref[rows, :] = _layer_norm(ALPHA * x + mx, g_ref[...], b_ref[...])


def _merge(x2d, y_a, y_b, w_gate, p_a, p_b, w_o, ln_g, ln_b, tm=ROW_TILE):
    M = x2d.shape[0]
    bf16 = jnp.bfloat16
    rows = lambda w: pl.BlockSpec((tm, w), lambda i: (i, 0))
    return pl.pallas_call(
        _merge_kernel,
        grid=(M // tm,),
        in_specs=[rows(D_MODEL), rows(RWKV_DIM), rows(NSA_DIM), _full((D_MODEL, 2 * D_MODEL)),
                  _full((RWKV_DIM, D_MODEL)), _full((NSA_DIM, D_MODEL)), _full((D_MODEL, D_MODEL)),
                  _full((1, D_MODEL)), _full((1, D_MODEL))],
        out_specs=rows(D_MODEL),
        out_shape=jax.ShapeDtypeStruct((M, D_MODEL), jnp.float32),
        compiler_params=_params("parallel"),
        name="merge",
    )(x2d, y_a, y_b, w_gate, p_a.astype(bf16), p_b.astype(bf16), w_o.astype(bf16),
      ln_g.reshape(1, -1), ln_b.reshape(1, -1))


def _mem_kv_kernel(mem_ref, wk_ref, wv_ref, k_out, v_out):
    m = mem_ref[...].astype(jnp.bfloat16)
    k_out[...] = jnp.dot(m, wk_ref[...], preferred_element_type=jnp.float32).astype(k_out.dtype)
    v_out[...] = jnp.dot(m, wv_ref[...], preferred_element_type=jnp.float32).astype(v_out.dtype)


def _mem_kv(mem2d, wk, wv):
    M = mem2d.shape[0]
    bf16 = jnp.bfloat16
    out = jax.ShapeDtypeStruct((M, D_MODEL), bf16)
    return pl.pallas_call(
        _mem_kv_kernel,
        grid=(1,),
        in_specs=[_full((M, D_MODEL)), _full((D_MODEL, D_MODEL)), _full((D_MODEL, D_MODEL))],
        out_specs=[_full((M, D_MODEL))] * 2,
        out_shape=[out, out],
        compiler_params=_params("arbitrary"),
        name="mem_kv",
    )(mem2d, wk.astype(bf16), wv.astype(bf16))


def _xattn_kernel(x_ref, k_ref, v_ref, wq_ref, wo_ref, g_ref, b_ref, o_ref):
    bf16, f32 = jnp.bfloat16, jnp.float32
    cols = [slice(h * X_HEAD_DIM, (h + 1) * X_HEAD_DIM) for h in range(X_HEADS)]
    half = x_ref.shape[0] // 2
    parts = [slice(0, half), slice(half, 2 * half)]
    xs = [x_ref[rows, :] for rows in parts]
    qs = [jnp.dot(x.astype(bf16), wq_ref[...], preferred_element_type=f32).astype(bf16) for x in xs]
    scores = [[lax.dot_general(q[:, sl], k_ref[:, sl], (((1,), (1,)), ((), ())),
                               preferred_element_type=f32) * (X_HEAD_DIM ** -0.5) for sl in cols]
              for q in qs]
    outs = []
    for per_head in scores:
        heads = []
        for s, sl in zip(per_head, cols):
            p = jnp.exp(s - jnp.max(s, axis=-1, keepdims=True))
            p = p / jnp.sum(p, axis=-1, keepdims=True)
            heads.append(jnp.dot(p.astype(bf16), v_ref[:, sl], preferred_element_type=f32))
        o = jnp.concatenate(heads, axis=1).astype(bf16)
        outs.append(jnp.dot(o, wo_ref[...], preferred_element_type=f32))
    for rows, x, xa in zip(parts, xs, outs):
        o_ref[rows, :] = _layer_norm(ALPHA * x + xa, g_ref[...], b_ref[...])


def _xattn(x2d, k_mem, v_mem, wq, wo, ln_g, ln_b, seq_len, mem_len, tm=ROW_TILE):
    M = x2d.shape[0]
    bf16 = jnp.bfloat16
    seq_tiles = seq_len // tm
    rows = pl.BlockSpec((tm, D_MODEL), lambda i: (i, 0))
    mem_spec = pl.BlockSpec((mem_len, D_MODEL), lambda i: (i // seq_tiles, 0))
    return pl.pallas_call(
        _xattn_kernel,
        grid=(M // tm,),
        in_specs=[rows, mem_spec, mem_spec, _full((D_MODEL, D_MODEL)), _full((D_MODEL, D_MODEL)),
                  _full((1, D_MODEL)), _full((1, D_MODEL))],
        out_specs=rows,
        out_shape=jax.ShapeDtypeStruct((M, D_MODEL), jnp.float32),
        compiler_params=_params("parallel"),
        name="xattn",
    )(x2d, k_mem, v_mem, wq.astype(bf16), wo.astype(bf16), ln_g.reshape(1, -1), ln_b.reshape(1, -1))


FFN_CHUNK = D_FF


def _ffn_kernel(seq_tiles, x_ref, xp_ref, wup_ref, cw_ref, cb_ref, wdn_ref, g_ref, b_ref, o_ref):
    bf16, f32 = jnp.bfloat16, jnp.float32
    i = pl.program_id(0)
    tm = x_ref.shape[0]
    H = V7X_SUBLANES
    x = x_ref[...]
    xprev = jnp.where(i % seq_tiles == 0, 0.0, xp_ref[...])
    xe = jnp.concatenate([xprev, x], axis=0).astype(bf16)

    def conv(cols):
        h = jnp.dot(xe, wup_ref[:, cols], preferred_element_type=f32)
        w = cw_ref[:, cols]
        return (h[H - 2:H - 2 + tm] * w[0:1] + h[H - 1:H - 1 + tm] * w[1:2] + h[H:] * w[2:3]
                + cb_ref[:, cols])

    acc = jnp.zeros((tm, D_MODEL), f32)
    for c in range(D_FF // FFN_CHUNK):
        gate = conv(slice(c * FFN_CHUNK, (c + 1) * FFN_CHUNK))
        val = conv(slice(D_FF + c * FFN_CHUNK, D_FF + (c + 1) * FFN_CHUNK))
        act = (gate * jax.nn.sigmoid(gate) * val).astype(bf16)
        acc = acc + jnp.dot(act, wdn_ref[c * FFN_CHUNK:(c + 1) * FFN_CHUNK, :], preferred_element_type=f32)
    o_ref[...] = _layer_norm(ALPHA * x + acc, g_ref[...], b_ref[...])


def _ffn(x2d, w_up, conv_w, conv_b, w_down, ln_g, ln_b, seq_len, tm=ROW_TILE):
    M = x2d.shape[0]
    bf16 = jnp.bfloat16
    seq_tiles = seq_len // tm
    blocks_per_tile = tm // V7X_SUBLANES
    rows = pl.BlockSpec((tm, D_MODEL), lambda i: (i, 0))
    once = lambda shape: pl.BlockSpec(shape, lambda i: (0,) * len(shape), pipeline_mode=pl.Buffered(1))
    return pl.pallas_call(
        functools.partial(_ffn_kernel, seq_tiles),
        grid=(M // tm,),
        in_specs=[rows,
                  pl.BlockSpec((V7X_SUBLANES, D_MODEL), lambda i: (jnp.maximum(i * blocks_per_tile - 1, 0), 0)),
                  once((D_MODEL, 2 * D_FF)), _full((3, 2 * D_FF)), _full((1, 2 * D_FF)),
                  once((D_FF, D_MODEL)), _full((1, D_MODEL)), _full((1, D_MODEL))],
        out_specs=rows,
        out_shape=jax.ShapeDtypeStruct((M, D_MODEL), jnp.float32),
        compiler_params=_params("parallel"),
        name="ffn",
    )(x2d, x2d, w_up.astype(bf16), conv_w, conv_b.reshape(1, -1), w_down.astype(bf16),
      ln_g.reshape(1, -1), ln_b.reshape(1, -1))


def kernel(x, mem, w_in, rwkv_mu, rwkv_w0, rwkv_w2, rwkv_a0, rwkv_a2, rwkv_k_k, rwkv_k_a, rwkv_r_k, rwkv_gn_g, rwkv_gn_b, nsa_pe_k, nsa_pe_v, nsa_ck_w1, nsa_ck_w2, nsa_cv_w1, nsa_cv_w2, merge_p_a, merge_p_b, mix_w_o, ln1_g, ln1_b, xa_wq, xa_wk, xa_wv, xa_wo, ln2_g, ln2_b, ffn_w_up, ffn_conv_w, ffn_conv_b, ffn_w_down, ln3_g, ln3_b):
    B, S, _ = x.shape
    mem_len = mem.shape[1]
    if x.shape[2] != D_MODEL or S % PREP_TILE or S % SEL_KEYS or S < WINDOW + Q_BLOCK:
        raise ValueError(f"unsupported input shape {x.shape}")
    bf16 = jnp.bfloat16
    x2d = x.reshape(B * S, D_MODEL)
    for l in range(DEPTH):
        w = w_in[l]
        w_rwkv = w[:, :RWKV_COLS].astype(bf16)
        w_nsa = _nsa_weight(w[:, RWKV_COLS:RWKV_COLS + NSA_COLS])
        w_gate = w[:, RWKV_COLS + NSA_COLS:].astype(bf16)
        y_a = _rwkv_time_mix(x2d, w_rwkv, rwkv_mu[l], rwkv_w0[l], rwkv_w2[l], rwkv_a0[l], rwkv_a2[l],
                             rwkv_k_k[l], rwkv_k_a[l], rwkv_r_k[l], rwkv_gn_g[l], rwkv_gn_b[l], B, S)
        y_b = _nsa_branch(x2d, w_nsa, nsa_pe_k[l], nsa_pe_v[l], nsa_ck_w1[l], nsa_ck_w2[l],
                          nsa_cv_w1[l], nsa_cv_w2[l], B, S)
        x2d = _merge(x2d, y_a, y_b, w_gate, merge_p_a[l], merge_p_b[l], mix_w_o[l], ln1_g[l], ln1_b[l])
        k_mem, v_mem = _mem_kv(mem.reshape(B * mem_len, D_MODEL), xa_wk[l], xa_wv[l])
        x2d = _xattn(x2d, k_mem, v_mem, xa_wq[l], xa_wo[l], ln2_g[l], ln2_b[l], S, mem_len)
        x2d = _ffn(x2d, ffn_w_up[l], ffn_conv_w[l], ffn_conv_b[l], ffn_w_down[l], ln3_g[l], ln3_b[l], S)
    return x2d.reshape(B, S, D_MODEL)
```
